```python
import math
import jax
import jax.numpy as jnp
from jax import lax
import numpy as np

D_MODEL = 1024
BATCH = 16
SEQ = 256
DEPTH = 4
DEC_BATCH = 2
DEC_SEQ = 4096
PAST_LEN = 256

GRID_W = 64
N_MIXERS = 3
N_A_LAYERS = (DEPTH + 2) // 3
N_B_LAYERS = (DEPTH + 1) // 3
N_C_LAYERS = DEPTH // 3
NORM_EPS = 1e-6

S5_GROUP_CH = 16
S5_GROUPS = D_MODEL // S5_GROUP_CH
S5_STATE = 64
S5_DT_MIN = 1e-3
S5_DT_MAX = 1e-1

HG_DK = 128
HG_HEADS = D_MODEL // HG_DK
HG_DV = D_MODEL // HG_HEADS
HG_CHUNK = 64

SSD_INNER = 2 * D_MODEL
SSD_HEADDIM = 64
SSD_HEADS = SSD_INNER // SSD_HEADDIM
SSD_GROUPS = 4
SSD_HPG = SSD_HEADS // SSD_GROUPS
SSD_STATE = 128
SSD_CONV = 5
SSD_CHUNK = 64
SSD_XBC = SSD_INNER + 2 * SSD_GROUPS * SSD_STATE
SSD_IN = SSD_INNER + SSD_XBC + 2 * SSD_HEADS
SSD_DT_MIN = 1e-3
SSD_DT_MAX = 1e-1

N_EXPERTS = 16
EXPERT_FF = 2 * D_MODEL
EC_CAPACITY = 2

kernel_name = 'hybrid_s5_hgrn2_ssd_ecmoe_diffusion_step'


def rmsnorm(x, g):
    xf = x.astype(jnp.float32)
    xf = xf * lax.rsqrt(jnp.mean(xf * xf, axis=-1, keepdims=True) + NORM_EPS)
    return (xf * g.astype(jnp.float32)).astype(x.dtype)


def grid_pos_embed(n_tokens):
    rows = n_tokens // GRID_W
    quarter = D_MODEL // 4
    omega = 1.0 / (10000.0 ** (jnp.arange(quarter, dtype=jnp.float32) / quarter))
    r = jnp.arange(rows, dtype=jnp.float32)[:, None] * omega
    cl = jnp.arange(GRID_W, dtype=jnp.float32)[:, None] * omega
    emb_r = jnp.concatenate([jnp.sin(r), jnp.cos(r)], axis=-1)
    emb_c = jnp.concatenate([jnp.sin(cl), jnp.cos(cl)], axis=-1)
    emb = jnp.concatenate([jnp.broadcast_to(emb_r[:, None], (rows, GRID_W, D_MODEL // 2)),
                           jnp.broadcast_to(emb_c[None], (rows, GRID_W, D_MODEL // 2))], axis=-1)
    return emb.reshape(rows * GRID_W, D_MODEL)


def _linear_combine(left, right):
    a1, b1 = left
    a2, b2 = right
    return a1 * a2, a2 * b1 + b2


def s5_mixer(h, h0_re, h0_im, lam_re, lam_im, log_dt, b_re, b_im, c_re, c_im, d_skip, w_glu, b_glu):
    bsz, t, _ = h.shape
    hf = h.astype(jnp.float32)
    u = hf.reshape(bsz, t, S5_GROUPS, S5_GROUP_CH).astype(jnp.complex64)
    h0 = lax.complex(h0_re.astype(jnp.float32), h0_im.astype(jnp.float32))
    y = d_skip.astype(jnp.float32) * hf
    fin = []
    for d in range(2):
        lam = lax.complex(lam_re[d].astype(jnp.float32), lam_im[d].astype(jnp.float32))
        dt = jnp.exp(log_dt[d].astype(jnp.float32))[:, None]
        lam_bar = jnp.exp(lam * dt)
        b_bar = ((lam_bar - 1.0) / lam)[..., None] * lax.complex(b_re[d].astype(jnp.float32), b_im[d].astype(jnp.float32))
        cmat = lax.complex(c_re[d].astype(jnp.float32), c_im[d].astype(jnp.float32))
        bu = jnp.einsum('gph,btgh->btgp', b_bar, u)
        acum, hs = lax.associative_scan(_linear_combine, (jnp.broadcast_to(lam_bar, bu.shape), bu),
                                        reverse=(d == 1), axis=1)
        hs = hs + acum * h0[:, d][:, None]
        y = y + jnp.einsum('ghp,btgp->btgh', cmat, hs).real.reshape(bsz, t, D_MODEL)
        fin.append(hs[:, 0] if d == 1 else hs[:, -1])
    fin = jnp.stack(fin, axis=1)
    yg = jax.nn.gelu(y).astype(h.dtype)
    ab = yg @ w_glu + b_glu
    out = ab[..., :D_MODEL] * jax.nn.sigmoid(ab[..., D_MODEL:])
    return out, jnp.real(fin), jnp.imag(fin)


def gla_chunked(q, k, v, logf, s0):
    bsz, t, nh, _ = q.shape
    n = t // HG_CHUNK

    def to_chunks(z):
        return jnp.moveaxis(z.reshape(bsz, n, HG_CHUNK, *z.shape[2:]), 1, 0)

    mask = jnp.tril(jnp.ones((HG_CHUNK, HG_CHUNK), dtype=bool))[:, :, None, None]

    def step(s, inp):
        qi, ki, vi, gi = inp
        cum = jnp.cumsum(gi, axis=1)
        seg = jnp.where(mask, cum[:, :, None] - cum[:, None], -jnp.inf)
        scores = jnp.einsum('bthk,bshk,btshk->bhts', qi, ki, jnp.exp(seg))
        o = (jnp.einsum('bhts,bshv->bthv', scores, vi)
             + jnp.einsum('bthk,bhkv->bthv', qi * jnp.exp(cum), s))
        last = cum[:, -1]
        s_new = (jnp.exp(last)[..., None] * s
                 + jnp.einsum('bshk,bshv->bhkv', ki * jnp.exp(last[:, None] - cum), vi))
        return s_new, o

    s_fin, oc = lax.scan(step, s0, (to_chunks(q), to_chunks(k), to_chunks(v), to_chunks(logf)))
    o = jnp.moveaxis(oc, 0, 1).reshape(bsz, t, nh, v.shape[-1])
    return o, s_fin


def hgrn2_mixer(h, s0, lb, w_qig, w_f, b_f, g_norm, w_o):
    bsz, t, _ = h.shape
    q, v, gate = jnp.split((h @ w_qig).astype(jnp.float32), 3, axis=-1)
    q = q.reshape(bsz, t, HG_HEADS, HG_DK)
    v = v.reshape(bsz, t, HG_HEADS, HG_DV)
    outs, fin = [], []
    for d in range(2):
        lbd = lb[d].reshape(HG_HEADS, HG_DK)
        f = lbd + (1.0 - lbd) * jax.nn.sigmoid((h @ w_f[d] + b_f[d]).astype(jnp.float32).reshape(bsz, t, HG_HEADS, HG_DK))
        args = (q, 1.0 - f, v, jnp.log(f))
        if d == 1:
            args = tuple(jnp.flip(a, axis=1) for a in args)
        od, sd = gla_chunked(*args, s0[:, d].astype(jnp.float32))
        outs.append(jnp.flip(od, axis=1) if d == 1 else od)
        fin.append(sd)
    o = rmsnorm(outs[0] + outs[1], g_norm) * jax.nn.silu(gate.reshape(bsz, t, HG_HEADS, HG_DV))
    out = o.reshape(bsz, t, D_MODEL).astype(h.dtype) @ w_o
    return out, jnp.stack(fin, axis=1)


def depthwise_conv_centred(x, w, b):
    k = w.shape[0]
    y = lax.conv_general_dilated(x, w[:, None, :], window_strides=(1,), padding=[(k // 2, k // 2)],
                                 dimension_numbers=('NWC', 'WIO', 'NWC'), feature_group_count=x.shape[-1])
    return y + b


def ssd_chunked(x, dt, bm, cm, a, h0):
    bsz, t = x.shape[:2]
    n = t // SSD_CHUNK

    def chunk(z):
        return z.reshape(bsz, n, SSD_CHUNK, *z.shape[2:])

    x, dt, bm, cm = chunk(x), chunk(dt), chunk(bm), chunk(cm)
    cum = jnp.cumsum(dt * a, axis=2)
    mask = jnp.tril(jnp.ones((SSD_CHUNK, SSD_CHUNK), dtype=bool))[:, :, None, None]
    lmat = jnp.exp(jnp.where(mask, cum[:, :, :, None] - cum[:, :, None], -jnp.inf))
    cb = jnp.einsum('bnlgd,bnmgd->bnlmg', cm, bm)
    xdt = x * dt[..., None]
    y = jnp.einsum('bnlmgj,bnmgjp->bnlgjp', cb[..., None] * lmat, xdt)
    dec_end = jnp.exp(cum[:, :, -1:] - cum)
    chunk_states = jnp.einsum('bnmgd,bnmgjp->bngjpd', bm, xdt * dec_end[..., None])
    chunk_decay = jnp.exp(cum[:, :, -1])

    def step(hc, inp):
        dec, st = inp
        return dec[..., None, None] * hc + st, hc

    h_fin, h_start = lax.scan(step, h0, (jnp.moveaxis(chunk_decay, 1, 0), jnp.moveaxis(chunk_states, 1, 0)))
    h_start = jnp.moveaxis(h_start, 0, 1)
    y = y + jnp.einsum('bnlgd,bngjpd->bnlgjp', cm, h_start) * jnp.exp(cum)[..., None]
    return y.reshape(bsz, t, *y.shape[3:]), h_fin


def ssd_mixer(h, h0, w_in, conv_w, conv_b, dt_bias, a_log, d_skip, norm_g, w_out):
    bsz, t, _ = h.shape
    zxbcdt = h @ w_in
    z = zxbcdt[..., :SSD_INNER].astype(jnp.float32)
    xbc = jax.nn.silu(depthwise_conv_centred(zxbcdt[..., SSD_INNER:SSD_INNER + SSD_XBC], conv_w, conv_b)).astype(jnp.float32)
    dt_raw = zxbcdt[..., SSD_INNER + SSD_XBC:].astype(jnp.float32).reshape(bsz, t, 2, SSD_GROUPS, SSD_HPG)
    x = xbc[..., :SSD_INNER].reshape(bsz, t, SSD_GROUPS, SSD_HPG, SSD_HEADDIM)
    bm = xbc[..., SSD_INNER:SSD_INNER + SSD_GROUPS * SSD_STATE].reshape(bsz, t, SSD_GROUPS, SSD_STATE)
    cm = xbc[..., SSD_INNER + SSD_GROUPS * SSD_STATE:].reshape(bsz, t, SSD_GROUPS, SSD_STATE)
    dt = jax.nn.softplus(dt_raw + dt_bias.astype(jnp.float32).reshape(2, SSD_GROUPS, SSD_HPG))
    a = -jnp.exp(a_log.astype(jnp.float32)).reshape(2, SSD_GROUPS, SSD_HPG)
    y = d_skip.astype(jnp.float32).reshape(SSD_GROUPS, SSD_HPG)[..., None] * x
    fin = []
    for d in range(2):
        args = (x, dt[:, :, d], bm, cm)
        if d == 1:
            args = tuple(jnp.flip(v, axis=1) for v in args)
        hd0 = h0[:, d].astype(jnp.float32).reshape(bsz, SSD_GROUPS, SSD_HPG, SSD_HEADDIM, SSD_STATE)
        yd, hd = ssd_chunked(*args, a[d], hd0)
        y = y + (jnp.flip(yd, axis=1) if d == 1 else yd)
        fin.append(hd.reshape(bsz, SSD_HEADS, SSD_HEADDIM, SSD_STATE))
    y = rmsnorm(y.reshape(bsz, t, SSD_INNER) * jax.nn.silu(z), norm_g)
    out = y.astype(h.dtype) @ w_out
    return out, jnp.stack(fin, axis=1)


def expert_choice_moe(h, w_router, w_gate, w_up, w_down):
    bsz, t, _ = h.shape
    cap = EC_CAPACITY * t // N_EXPERTS
    aff = jax.nn.softmax((h @ w_router).astype(jnp.float32), axis=-1)
    g, idx = lax.top_k(jnp.swapaxes(aff, 1, 2), cap)
    xs = jax.vmap(lambda hb, ib: hb[ib])(h, idx)
    hid = jax.nn.silu(jnp.einsum('becd,edf->becf', xs, w_gate)) * jnp.einsum('becd,edf->becf', xs, w_up)
    ys = jnp.einsum('becf,efd->becd', hid, w_down) * g[..., None].astype(h.dtype)
    return jax.vmap(lambda yb, ib: jnp.zeros((t, D_MODEL), yb.dtype).at[ib.reshape(-1)].add(yb.reshape(-1, D_MODEL)))(ys, idx)


def trunk(x, cond, st_s5_re, st_s5_im, st_hg, st_ssd, W):
    lb_table = jnp.cumsum(jax.nn.softmax(W['hg_lb_logits'].astype(jnp.float32), axis=1), axis=1)
    lb_table = lb_table - lb_table[:, :1]
    silu_c = jax.nn.silu(cond)
    fr, fi, fh, fs = [], [], [], []
    for i in range(DEPTH):
        mod = (silu_c @ W['w_ada'][i] + W['b_ada'][i])[:, None, :]
        sh1, sc1, g1, sh2, sc2, g2 = jnp.split(mod, 6, axis=-1)
        h = rmsnorm(x, W['norm_mix'][i]) * (1.0 + sc1) + sh1
        kind, j = i % N_MIXERS, i // N_MIXERS
        if kind == 0:
            y, r, im = s5_mixer(h, st_s5_re[:, j], st_s5_im[:, j], W['s5_lam_re'][j], W['s5_lam_im'][j],
                                W['s5_log_dt'][j], W['s5_b_re'][j], W['s5_b_im'][j], W['s5_c_re'][j],
                                W['s5_c_im'][j], W['s5_d'][j], W['s5_w_glu'][j], W['s5_b_glu'][j])
            fr.append(r)
            fi.append(im)
        elif kind == 1:
            y, s = hgrn2_mixer(h, st_hg[:, j], lb_table[:, i], W['hg_w_qig'][j], W['hg_w_f'][j],
                               W['hg_b_f'][j], W['hg_norm'][j], W['hg_w_o'][j])
            fh.append(s)
        else:
            y, s = ssd_mixer(h, st_ssd[:, j], W['ssd_w_in'][j], W['ssd_conv_w'][j], W['ssd_conv_b'][j],
                             W['ssd_dt_bias'][j], W['ssd_a_log'][j], W['ssd_d'][j], W['ssd_norm'][j],
                             W['ssd_w_out'][j])
            fs.append(s)
        x = x + g1 * y
        h = rmsnorm(x, W['norm_ffn'][i]) * (1.0 + sc2) + sh2
        x = x + g2 * expert_choice_moe(h, W['moe_router'][i], W['moe_w_gate'][i], W['moe_w_up'][i], W['moe_w_down'][i])
    return (rmsnorm(x, W['norm_final']), jnp.stack(fr, axis=1), jnp.stack(fi, axis=1),
            jnp.stack(fh, axis=1), jnp.stack(fs, axis=1))


def setup_inputs(seed: int = 0) -> dict:
    key = jax.random.key(seed)
    ks = iter(jax.random.split(key, 48))

    def nrm(shape, scale):
        return scale * jax.random.normal(next(ks), shape, jnp.float32)

    def unif(shape, lo, hi):
        return jax.random.uniform(next(ks), shape, jnp.float32, lo, hi)

    D = D_MODEL
    s5_n = jnp.arange(S5_STATE, dtype=jnp.float32)
    ssd_dt = jnp.exp(unif((N_C_LAYERS, 2, SSD_HEADS), math.log(SSD_DT_MIN), math.log(SSD_DT_MAX)))
    return {
        'x_prompt': nrm((BATCH, SEQ, D), 1.0),
        'x_sample': nrm((DEC_BATCH, DEC_SEQ, D), 1.0),
        'state_s5_re': nrm((DEC_BATCH, N_A_LAYERS, 2, S5_GROUPS, S5_STATE), 0.1),
        'state_s5_im': nrm((DEC_BATCH, N_A_LAYERS, 2, S5_GROUPS, S5_STATE), 0.1),
        'state_hgrn': nrm((DEC_BATCH, N_B_LAYERS, 2, HG_HEADS, HG_DK, HG_DV), 0.5),
        'state_ssd': nrm((DEC_BATCH, N_C_LAYERS, 2, SSD_HEADS, SSD_HEADDIM, SSD_STATE), 0.1),
        'c': nrm((DEC_BATCH, D), 1.0),
        'c_ctx': nrm((D,), 1.0),
        'w_ada': nrm((DEPTH, D, 6 * D), 0.5 * D ** -0.5),
        'b_ada': nrm((DEPTH, 6 * D), 0.02),
        'norm_mix': 1.0 + nrm((DEPTH, D), 0.02),
        'norm_ffn': 1.0 + nrm((DEPTH, D), 0.02),
        'norm_final': 1.0 + nrm((D,), 0.02),
        's5_lam_re': -0.5 + nrm((N_A_LAYERS, 2, S5_GROUPS, S5_STATE), 0.01),
        's5_lam_im': math.pi * s5_n + nrm((N_A_LAYERS, 2, S5_GROUPS, S5_STATE), 0.01),
        's5_log_dt': unif((N_A_LAYERS, 2, S5_GROUPS), math.log(S5_DT_MIN), math.log(S5_DT_MAX)),
        's5_b_re': nrm((N_A_LAYERS, 2, S5_GROUPS, S5_STATE, S5_GROUP_CH), (2 * S5_GROUP_CH) ** -0.5),
        's5_b_im': nrm((N_A_LAYERS, 2, S5_GROUPS, S5_STATE, S5_GROUP_CH), (2 * S5_GROUP_CH) ** -0.5),
        's5_c_re': nrm((N_A_LAYERS, 2, S5_GROUPS, S5_GROUP_CH, S5_STATE), (2 * S5_STATE) ** -0.5),
        's5_c_im': nrm((N_A_LAYERS, 2, S5_GROUPS, S5_GROUP_CH, S5_STATE), (2 * S5_STATE) ** -0.5),
        's5_d': nrm((N_A_LAYERS, D), 1.0),
        's5_w_glu': nrm((N_A_LAYERS, D, 2 * D), D ** -0.5),
        's5_b_glu': nrm((N_A_LAYERS, 2 * D), 0.02),
        'hg_w_qig': nrm((N_B_LAYERS, D, 3 * D), D ** -0.5),
        'hg_w_f': nrm((N_B_LAYERS, 2, D, HG_HEADS * HG_DK), D ** -0.5),
        'hg_b_f': nrm((N_B_LAYERS, 2, HG_HEADS * HG_DK), 0.1),
        'hg_lb_logits': nrm((2, DEPTH, HG_HEADS * HG_DK), 0.5),
        'hg_norm': 1.0 + nrm((N_B_LAYERS, HG_DV), 0.02),
        'hg_w_o': nrm((N_B_LAYERS, D, D), D ** -0.5),
        'ssd_w_in': nrm((N_C_LAYERS, D, SSD_IN), D ** -0.5),
        'ssd_conv_w': nrm((N_C_LAYERS, SSD_CONV, SSD_XBC), SSD_CONV ** -0.5),
        'ssd_conv_b': nrm((N_C_LAYERS, SSD_XBC), 0.02),
        'ssd_dt_bias': ssd_dt + jnp.log(-jnp.expm1(-ssd_dt)),
        'ssd_a_log': jnp.log(unif((N_C_LAYERS, 2, SSD_HEADS), 1.0, 16.0)),
        'ssd_d': 1.0 + nrm((N_C_LAYERS, SSD_HEADS), 0.1),
        'ssd_norm': 1.0 + nrm((N_C_LAYERS, SSD_INNER), 0.02),
        'ssd_w_out': nrm((N_C_LAYERS, SSD_INNER, D), SSD_INNER ** -0.5),
        'moe_router': nrm((DEPTH, D, N_EXPERTS), D ** -0.5),
        'moe_w_gate': nrm((DEPTH, N_EXPERTS, D, EXPERT_FF), D ** -0.5),
        'moe_w_up': nrm((DEPTH, N_EXPERTS, D, EXPERT_FF), D ** -0.5),
        'moe_w_down': nrm((DEPTH, N_EXPERTS, EXPERT_FF, D), EXPERT_FF ** -0.5),
    }


def reference(x_prompt, x_sample, state_s5_re, state_s5_im, state_hgrn, state_ssd, c, c_ctx,
              w_ada, b_ada, norm_mix, norm_ffn, norm_final,
              s5_lam_re, s5_lam_im, s5_log_dt, s5_b_re, s5_b_im, s5_c_re, s5_c_im, s5_d, s5_w_glu, s5_b_glu,
              hg_w_qig, hg_w_f, hg_b_f, hg_lb_logits, hg_norm, hg_w_o,
              ssd_w_in, ssd_conv_w, ssd_conv_b, ssd_dt_bias, ssd_a_log, ssd_d, ssd_norm, ssd_w_out,
              moe_router, moe_w_gate, moe_w_up, moe_w_down):
    W = dict(w_ada=w_ada, b_ada=b_ada, norm_mix=norm_mix, norm_ffn=norm_ffn, norm_final=norm_final,
             s5_lam_re=s5_lam_re, s5_lam_im=s5_lam_im, s5_log_dt=s5_log_dt, s5_b_re=s5_b_re, s5_b_im=s5_b_im,
             s5_c_re=s5_c_re, s5_c_im=s5_c_im, s5_d=s5_d, s5_w_glu=s5_w_glu, s5_b_glu=s5_b_glu,
             hg_w_qig=hg_w_qig, hg_w_f=hg_w_f, hg_b_f=hg_b_f, hg_lb_logits=hg_lb_logits, hg_norm=hg_norm,
             hg_w_o=hg_w_o, ssd_w_in=ssd_w_in, ssd_conv_w=ssd_conv_w, ssd_conv_b=ssd_conv_b,
             ssd_dt_bias=ssd_dt_bias, ssd_a_log=ssd_a_log, ssd_d=ssd_d, ssd_norm=ssd_norm, ssd_w_out=ssd_w_out,
             moe_router=moe_router, moe_w_gate=moe_w_gate, moe_w_up=moe_w_up, moe_w_down=moe_w_down)
    bp = x_prompt.shape[0]
    z_s5 = jnp.zeros((bp, N_A_LAYERS, 2, S5_GROUPS, S5_STATE), jnp.float32)
    z_hg = jnp.zeros((bp, N_B_LAYERS, 2, HG_HEADS, HG_DK, HG_DV), jnp.float32)
    z_ssd = jnp.zeros((bp, N_C_LAYERS, 2, SSD_HEADS, SSD_HEADDIM, SSD_STATE), jnp.float32)
    y_prompt, new_s5_re, new_s5_im, new_hgrn, new_ssd = trunk(x_prompt, c_ctx[None, :], z_s5, z_s5, z_hg, z_ssd, W)
    x_lat = x_sample + grid_pos_embed(x_sample.shape[1]).astype(x_sample.dtype)[None]
    y_sample, _, _, _, _ = trunk(x_lat, c, state_s5_re, state_s5_im, state_hgrn, state_ssd, W)
    return (y_prompt, y_sample, new_s5_re, new_s5_im, new_hgrn, new_ssd)
```

```python
import functools
import math

import jax
import jax.numpy as jnp
from jax import lax
from jax.experimental import pallas as pl
from jax.experimental.pallas import tpu as pltpu

F32 = jnp.float32
BF16 = jnp.bfloat16
HIGHEST = lax.Precision.HIGHEST

D = 1024
DEPTH = 4
N_CTX_SEQ = 16
CTX_LEN = 256
N_LAT_SEQ = 2
LAT_LEN = 4096
GROUP_ROWS = 4096
N_GROUPS = 3
N_TOK = N_GROUPS * GROUP_ROWS
N_SEQ = N_CTX_SEQ + N_LAT_SEQ
NORM_EPS = 1e-6
GRID_W = 64

VMEM_LIMIT_BYTES = 56 * 1024 * 1024


def _cparams(*sem):
    return pltpu.CompilerParams(dimension_semantics=sem, vmem_limit_bytes=VMEM_LIMIT_BYTES)


def _silu(x):
    return x * jax.nn.sigmoid(x)


def _normmod(x, g, sc, sh):
    ms = jnp.mean(x * x, axis=-1, keepdims=True)
    return x * lax.rsqrt(ms + NORM_EPS) * g * (1.0 + sc) + sh


def _cmul(ar, ai, br, bi):
    return ar * br - ai * bi, ar * bi + ai * br


def _mod_kernel(c_ref, w_ref, b_ref, o_ref):
    o_ref[0] = jnp.dot(_silu(c_ref[...]), w_ref[0], precision=HIGHEST,
                       preferred_element_type=F32) + b_ref[0]


def ada_mod(cond8, w_ada, b_ada):
    tn = 1536
    return pl.pallas_call(
        _mod_kernel,
        grid=(DEPTH, 6 * D // tn),
        in_specs=[pl.BlockSpec((8, D), lambda i, j: (0, 0)),
                  pl.BlockSpec((1, D, tn), lambda i, j: (i, 0, j)),
                  pl.BlockSpec((1, 1, tn), lambda i, j: (i, 0, j))],
        out_specs=pl.BlockSpec((1, 8, tn), lambda i, j: (i, 0, j)),
        out_shape=jax.ShapeDtypeStruct((DEPTH, 8, 6 * D), F32),
        compiler_params=_cparams("parallel", "parallel"),
        name="ada_mod",
    )(cond8, w_ada, b_ada.reshape(DEPTH, 1, 6 * D))


def _embed_kernel(xp_ref, xs_ref, pos_ref, o_ref):
    r = pl.program_id(0)

    @pl.when(r == 0)
    def _():
        o_ref[...] = xp_ref[...]

    @pl.when(r > 0)
    def _():
        o_ref[...] = xs_ref[...] + pos_ref[...]


def embed_tokens(xp, xs, pos):
    tm = 1024
    nt = GROUP_ROWS // tm
    return pl.pallas_call(
        _embed_kernel,
        grid=(N_GROUPS, nt),
        in_specs=[pl.BlockSpec((tm, D), lambda r, i: (jnp.where(r == 0, i, 0), 0)),
                  pl.BlockSpec((tm, D), lambda r, i: (jnp.where(r == 0, 0, (r - 1) * nt + i), 0)),
                  pl.BlockSpec((tm, D), lambda r, i: (i, 0))],
        out_specs=pl.BlockSpec((tm, D), lambda r, i: (r * nt + i, 0)),
        out_shape=jax.ShapeDtypeStruct((N_TOK, D), F32),
        compiler_params=_cparams("parallel", "parallel"),
        name="embed_tokens",
    )(xp, xs, pos)


TM = 1024
TILES_PER_GROUP = GROUP_ROWS // TM


def _group_of_tile(i):
    return i // TILES_PER_GROUP


def _normmod_kernel(x_ref, g_ref, sc_ref, sh_ref, o_ref):
    o_ref[...] = _normmod(x_ref[...], g_ref[...], sc_ref[0], sh_ref[0])


def normmod(x, g, sc, sh):
    return pl.pallas_call(
        _normmod_kernel,
        grid=(N_TOK // TM,),
        in_specs=[pl.BlockSpec((TM, D), lambda i: (i, 0)),
                  pl.BlockSpec((1, D), lambda i: (0, 0)),
                  pl.BlockSpec((1, 1, D), lambda i: (_group_of_tile(i), 0, 0)),
                  pl.BlockSpec((1, 1, D), lambda i: (_group_of_tile(i), 0, 0))],
        out_specs=pl.BlockSpec((TM, D), lambda i: (i, 0)),
        out_shape=jax.ShapeDtypeStruct((N_TOK, D), F32),
        compiler_params=_cparams("parallel"),
        name="normmod",
    )(x, g.reshape(1, D), sc, sh)


def _nm_matmul_kernel(x_ref, g_ref, sc_ref, sh_ref, w_ref, b_ref, o_ref, h_scr):
    @pl.when(pl.program_id(1) == 0)
    def _():
        h_scr[...] = _normmod(x_ref[...], g_ref[...], sc_ref[0], sh_ref[0]).astype(BF16)

    o_ref[...] = jnp.dot(h_scr[...], w_ref[...].astype(BF16),
                         preferred_element_type=F32) + b_ref[...]


def nm_matmul(x, g, sc, sh, w, b, n_out, tn=512, name="nm_matmul"):
    return pl.pallas_call(
        _nm_matmul_kernel,
        grid=(N_TOK // TM, n_out // tn),
        in_specs=[pl.BlockSpec((TM, D), lambda i, j: (i, 0)),
                  pl.BlockSpec((1, D), lambda i, j: (0, 0)),
                  pl.BlockSpec((1, 1, D), lambda i, j: (_group_of_tile(i), 0, 0)),
                  pl.BlockSpec((1, 1, D), lambda i, j: (_group_of_tile(i), 0, 0)),
                  pl.BlockSpec((D, tn), lambda i, j: (0, j)),
                  pl.BlockSpec((1, tn), lambda i, j: (0, j))],
        out_specs=pl.BlockSpec((TM, tn), lambda i, j: (i, j)),
        out_shape=jax.ShapeDtypeStruct((N_TOK, n_out), F32),
        scratch_shapes=[pltpu.VMEM((TM, D), BF16)],
        compiler_params=_cparams("parallel", "arbitrary"),
        name=name,
    )(x, g.reshape(1, D), sc, sh, w, b.reshape(1, -1))


S5_G = 64
S5_H = 16
S5_P = 64
S5_L = 16
S5_GB = 8
S5_NB = S5_G // S5_GB
S5_CH = GROUP_ROWS // S5_L
S5_CTX_CH = CTX_LEN // S5_L
S5_NPOW = 8
S5_XW = S5_L * 128
S5_SW = 2 * S5_GB * S5_P


def _s5_prep_kernel(lrr_ref, lir_ref, lrc_ref, lic_ref, ldt_ref, btr_ref, bti_ref,
                    bxr_ref, bxi_ref, cxr_ref, cxi_ref,
                    kk_ref, wsr_ref, wsi_ref, wor_ref, woi_ref, apr_ref, api_ref):
    dt = jnp.exp(ldt_ref[0])
    lam_r, lam_i = lrr_ref[0], lir_ref[0]
    ar, ai = lam_r * dt, lam_i * dt
    e1 = jnp.exp(ar)
    lbr, lbi = e1 * jnp.cos(ai), e1 * jnp.sin(ai)
    den = lam_r * lam_r + lam_i * lam_i
    nr, ni = lbr - 1.0, lbi
    beta_r = (nr * lam_r + ni * lam_i) / den
    beta_i = (ni * lam_r - nr * lam_i) / den
    bbr, bbi = _cmul(beta_r, beta_i, btr_ref[0], bti_ref[0])
    srow = lax.broadcasted_iota(jnp.int32, (S5_GB, S5_L * S5_H, S5_P), 1) // S5_H
    pw = (S5_L - 1 - srow).astype(F32)
    es = jnp.exp(pw * ar)
    esr, esi = es * jnp.cos(pw * ai), es * jnp.sin(pw * ai)
    tr, ti = _cmul(beta_r, beta_i, bxr_ref[0], bxi_ref[0])
    wr, wi = _cmul(esr, esi, tr, ti)
    wsr_ref[0] = wr
    wsi_ref[0] = wi
    el = jnp.exp(S5_L * ar)
    pr, pi_ = el * jnp.cos(S5_L * ai), el * jnp.sin(S5_L * ai)
    for k in range(S5_NPOW):
        apr_ref[0, :, k:k + 1, :] = pr
        api_ref[0, :, k:k + 1, :] = pi_
        pr, pi_ = _cmul(pr, pi_, pr, pi_)
    dtc = dt
    acr, aci = lrc_ref[0] * dtc, lic_ref[0] * dtc
    tl = (lax.broadcasted_iota(jnp.int32, (S5_GB, S5_P, S5_L * S5_H), 2) // S5_H).astype(F32)
    cr, ci = cxr_ref[0], cxi_ref[0]

    def c_times_pow(pw):
        e = jnp.exp(pw * acr)
        er, ei = e * jnp.cos(pw * aci), e * jnp.sin(pw * aci)
        return _cmul(cr, ci, er, ei)

    o_r, o_i = c_times_pow(tl + 1.0)
    wor_ref[0] = o_r
    woi_ref[0] = -o_i
    k_r, k_i = c_times_pow(tl)
    for g in range(S5_GB):
        kk_ref[0, g] = (jnp.dot(bbr[g], k_r[g], precision=HIGHEST, preferred_element_type=F32)
                        - jnp.dot(bbi[g], k_i[g], precision=HIGHEST, preferred_element_type=F32))


def s5_prepare(lam_re, lam_im, log_dt, b_re, b_im, c_re, c_im):
    LH = S5_L * S5_H
    lrr = lam_re.reshape(2, S5_G, 1, S5_P)
    lir = lam_im.reshape(2, S5_G, 1, S5_P)
    lrc = lam_re.reshape(2, S5_G, S5_P, 1)
    lic = lam_im.reshape(2, S5_G, S5_P, 1)
    ldt = log_dt.reshape(2, S5_G, 1, 1)
    btr = jnp.swapaxes(b_re, 2, 3)
    bti = jnp.swapaxes(b_im, 2, 3)
    bxr = jnp.tile(btr, (1, 1, S5_L, 1))
    bxi = jnp.tile(bti, (1, 1, S5_L, 1))
    cxr = jnp.tile(jnp.swapaxes(c_re, 2, 3), (1, 1, 1, S5_L))
    cxi = jnp.tile(jnp.swapaxes(c_im, 2, 3), (1, 1, 1, S5_L))

    def spec(*tail):
        return pl.BlockSpec((1, S5_GB) + tail, lambda d, j: (d, j) + (0,) * len(tail))

    outs = pl.pallas_call(
        _s5_prep_kernel,
        grid=(2, S5_NB),
        in_specs=[spec(1, S5_P), spec(1, S5_P), spec(S5_P, 1), spec(S5_P, 1), spec(1, 1),
                  spec(S5_H, S5_P), spec(S5_H, S5_P), spec(LH, S5_P), spec(LH, S5_P),
                  spec(S5_P, LH), spec(S5_P, LH)],
        out_specs=[spec(S5_H, LH), spec(LH, S5_P), spec(LH, S5_P), spec(S5_P, LH), spec(S5_P, LH),
                   spec(S5_NPOW, S5_P), spec(S5_NPOW, S5_P)],
        out_shape=[jax.ShapeDtypeStruct((2, S5_G, S5_H, LH), F32),
                   jax.ShapeDtypeStruct((2, S5_G, LH, S5_P), F32),
                   jax.ShapeDtypeStruct((2, S5_G, LH, S5_P), F32),
                   jax.ShapeDtypeStruct((2, S5_G, S5_P, LH), F32),
                   jax.ShapeDtypeStruct((2, S5_G, S5_P, LH), F32),
                   jax.ShapeDtypeStruct((2, S5_G, S5_NPOW, S5_P), F32),
                   jax.ShapeDtypeStruct((2, S5_G, S5_NPOW, S5_P), F32)],
        compiler_params=_cparams("parallel", "parallel"),
        name="s5_prepare",
    )(lrr, lir, lrc, lic, ldt, btr, bti, bxr, bxi, cxr, cxi)
    kk, wsr, wsi, wor, woi, apr, api = outs
    eye = jnp.eye(S5_GB, dtype=F32)
    kk8 = jnp.einsum('dbghto,gk->dbghtko',
                     kk.reshape(2, S5_NB, S5_GB, S5_H, S5_L, S5_H), eye
                     ).reshape(2, S5_NB, 128, S5_XW).astype(BF16)
    ws = jnp.stack([wsr, wsi], axis=3).reshape(2, S5_NB, S5_GB, S5_L, S5_H, 2, S5_P)
    wst8 = jnp.einsum('dbgshcp,gk->dbsghckp', ws, eye).reshape(2, S5_NB, S5_XW, S5_SW).astype(BF16)
    wo = jnp.stack([wor, woi], axis=2).reshape(2, S5_NB, S5_GB, 2, S5_P, S5_L, S5_H)
    wout8 = jnp.einsum('dbgcpto,gk->dbcgptko', wo, eye).reshape(2, S5_NB, S5_SW, S5_XW).astype(BF16)

    def pw(a):
        return jnp.transpose(a.reshape(2, S5_NB, S5_GB, S5_NPOW, S5_P), (0, 1, 3, 2, 4)
                             ).reshape(2, S5_NB, S5_NPOW, S5_GB * S5_P)

    return kk8, wst8, wout8, pw(apr), pw(api)


def _s5_scan_body(d, r, h_ref, kk_ref, wst_ref, wout_ref, apr_ref, api_ref, h0r_ref, h0i_ref,
                  y_ref, fr_ref, fi_ref, m8, zr_s, zi_s):
    half = S5_SW // 2

    @pl.when(r == 0)
    def _build():
        for s in range(S5_L):
            if s:
                m8[128 * s:128 * (s + 1), 0:128 * s] = jnp.zeros((128, 128 * s), BF16)
            m8[128 * s:128 * (s + 1), 128 * s:] = kk_ref[0, 0, :, :S5_XW - 128 * s]

    def tloc(s):
        return s if d == 0 else S5_L - 1 - s

    slabs = [h_ref[pl.ds(tloc(s), S5_CH, stride=S5_L), :] for s in range(S5_L)]
    x8 = jnp.concatenate(slabs, axis=1).astype(BF16)
    delta = jnp.dot(x8, wst_ref[0, 0], preferred_element_type=F32)
    zr, zi = delta[:, :half], delta[:, half:]

    row = lax.broadcasted_iota(jnp.int32, (S5_CH, 1), 0)
    is_ctx = r == 0
    pos = jnp.where(is_ctx, row & (S5_CTX_CH - 1), row)
    last = jnp.where(is_ctx, S5_CTX_CH - 1, S5_CH - 1)
    a = jnp.maximum(r - 1, 0)
    lat = (r > 0).astype(F32)
    h0r = h0r_ref[0, 0, pl.ds(a, 1), :] * lat
    h0i = h0i_ref[0, 0, pl.ds(a, 1), :] * lat
    first = (pos == 0) if d == 0 else (pos == last)
    ir, ii = _cmul(apr_ref[0, 0, 0:1, :], api_ref[0, 0, 0:1, :], h0r, h0i)
    zr = zr + jnp.where(first, ir, 0.0)
    zi = zi + jnp.where(first, ii, 0.0)
    for k in range(S5_NPOW):
        m = 1 << k
        akr, aki = apr_ref[0, 0, k:k + 1, :], api_ref[0, 0, k:k + 1, :]
        if d == 0:
            sr, si = pltpu.roll(zr, m, 0), pltpu.roll(zi, m, 0)
            valid = pos >= m
        else:
            sr, si = pltpu.roll(zr, S5_CH - m, 0), pltpu.roll(zi, S5_CH - m, 0)
            valid = pos <= last - m
        pr, pi_ = _cmul(akr, aki, sr, si)
        zr = zr + jnp.where(valid, pr, 0.0)
        zi = zi + jnp.where(valid, pi_, 0.0)
    if d == 0:
        sr, si = pltpu.roll(zr, 1, 0), pltpu.roll(zi, 1, 0)
    else:
        sr, si = pltpu.roll(zr, S5_CH - 1, 0), pltpu.roll(zi, S5_CH - 1, 0)
    sr = jnp.where(first, h0r, sr)
    si = jnp.where(first, h0i, si)
    s_in = jnp.concatenate([sr, si], axis=1).astype(BF16)
    y8 = (jnp.dot(x8, m8[...], preferred_element_type=F32)
          + jnp.dot(s_in, wout_ref[0, 0], preferred_element_type=F32))
    for t in range(S5_L):
        y_ref[0, pl.ds(tloc(t), S5_CH, stride=S5_L), :] = y8[:, 128 * t:128 * (t + 1)]

    @pl.when(r == 0)
    def _fin():
        off = S5_CTX_CH - 1 if d == 0 else 0
        for q in range(half // 128):
            zr_s[q] = zr[:, 128 * q:128 * (q + 1)]
            zi_s[q] = zi[:, 128 * q:128 * (q + 1)]
            fr_ref[0, 0, :, 128 * q:128 * (q + 1)] = zr_s[q, pl.ds(off, N_CTX_SEQ, stride=S5_CTX_CH), :]
            fi_ref[0, 0, :, 128 * q:128 * (q + 1)] = zi_s[q, pl.ds(off, N_CTX_SEQ, stride=S5_CTX_CH), :]


def _s5_scan_kernel(*refs):
    d = pl.program_id(1)
    r = pl.program_id(2)

    @pl.when(d == 0)
    def _():
        _s5_scan_body(0, r, *refs)

    @pl.when(d == 1)
    def _():
        _s5_scan_body(1, r, *refs)


def s5_scan(h, kk8, wst8, wout8, apr, api, h0r, h0i):
    half = S5_SW // 2
    return pl.pallas_call(
        _s5_scan_kernel,
        grid=(S5_NB, 2, N_GROUPS),
        in_specs=[pl.BlockSpec((GROUP_ROWS, 128), lambda j, d, r: (r, j)),
                  pl.BlockSpec((1, 1, 128, S5_XW), lambda j, d, r: (d, j, 0, 0)),
                  pl.BlockSpec((1, 1, S5_XW, S5_SW), lambda j, d, r: (d, j, 0, 0)),
                  pl.BlockSpec((1, 1, S5_SW, S5_XW), lambda j, d, r: (d, j, 0, 0)),
                  pl.BlockSpec((1, 1, S5_NPOW, half), lambda j, d, r: (d, j, 0, 0)),
                  pl.BlockSpec((1, 1, S5_NPOW, half), lambda j, d, r: (d, j, 0, 0)),
                  pl.BlockSpec((1, 1, N_LAT_SEQ, half), lambda j, d, r: (d, j, 0, 0)),
                  pl.BlockSpec((1, 1, N_LAT_SEQ, half), lambda j, d, r: (d, j, 0, 0))],
        out_specs=[pl.BlockSpec((1, GROUP_ROWS, 128), lambda j, d, r: (d, r, j)),
                   pl.BlockSpec((1, 1, N_CTX_SEQ, half), lambda j, d, r: (d, j, 0, 0)),
                   pl.BlockSpec((1, 1, N_CTX_SEQ, half), lambda j, d, r: (d, j, 0, 0))],
        out_shape=[jax.ShapeDtypeStruct((2, N_TOK, D), F32),
                   jax.ShapeDtypeStruct((2, S5_NB, N_CTX_SEQ, half), F32),
                   jax.ShapeDtypeStruct((2, S5_NB, N_CTX_SEQ, half), F32)],
        scratch_shapes=[pltpu.VMEM((S5_XW, S5_XW), BF16),
                        pltpu.VMEM((half // 128, S5_CH, 128), F32),
                        pltpu.VMEM((half // 128, S5_CH, 128), F32)],
        compiler_params=_cparams("arbitrary", "arbitrary", "arbitrary"),
        name="s5_scan",
    )(h, kk8, wst8, wout8, apr, api, h0r, h0i)


def _s5_glu_kernel(h_ref, y0_ref, y1_ref, dsk_ref, wa_ref, wb_ref, ba_ref, bb_ref, x_ref, g1_ref,
                   o_ref, yg_scr):
    @pl.when(pl.program_id(1) == 0)
    def _():
        y = dsk_ref[...] * h_ref[...] + y0_ref[0] + y1_ref[0]
        yg_scr[...] = jax.nn.gelu(y).astype(BF16)

    yg = yg_scr[...]
    a = jnp.dot(yg, wa_ref[...].astype(BF16), preferred_element_type=F32) + ba_ref[...]
    b = jnp.dot(yg, wb_ref[...].astype(BF16), preferred_element_type=F32) + bb_ref[...]
    o_ref[...] = x_ref[...] + g1_ref[0] * (a * jax.nn.sigmoid(b))


def s5_glu(h, y, d_skip, w_glu, b_glu, x, g1, tn=512):
    nj = D // tn
    b2 = b_glu.reshape(1, 2 * D)
    return pl.pallas_call(
        _s5_glu_kernel,
        grid=(N_TOK // TM, nj),
        in_specs=[pl.BlockSpec((TM, D), lambda i, j: (i, 0)),
                  pl.BlockSpec((1, TM, D), lambda i, j: (0, i, 0)),
                  pl.BlockSpec((1, TM, D), lambda i, j: (1, i, 0)),
                  pl.BlockSpec((1, D), lambda i, j: (0, 0)),
                  pl.BlockSpec((D, tn), lambda i, j: (0, j)),
                  pl.BlockSpec((D, tn), lambda i, j: (0, nj + j)),
                  pl.BlockSpec((1, tn), lambda i, j: (0, j)),
                  pl.BlockSpec((1, tn), lambda i, j: (0, nj + j)),
                  pl.BlockSpec((TM, tn), lambda i, j: (i, j)),
                  pl.BlockSpec((1, 1, tn), lambda i, j: (_group_of_tile(i), 0, j))],
        out_specs=pl.BlockSpec((TM, tn), lambda i, j: (i, j)),
        out_shape=jax.ShapeDtypeStruct((N_TOK, D), F32),
        scratch_shapes=[pltpu.VMEM((TM, D), BF16)],
        compiler_params=_cparams("parallel", "arbitrary"),
        name="s5_glu",
    )(h, y, y, d_skip.reshape(1, D), w_glu, w_glu, b2, b2, x, g1)


def s5_layer(x, mods, g_norm, st_re, st_im, lam_re, lam_im, log_dt, b_re, b_im, c_re, c_im,
             d_skip, w_glu, b_glu):
    sh1, sc1, g1 = mods
    h = normmod(x, g_norm, sc1, sh1)
    kk8, wst8, wout8, apr, api = s5_prepare(lam_re, lam_im, log_dt, b_re, b_im, c_re, c_im)

    def h0(st):
        return jnp.transpose(st.reshape(N_LAT_SEQ, 2, S5_NB, S5_GB * S5_P), (1, 2, 0, 3))

    y, fr, fi = s5_scan(h, kk8, wst8, wout8, apr, api, h0(st_re), h0(st_im))

    def fin(f):
        return jnp.transpose(f.reshape(2, S5_NB, N_CTX_SEQ, S5_GB, S5_P), (2, 0, 1, 3, 4)
                             ).reshape(N_CTX_SEQ, 2, S5_G, S5_P)

    x = s5_glu(h, y, d_skip, w_glu, b_glu, x, g1)
    return x, fin(fr), fin(fi)


CHUNK = 64
N_CHUNKS = N_TOK // CHUNK
CTX_CHUNKS = N_CTX_SEQ * CTX_LEN // CHUNK
CH_PER_CTX = CTX_LEN // CHUNK
CH_PER_LAT = LAT_LEN // CHUNK


def _chunk_of_step(d, c):
    return jnp.where(d == 0, c, N_CHUNKS - 1 - c)


def _seq_of_chunk(ce):
    return jnp.where(ce < CTX_CHUNKS, ce // CH_PER_CTX, N_CTX_SEQ + (ce - CTX_CHUNKS) // CH_PER_LAT)


def _chunk_flags(d, ce):
    is_ctx = ce < CTX_CHUNKS
    pos = jnp.where(is_ctx, ce % CH_PER_CTX, (ce - CTX_CHUNKS) % CH_PER_LAT)
    n = jnp.where(is_ctx, CH_PER_CTX, CH_PER_LAT)
    t_first, t_last = pos == 0, pos == n - 1
    if d == 0:
        return is_ctx, t_first, t_last
    return is_ctx, t_last, t_first


def _tri(d, shape, row_axis=0, col_axis=1):
    r = lax.broadcasted_iota(jnp.int32, shape, row_axis)
    c = lax.broadcasted_iota(jnp.int32, shape, col_axis)
    return (r >= c) if d == 0 else (r <= c)


HG_H = 8
HG_K = 128


def _hg_scan_body(d, layer, q_ref, v_ref, z_ref, lbl_ref, s0_ref, o_ref, fin_ref, st_scr):
    ce = _chunk_of_step(d, pl.program_id(1))
    is_ctx, starts, ends = _chunk_flags(d, ce)

    @pl.when(starts & is_ctx)
    def _():
        st_scr[...] = jnp.zeros_like(st_scr)

    @pl.when(starts & jnp.logical_not(is_ctx))
    def _():
        st_scr[...] = s0_ref[0, 0]

    lg = lbl_ref[0]
    e = jnp.exp(lg - jnp.max(lg, axis=0, keepdims=True))
    sm = e / jnp.sum(e, axis=0, keepdims=True)
    lb = jnp.sum(sm[1:layer + 1], axis=0, keepdims=True)
    f = lb + (1.0 - lb) * jax.nn.sigmoid(z_ref[...])
    g = jnp.log(f)
    kk = 1.0 - f
    tri = _tri(d, (CHUNK, CHUNK))
    cum = jnp.dot(tri.astype(F32), g, precision=HIGHEST, preferred_element_type=F32)
    tot = cum[CHUNK - 1:CHUNK] if d == 0 else cum[0:1]
    mid = cum[CHUNK // 2 - 1:CHUNK // 2]
    q = q_ref[...]
    v = v_ref[...]
    qa = (q * jnp.exp(cum - mid)).astype(BF16)
    ka = (kk * jnp.exp(mid - cum)).astype(BF16)
    qs = (q * jnp.exp(cum)).astype(BF16)
    kd = (kk * jnp.exp(tot - cum)).astype(BF16)
    vb = v.astype(BF16)
    etot = jnp.exp(tot)
    nt = (((1,), (1,)), ((), ()))
    tn = (((0,), (0,)), ((), ()))
    for hd in range(HG_H):
        sl = slice(HG_K * hd, HG_K * (hd + 1))
        a = lax.dot_general(qa[:, sl], ka[:, sl], nt, preferred_element_type=F32)
        a = jnp.where(tri, a, 0.0).astype(BF16)
        st = st_scr[hd]
        o = (jnp.dot(a, vb[:, sl], preferred_element_type=F32)
             + lax.dot_general(qs[:, sl], st.astype(BF16), nt, preferred_element_type=F32))
        o_ref[0, :, sl] = o
        st_scr[hd] = st * etot[:, sl] + lax.dot_general(vb[:, sl], kd[:, sl], tn,
                                                        preferred_element_type=F32)

    @pl.when(ends)
    def _():
        for hd in range(HG_H):
            fin_ref[0, 0, hd] = st_scr[hd].T


def _hg_scan_kernel(*refs, layer):
    d = pl.program_id(0)

    @pl.when(d == 0)
    def _():
        _hg_scan_body(0, layer, *refs)

    @pl.when(d == 1)
    def _():
        _hg_scan_body(1, layer, *refs)


def hg_scan(proj, lb_logits, s0t, layer):
    def tok(col):
        return lambda d, c: (_chunk_of_step(d, c), col)

    def lat_idx(d, c):
        return jnp.maximum(_seq_of_chunk(_chunk_of_step(d, c)) - N_CTX_SEQ, 0)

    def fin_idx(d, c):
        return jnp.minimum(_seq_of_chunk(_chunk_of_step(d, c)), N_CTX_SEQ)

    return pl.pallas_call(
        functools.partial(_hg_scan_kernel, layer=layer),
        grid=(2, N_CHUNKS),
        in_specs=[pl.BlockSpec((CHUNK, D), tok(0)),
                  pl.BlockSpec((CHUNK, D), tok(1)),
                  pl.BlockSpec((CHUNK, D), lambda d, c: (_chunk_of_step(d, c), 3 + d)),
                  pl.BlockSpec((1, DEPTH, D), lambda d, c: (d, 0, 0)),
                  pl.BlockSpec((1, 1, HG_H, HG_K, HG_K), lambda d, c: (d, lat_idx(d, c), 0, 0, 0))],
        out_specs=[pl.BlockSpec((1, CHUNK, D), lambda d, c: (d, _chunk_of_step(d, c), 0)),
                   pl.BlockSpec((1, 1, HG_H, HG_K, HG_K), lambda d, c: (fin_idx(d, c), d, 0, 0, 0))],
        out_shape=[jax.ShapeDtypeStruct((2, N_TOK, D), F32),
                   jax.ShapeDtypeStruct((N_CTX_SEQ + 1, 2, HG_H, HG_K, HG_K), F32)],
        scratch_shapes=[pltpu.VMEM((HG_H, HG_K, HG_K), F32)],
        compiler_params=_cparams("arbitrary", "arbitrary"),
        name="hg_scan",
    )(proj, proj, proj, lb_logits, s0t)


def _hg_out_kernel(o0_ref, o1_ref, gate_ref, gn_ref, w_ref, x_ref, g1_ref, out_ref, on_scr):
    @pl.when(pl.program_id(1) == 0)
    def _():
        for hd in range(HG_H):
            sl = slice(HG_K * hd, HG_K * (hd + 1))
            o = o0_ref[0, :, sl] + o1_ref[0, :, sl]
            o = o * lax.rsqrt(jnp.mean(o * o, axis=-1, keepdims=True) + NORM_EPS) * gn_ref[...]
            on_scr[:, sl] = (o * _silu(gate_ref[:, sl])).astype(BF16)

    out_ref[...] = x_ref[...] + g1_ref[0] * jnp.dot(on_scr[...], w_ref[...].astype(BF16),
                                                    preferred_element_type=F32)


def hg_out(o, proj, g_norm, w_o, x, g1, tn=512):
    return pl.pallas_call(
        _hg_out_kernel,
        grid=(N_TOK // TM, D // tn),
        in_specs=[pl.BlockSpec((1, TM, D), lambda i, j: (0, i, 0)),
                  pl.BlockSpec((1, TM, D), lambda i, j: (1, i, 0)),
                  pl.BlockSpec((TM, D), lambda i, j: (i, 2)),
                  pl.BlockSpec((1, HG_K), lambda i, j: (0, 0)),
                  pl.BlockSpec((D, tn), lambda i, j: (0, j)),
                  pl.BlockSpec((TM, tn), lambda i, j: (i, j)),
                  pl.BlockSpec((1, 1, tn), lambda i, j: (_group_of_tile(i), 0, j))],
        out_specs=pl.BlockSpec((TM, tn), lambda i, j: (i, j)),
        out_shape=jax.ShapeDtypeStruct((N_TOK, D), F32),
        scratch_shapes=[pltpu.VMEM((TM, D), BF16)],
        compiler_params=_cparams("parallel", "arbitrary"),
        name="hg_out",
    )(o, o, proj, g_norm.reshape(1, HG_K), w_o, x, g1)


def hgrn_layer(x, mods, g_mix, state, layer, lb_logits, w_qig, w_f, b_f, g_norm, w_o):
    sh1, sc1, g1 = mods
    w5 = jnp.concatenate([w_qig, w_f[0], w_f[1]], axis=1)
    b5 = jnp.concatenate([jnp.zeros((3 * D,), F32), b_f[0], b_f[1]])
    proj = nm_matmul(x, g_mix, sc1, sh1, w5, b5, 5 * D, name="hg_proj")
    s0t = jnp.transpose(state, (1, 0, 2, 4, 3))
    o, fin = hg_scan(proj, lb_logits, s0t, layer)
    x = hg_out(o, proj, g_norm, w_o, x, g1)
    return x, fin[:N_CTX_SEQ]


SSD_INNER = 2 * D
SSD_HEADS = 32
SSD_P = 64
SSD_NG = 4
SSD_N = 128
SSD_XBC = SSD_INNER + 2 * SSD_NG * SSD_N
SSD_ZX = SSD_INNER + SSD_XBC
SSD_CONV = 5
CONV_TM = 256
CONV_HALO = 8


def _ssd_conv_kernel(cur_ref, prev_ref, next_ref, w_ref, b_ref, o_ref, ext):
    i = pl.program_id(0)
    n_ctx_tiles = N_CTX_SEQ * CTX_LEN // CONV_TM
    per_lat = LAT_LEN // CONV_TM
    is_ctx = i < n_ctx_tiles
    k = (i - n_ctx_tiles) % per_lat
    seq_start = is_ctx | (k == 0)
    seq_end = is_ctx | (k == per_lat - 1)
    ext[0:CONV_HALO] = jnp.where(seq_start, 0.0, prev_ref[...])
    ext[CONV_HALO:CONV_HALO + CONV_TM] = cur_ref[...]
    ext[CONV_HALO + CONV_TM:] = jnp.where(seq_end, 0.0, next_ref[...])
    acc = jnp.broadcast_to(b_ref[...], (CONV_TM, D))
    for t in range(SSD_CONV):
        acc = acc + w_ref[t:t + 1, :] * ext[pl.ds(CONV_HALO - SSD_CONV // 2 + t, CONV_TM), :]
    o_ref[...] = _silu(acc)


def ssd_conv(zx, conv_w, conv_b):
    nrb = N_TOK // CONV_HALO
    rpt = CONV_TM // CONV_HALO
    c0 = SSD_INNER // D
    return pl.pallas_call(
        _ssd_conv_kernel,
        grid=(N_TOK // CONV_TM, SSD_XBC // D),
        in_specs=[pl.BlockSpec((CONV_TM, D), lambda i, j: (i, c0 + j)),
                  pl.BlockSpec((CONV_HALO, D), lambda i, j: (jnp.maximum(i * rpt - 1, 0), c0 + j)),
                  pl.BlockSpec((CONV_HALO, D), lambda i, j: (jnp.minimum((i + 1) * rpt, nrb - 1), c0 + j)),
                  pl.BlockSpec((SSD_CONV, D), lambda i, j: (0, j)),
                  pl.BlockSpec((1, D), lambda i, j: (0, j))],
        out_specs=pl.BlockSpec((CONV_TM, D), lambda i, j: (i, j)),
        out_shape=jax.ShapeDtypeStruct((N_TOK, SSD_XBC), F32),
        scratch_shapes=[pltpu.VMEM((CONV_TM + 2 * CONV_HALO, D), F32)],
        compiler_params=_cparams("parallel", "parallel"),
        name="ssd_conv",
    )(zx, zx, zx, conv_w, conv_b.reshape(1, SSD_XBC))


def _ssd_scan_body(d, xlo_ref, xhi_ref, bc_ref, dtr_ref, dtb_ref, alog_ref, h0_ref,
                   y_ref, fin_ref, ht_scr):
    ce = _chunk_of_step(d, pl.program_id(1))
    is_ctx, starts, ends = _chunk_flags(d, ce)

    @pl.when(starts & is_ctx)
    def _():
        ht_scr[...] = jnp.zeros_like(ht_scr)

    @pl.when(starts & jnp.logical_not(is_ctx))
    def _():
        ht_scr[...] = h0_ref[0, 0]

    xr = dtr_ref[...] + dtb_ref[...]
    dt = jnp.maximum(xr, 0.0) + jnp.log(1.0 + jnp.exp(-jnp.abs(xr)))
    dta = dt * (-jnp.exp(alog_ref[...]))
    tri = _tri(d, (CHUNK, CHUNK))
    cum = jnp.dot(tri.astype(F32), dta, precision=HIGHEST, preferred_element_type=F32)
    r = lax.broadcasted_iota(jnp.int32, (CHUNK, 2 * CHUNK), 0)
    cc = lax.broadcasted_iota(jnp.int32, (CHUNK, 2 * CHUNK), 1)
    lo_half = cc < CHUNK
    ccm = jnp.where(lo_half, cc, cc - CHUNK)
    trit = (ccm >= r) if d == 0 else (ccm <= r)
    tn = (((0,), (0,)), ((), ()))
    nt = (((1,), (1,)), ((), ()))
    cumt_lo = lax.dot_general(dta, (trit & lo_half).astype(F32), tn, precision=HIGHEST,
                              preferred_element_type=F32)
    cumt_hi = lax.dot_general(dta, (trit & jnp.logical_not(lo_half)).astype(F32), tn,
                              precision=HIGHEST, preferred_element_type=F32)
    lane = lax.broadcasted_iota(jnp.int32, (CHUNK, 2 * SSD_P), 1)
    first_head = lane < SSD_P
    tri2 = (r >= ccm) if d == 0 else (r <= ccm)
    bc = bc_ref[...]
    for gq in range(SSD_NG):
        bg = bc[:, SSD_N * gq:SSD_N * (gq + 1)].astype(BF16)
        cg = bc[:, SSD_NG * SSD_N + SSD_N * gq:SSD_NG * SSD_N + SSD_N * (gq + 1)].astype(BF16)
        cb2 = lax.dot_general(cg, jnp.concatenate([bg, bg], axis=0), nt,
                              preferred_element_type=F32)
        for pp in range(4 * gq, 4 * gq + 4):
            h1 = SSD_HEADS * d + 2 * pp
            colp = jnp.where(first_head, cum[:, h1:h1 + 1], cum[:, h1 + 1:h1 + 2])
            rowp = cumt_lo[h1:h1 + 1, :] + cumt_hi[h1 + 1:h1 + 2, :]
            lmat = jnp.exp(jnp.where(tri2, colp - rowp, -jnp.inf))
            dtp = jnp.where(first_head, dt[:, h1:h1 + 1], dt[:, h1 + 1:h1 + 2])
            xref = xlo_ref if pp < 8 else xhi_ref
            c0 = 128 * (pp % 8)
            xdt = xref[:, c0:c0 + 128] * dtp
            rhs = jnp.concatenate([jnp.where(first_head, xdt, 0.0),
                                   jnp.where(first_head, 0.0, xdt)], axis=0).astype(BF16)
            y = jnp.dot((cb2 * lmat).astype(BF16), rhs, preferred_element_type=F32)
            ht = ht_scr[:, 128 * pp:128 * (pp + 1)]
            y = y + jnp.dot(cg, ht.astype(BF16), preferred_element_type=F32) * jnp.exp(colp)
            y_ref[0, :, 128 * pp:128 * (pp + 1)] = y
            totp = colp[CHUNK - 1:CHUNK] if d == 0 else colp[0:1]
            xw = (xdt * jnp.exp(totp - colp)).astype(BF16)
            ht_scr[:, 128 * pp:128 * (pp + 1)] = (
                ht * jnp.exp(totp) + lax.dot_general(bg, xw, tn, preferred_element_type=F32))

    @pl.when(ends)
    def _():
        fin_ref[0, 0] = ht_scr[...]


def _ssd_scan_kernel(*refs):
    d = pl.program_id(0)

    @pl.when(d == 0)
    def _():
        _ssd_scan_body(0, *refs)

    @pl.when(d == 1)
    def _():
        _ssd_scan_body(1, *refs)


def ssd_scan(xbc, dtr, dt_bias, a_log, h0t):
    nh2 = 2 * SSD_HEADS
    hp = SSD_HEADS * SSD_P

    def tok(col):
        return lambda d, c: (_chunk_of_step(d, c), col)

    def lat_idx(d, c):
        return jnp.maximum(_seq_of_chunk(_chunk_of_step(d, c)) - N_CTX_SEQ, 0)

    def fin_idx(d, c):
        return jnp.minimum(_seq_of_chunk(_chunk_of_step(d, c)), N_CTX_SEQ)

    return pl.pallas_call(
        _ssd_scan_kernel,
        grid=(2, N_CHUNKS),
        in_specs=[pl.BlockSpec((CHUNK, D), tok(0)),
                  pl.BlockSpec((CHUNK, D), tok(1)),
                  pl.BlockSpec((CHUNK, D), tok(2)),
                  pl.BlockSpec((CHUNK, nh2), tok(0)),
                  pl.BlockSpec((1, nh2), lambda d, c: (0, 0)),
                  pl.BlockSpec((1, nh2), lambda d, c: (0, 0)),
                  pl.BlockSpec((1, 1, SSD_N, hp), lambda d, c: (d, lat_idx(d, c), 0, 0))],
        out_specs=[pl.BlockSpec((1, CHUNK, hp), lambda d, c: (d, _chunk_of_step(d, c), 0)),
                   pl.BlockSpec((1, 1, SSD_N, hp), lambda d, c: (fin_idx(d, c), d, 0, 0))],
        out_shape=[jax.ShapeDtypeStruct((2, N_TOK, hp), F32),
                   jax.ShapeDtypeStruct((N_CTX_SEQ + 1, 2, SSD_N, hp), F32)],
        scratch_shapes=[pltpu.VMEM((SSD_N, hp), F32)],
        compiler_params=_cparams("arbitrary", "arbitrary"),
        name="ssd_scan",
    )(xbc, xbc, xbc, dtr, dt_bias.reshape(1, nh2), a_log.reshape(1, nh2), h0t)


SSD_OUT_TM = 512


def _ssd_out_kernel(xlo_ref, xhi_ref, zlo_ref, zhi_ref, y0_ref, y1_ref, dsk_ref, gn_ref, w_ref,
                    x_ref, g1_ref, out_ref, yn_scr):
    @pl.when(pl.program_id(1) == 0)
    def _():
        halves = []
        ss = jnp.zeros((SSD_OUT_TM, 1), F32)
        for k, (xr, zr) in enumerate(((xlo_ref, zlo_ref), (xhi_ref, zhi_ref))):
            sl = slice(D * k, D * (k + 1))
            y = dsk_ref[:, sl] * xr[...] + y0_ref[0, :, sl] + y1_ref[0, :, sl]
            y = y * _silu(zr[...])
            ss = ss + jnp.sum(y * y, axis=-1, keepdims=True)
            halves.append(y)
        scale = lax.rsqrt(ss / SSD_INNER + NORM_EPS)
        for k, y in enumerate(halves):
            sl = slice(D * k, D * (k + 1))
            yn_scr[:, sl] = (y * scale * gn_ref[:, sl]).astype(BF16)

    out_ref[...] = x_ref[...] + g1_ref[0] * jnp.dot(yn_scr[...], w_ref[...].astype(BF16),
                                                    preferred_element_type=F32)


def ssd_out(xbc, zx, y, d_skip_cols, g_norm, w_out, x, g1, tn=512):
    tm = SSD_OUT_TM
    tpg = GROUP_ROWS // tm
    return pl.pallas_call(
        _ssd_out_kernel,
        grid=(N_TOK // tm, D // tn),
        in_specs=[pl.BlockSpec((tm, D), lambda i, j: (i, 0)),
                  pl.BlockSpec((tm, D), lambda i, j: (i, 1)),
                  pl.BlockSpec((tm, D), lambda i, j: (i, 0)),
                  pl.BlockSpec((tm, D), lambda i, j: (i, 1)),
                  pl.BlockSpec((1, tm, SSD_INNER), lambda i, j: (0, i, 0)),
                  pl.BlockSpec((1, tm, SSD_INNER), lambda i, j: (1, i, 0)),
                  pl.BlockSpec((1, SSD_INNER), lambda i, j: (0, 0)),
                  pl.BlockSpec((1, SSD_INNER), lambda i, j: (0, 0)),
                  pl.BlockSpec((SSD_INNER, tn), lambda i, j: (0, j)),
                  pl.BlockSpec((tm, tn), lambda i, j: (i, j)),
                  pl.BlockSpec((1, 1, tn), lambda i, j: (i // tpg, 0, j))],
        out_specs=pl.BlockSpec((tm, tn), lambda i, j: (i, j)),
        out_shape=jax.ShapeDtypeStruct((N_TOK, D), F32),
        scratch_shapes=[pltpu.VMEM((tm, SSD_INNER), BF16)],
        compiler_params=_cparams("parallel", "arbitrary"),
        name="ssd_out",
    )(xbc, xbc, zx, zx, y, y, d_skip_cols, g_norm.reshape(1, SSD_INNER), w_out, x, g1)


def ssd_layer(x, mods, g_mix, state, w_in, conv_w, conv_b, dt_bias, a_log, d_skip, g_norm, w_out):
    sh1, sc1, g1 = mods
    zx = nm_matmul(x, g_mix, sc1, sh1, w_in, jnp.zeros((SSD_ZX,), F32), SSD_ZX, name="ssd_proj")
    nh2 = 2 * SSD_HEADS
    dtr = nm_matmul(x, g_mix, sc1, sh1, w_in[:, SSD_ZX:], jnp.zeros((nh2,), F32), nh2, tn=nh2,
                    name="ssd_proj_dt")
    xbc = ssd_conv(zx, conv_w, conv_b)
    h0t = jnp.transpose(state, (1, 0, 4, 2, 3)).reshape(2, N_LAT_SEQ, SSD_N, SSD_HEADS * SSD_P)
    y, fin = ssd_scan(xbc, dtr, dt_bias, a_log, h0t)
    dcols = jnp.repeat(d_skip, SSD_P).reshape(1, SSD_INNER)
    x = ssd_out(xbc, zx, y, dcols, g_norm, w_out, x, g1)
    fin = jnp.transpose(fin[:N_CTX_SEQ].reshape(N_CTX_SEQ, 2, SSD_N, SSD_HEADS, SSD_P), (0, 1, 3, 4, 2))
    return x, fin


N_EXP = 16
FF = 2 * D
CAP_CTX = 2 * CTX_LEN // N_EXP
CAP_LAT = 2 * LAT_LEN // N_EXP
SLOTS_PER_GROUP = 512
SLOTS = N_GROUPS * SLOTS_PER_GROUP


def _router_kernel(x_ref, g_ref, sc_ref, sh_ref, wt_ref, h_ref, aff_ref):
    h = _normmod(x_ref[...], g_ref[...], sc_ref[0], sh_ref[0])
    h_ref[...] = h.astype(BF16)
    logits = lax.dot_general(wt_ref[...], h, (((1,), (1,)), ((), ())), precision=HIGHEST,
                             preferred_element_type=F32)
    e = jnp.exp(logits - jnp.max(logits, axis=0, keepdims=True))
    aff_ref[...] = e / jnp.sum(e, axis=0, keepdims=True)


def moe_route(x, g, sc, sh, w_router_t):
    return pl.pallas_call(
        _router_kernel,
        grid=(N_TOK // TM,),
        in_specs=[pl.BlockSpec((TM, D), lambda i: (i, 0)),
                  pl.BlockSpec((1, D), lambda i: (0, 0)),
                  pl.BlockSpec((1, 1, D), lambda i: (_group_of_tile(i), 0, 0)),
                  pl.BlockSpec((1, 1, D), lambda i: (_group_of_tile(i), 0, 0)),
                  pl.BlockSpec((N_EXP, D), lambda i: (0, 0))],
        out_specs=[pl.BlockSpec((TM, D), lambda i: (i, 0)),
                   pl.BlockSpec((N_EXP, TM), lambda i: (0, i))],
        out_shape=[jax.ShapeDtypeStruct((N_TOK, D), BF16),
                   jax.ShapeDtypeStruct((N_EXP, N_TOK), F32)],
        compiler_params=_cparams("parallel"),
        name="moe_router",
    )(x, g.reshape(1, D), sc, sh, w_router_t)


def _lane_prefix_excl(m):
    s, t = m.shape
    r = lax.broadcasted_iota(jnp.int32, (128, 128), 0)
    c = lax.broadcasted_iota(jnp.int32, (128, 128), 1)
    upper = (r <= c).astype(BF16)
    run = jnp.zeros((s, 1), F32)
    out = []
    for k in range(t // 128):
        blk = m[:, 128 * k:128 * (k + 1)]
        inc = jnp.dot(blk.astype(BF16), upper, preferred_element_type=F32) + run
        out.append(inc - blk)
        run = inc[:, 127:128]
    return jnp.concatenate(out, axis=1)


def _select_kernel(a_ref, off_ref, slot_ref, *, cap):
    bits = pltpu.bitcast(a_ref[...], jnp.int32)
    s = bits.shape[0]
    capf = float(cap)

    def body(_, lohi):
        lo, hi = lohi
        mid = lo + ((hi - lo + 1) >> 1)
        cnt = jnp.sum((bits >= mid).astype(F32), axis=1, keepdims=True)
        ok = cnt >= capf
        return jnp.where(ok, mid, lo), jnp.where(ok, hi, mid - 1)

    lo0 = jnp.zeros((s, 1), jnp.int32)
    hi0 = jnp.full((s, 1), 0x7F800000, jnp.int32)
    thr, _ = lax.fori_loop(0, 31, body, (lo0, hi0))
    gt = (bits > thr).astype(F32)
    eq = (bits == thr).astype(F32)
    need = capf - jnp.sum(gt, axis=1, keepdims=True)
    sel = gt + eq * (_lane_prefix_excl(eq) < need).astype(F32)
    slot = _lane_prefix_excl(sel) + off_ref[...]
    slot_ref[...] = jnp.where(sel > 0.0, slot, -1.0).astype(jnp.int32)


def moe_select(aff, off, cap):
    s, t = aff.shape
    return pl.pallas_call(
        functools.partial(_select_kernel, cap=cap),
        grid=(1,),
        in_specs=[pl.BlockSpec((s, t), lambda i: (0, 0)),
                  pl.BlockSpec((s, 1), lambda i: (0, 0))],
        out_specs=pl.BlockSpec((s, t), lambda i: (0, 0)),
        out_shape=jax.ShapeDtypeStruct((s, t), jnp.int32),
        compiler_params=_cparams("arbitrary"),
        name="moe_select",
    )(aff, off)


def _onehot(slot_row):
    s = lax.broadcasted_iota(jnp.int32, (SLOTS_PER_GROUP, GROUP_ROWS), 0)
    return s == slot_row


def _gather_kernel(slot_ref, aff_ref, h_ref, xs_ref, gs_ref):
    oh = _onehot(slot_ref[0, 0])
    xs_ref[0] = jnp.dot(oh.astype(BF16), h_ref[...], preferred_element_type=F32).astype(BF16)
    gs_ref[0] = jnp.sum(jnp.where(oh, aff_ref[0], 0.0), axis=1, keepdims=True)


def moe_gather(slot, aff3, h2):
    return pl.pallas_call(
        _gather_kernel,
        grid=(N_GROUPS, N_EXP),
        in_specs=[pl.BlockSpec((1, 1, 1, GROUP_ROWS), lambda r, e: (r, e, 0, 0)),
                  pl.BlockSpec((1, 1, GROUP_ROWS), lambda r, e: (e, 0, r)),
                  pl.BlockSpec((GROUP_ROWS, D), lambda r, e: (r, 0))],
        out_specs=[pl.BlockSpec((1, SLOTS_PER_GROUP, D), lambda r, e: (e, r, 0)),
                   pl.BlockSpec((1, SLOTS_PER_GROUP, 1), lambda r, e: (e, r, 0))],
        out_shape=[jax.ShapeDtypeStruct((N_EXP, SLOTS, D), BF16),
                   jax.ShapeDtypeStruct((N_EXP, SLOTS, 1), F32)],
        compiler_params=_cparams("parallel", "parallel"),
        name="moe_gather",
    )(slot, aff3, h2)


FF_TILE = 512


def _ffn_kernel(xs_ref, gs_ref, wg_ref, wu_ref, wd_ref, ys_ref, acc):
    f = pl.program_id(1)
    x = xs_ref[0]
    g = jnp.dot(x, wg_ref[0, 0].astype(BF16), preferred_element_type=F32)
    u = jnp.dot(x, wu_ref[0, 0].astype(BF16), preferred_element_type=F32)
    hid = (_silu(g) * u).astype(BF16)
    contrib = jnp.dot(hid, wd_ref[0, 0].astype(BF16), preferred_element_type=F32)

    @pl.when(f == 0)
    def _():
        acc[...] = contrib

    @pl.when(f > 0)
    def _():
        acc[...] += contrib

    @pl.when(f == FF // FF_TILE - 1)
    def _():
        ys_ref[0] = (acc[...] * gs_ref[0]).astype(BF16)


def moe_ffn(xs, gs, w_gate, w_up, w_down, layer):
    return pl.pallas_call(
        _ffn_kernel,
        grid=(N_EXP, FF // FF_TILE),
        in_specs=[pl.BlockSpec((1, SLOTS, D), lambda e, f: (e, 0, 0)),
                  pl.BlockSpec((1, SLOTS, 1), lambda e, f: (e, 0, 0)),
                  pl.BlockSpec((1, 1, D, FF_TILE), lambda e, f: (layer, e, 0, f)),
                  pl.BlockSpec((1, 1, D, FF_TILE), lambda e, f: (layer, e, 0, f)),
                  pl.BlockSpec((1, 1, FF_TILE, D), lambda e, f: (layer, e, f, 0))],
        out_specs=pl.BlockSpec((1, SLOTS, D), lambda e, f: (e, 0, 0)),
        out_shape=jax.ShapeDtypeStruct((N_EXP, SLOTS, D), BF16),
        scratch_shapes=[pltpu.VMEM((SLOTS, D), F32)],
        compiler_params=_cparams("parallel", "arbitrary"),
        name="moe_ffn",
    )(xs, gs, w_gate, w_up, w_down)


SCAT_TN = 256


def _scatter_kernel(slot_ref, ys_ref, x_ref, g2_ref, o_ref):
    e = pl.program_id(2)
    oh = _onehot(slot_ref[0, 0]).astype(BF16)
    contrib = lax.dot_general(oh, ys_ref[0], (((0,), (0,)), ((), ())),
                              preferred_element_type=F32)

    @pl.when(e == 0)
    def _():
        o_ref[...] = contrib

    @pl.when(e > 0)
    def _():
        o_ref[...] += contrib

    @pl.when(e == N_EXP - 1)
    def _():
        o_ref[...] = x_ref[...] + g2_ref[0] * o_ref[...]


def moe_scatter(slot, ys, x, g2):
    return pl.pallas_call(
        _scatter_kernel,
        grid=(N_GROUPS, D // SCAT_TN, N_EXP),
        in_specs=[pl.BlockSpec((1, 1, 1, GROUP_ROWS), lambda r, c, e: (r, e, 0, 0)),
                  pl.BlockSpec((1, SLOTS_PER_GROUP, SCAT_TN), lambda r, c, e: (e, r, c)),
                  pl.BlockSpec((GROUP_ROWS, SCAT_TN), lambda r, c, e: (r, c)),
                  pl.BlockSpec((1, 1, SCAT_TN), lambda r, c, e: (r, 0, c))],
        out_specs=pl.BlockSpec((GROUP_ROWS, SCAT_TN), lambda r, c, e: (r, c)),
        out_shape=jax.ShapeDtypeStruct((N_TOK, D), F32),
        compiler_params=_cparams("parallel", "parallel", "arbitrary"),
        name="moe_scatter",
    )(slot, ys, x, g2)


def moe_layer(x, mods, g_ffn, layer, w_router, w_gate, w_up, w_down):
    sh2, sc2, g2 = mods
    h2, aff = moe_route(x, g_ffn, sc2, sh2, w_router.T)
    aff_ctx = aff[:, :GROUP_ROWS].reshape(N_EXP * N_CTX_SEQ, CTX_LEN)
    off_ctx = jnp.tile(jnp.arange(N_CTX_SEQ, dtype=F32) * CAP_CTX, N_EXP).reshape(-1, 1)
    slot_ctx = moe_select(aff_ctx, off_ctx, CAP_CTX).reshape(1, N_EXP, GROUP_ROWS)
    aff_lat = jnp.transpose(aff[:, GROUP_ROWS:].reshape(N_EXP, N_LAT_SEQ, LAT_LEN), (1, 0, 2)
                            ).reshape(N_LAT_SEQ * N_EXP, LAT_LEN)
    slot_lat = moe_select(aff_lat, jnp.zeros((N_LAT_SEQ * N_EXP, 1), F32), CAP_LAT
                          ).reshape(N_LAT_SEQ, N_EXP, LAT_LEN)
    slot = jnp.concatenate([slot_ctx, slot_lat], axis=0).reshape(N_GROUPS, N_EXP, 1, GROUP_ROWS)
    xs, gs = moe_gather(slot, aff.reshape(N_EXP, 1, N_TOK), h2)
    ys = moe_ffn(xs, gs, w_gate, w_up, w_down, layer)
    return moe_scatter(slot, ys, x, g2)


def _final_norm_kernel(x_ref, g_ref, o_ref):
    x = x_ref[...]
    o_ref[...] = x * lax.rsqrt(jnp.mean(x * x, axis=-1, keepdims=True) + NORM_EPS) * g_ref[...]


def final_norm(x, g):
    return pl.pallas_call(
        _final_norm_kernel,
        grid=(N_TOK // TM,),
        in_specs=[pl.BlockSpec((TM, D), lambda i: (i, 0)),
                  pl.BlockSpec((1, D), lambda i: (0, 0))],
        out_specs=pl.BlockSpec((TM, D), lambda i: (i, 0)),
        out_shape=jax.ShapeDtypeStruct((N_TOK, D), F32),
        compiler_params=_cparams("parallel"),
        name="final_norm",
    )(x, g.reshape(1, D))


def _grid_pos_embed():
    rows = LAT_LEN // GRID_W
    quarter = D // 4
    omega = 1.0 / (10000.0 ** (jnp.arange(quarter, dtype=F32) / quarter))
    r = jnp.arange(rows, dtype=F32)[:, None] * omega
    cl = jnp.arange(GRID_W, dtype=F32)[:, None] * omega
    emb_r = jnp.concatenate([jnp.sin(r), jnp.cos(r)], axis=-1)
    emb_c = jnp.concatenate([jnp.sin(cl), jnp.cos(cl)], axis=-1)
    emb = jnp.concatenate([jnp.broadcast_to(emb_r[:, None], (rows, GRID_W, D // 2)),
                           jnp.broadcast_to(emb_c[None], (rows, GRID_W, D // 2))], axis=-1)
    return emb.reshape(LAT_LEN, D)


def kernel(x_prompt, x_sample, state_s5_re, state_s5_im, state_hgrn, state_ssd, c, c_ctx, w_ada, b_ada, norm_mix, norm_ffn, norm_final, s5_lam_re, s5_lam_im, s5_log_dt, s5_b_re, s5_b_im, s5_c_re, s5_c_im, s5_d, s5_w_glu, s5_b_glu, hg_w_qig, hg_w_f, hg_b_f, hg_lb_logits, hg_norm, hg_w_o, ssd_w_in, ssd_conv_w, ssd_conv_b, ssd_dt_bias, ssd_a_log, ssd_d, ssd_norm, ssd_w_out, moe_router, moe_w_gate, moe_w_up, moe_w_down):
    cond8 = jnp.concatenate([c_ctx[None], c, jnp.zeros((5, D), F32)], axis=0)
    mod = ada_mod(cond8, w_ada, b_ada)
    mods = jnp.transpose(mod.reshape(DEPTH, 8, 6, D)[:, :3], (0, 2, 1, 3)).reshape(DEPTH, 6, 3, 1, D)
    x = embed_tokens(x_prompt.reshape(-1, D), x_sample.reshape(-1, D), _grid_pos_embed())
    s5_re, s5_im, hg_fin, ssd_fin = [], [], [], []
    for i in range(DEPTH):
        mix_mods = (mods[i, 0], mods[i, 1], mods[i, 2])
        kind, j = i % 3, i // 3
        if kind == 0:
            x, fr, fi = s5_layer(x, mix_mods, norm_mix[i], state_s5_re[:, j], state_s5_im[:, j],
                                 s5_lam_re[j], s5_lam_im[j], s5_log_dt[j], s5_b_re[j], s5_b_im[j],
                                 s5_c_re[j], s5_c_im[j], s5_d[j], s5_w_glu[j], s5_b_glu[j])
            s5_re.append(fr)
            s5_im.append(fi)
        elif kind == 1:
            x, fh = hgrn_layer(x, mix_mods, norm_mix[i], state_hgrn[:, j], i, hg_lb_logits,
                               hg_w_qig[j], hg_w_f[j], hg_b_f[j], hg_norm[j], hg_w_o[j])
            hg_fin.append(fh)
        else:
            x, fs = ssd_layer(x, mix_mods, norm_mix[i], state_ssd[:, j], ssd_w_in[j], ssd_conv_w[j],
                              ssd_conv_b[j], ssd_dt_bias[j], ssd_a_log[j], ssd_d[j], ssd_norm[j],
                              ssd_w_out[j])
            ssd_fin.append(fs)
        x = moe_layer(x, (mods[i, 3], mods[i, 4], mods[i, 5]), norm_ffn[i], i, moe_router[i],
                      moe_w_gate, moe_w_up, moe_w_down)
    y = final_norm(x, norm_final)
    n_ctx = N_CTX_SEQ * CTX_LEN
    return (y[:n_ctx].reshape(N_CTX_SEQ, CTX_LEN, D), y[n_ctx:].reshape(N_LAT_SEQ, LAT_LEN, D),
            jnp.stack(s5_re, axis=1), jnp.stack(s5_im, axis=1),
            jnp.stack(hg_fin, axis=1), jnp.stack(ssd_fin, axis=1))
```

```python
import functools
import math

import jax
import jax.numpy as jnp
from jax import lax
from jax.experimental import pallas as pl
from jax.experimental.pallas import tpu as pltpu

F32 = jnp.float32
BF16 = jnp.bfloat16
HIGHEST = lax.Precision.HIGHEST

D = 1024
DEPTH = 4
N_CTX_SEQ = 16
CTX_LEN = 256
N_LAT_SEQ = 2
LAT_LEN = 4096
GROUP_ROWS = 4096
N_GROUPS = 3
N_TOK = N_GROUPS * GROUP_ROWS
N_SEQ = N_CTX_SEQ + N_LAT_SEQ
NORM_EPS = 1e-6
GRID_W = 64

VMEM_LIMIT_BYTES = 56 * 1024 * 1024


def _cparams(*sem):
    return pltpu.CompilerParams(dimension_semantics=sem, vmem_limit_bytes=VMEM_LIMIT_BYTES)


def _silu(x):
    return x * jax.nn.sigmoid(x)


def _normmod(x, g, sc, sh):
    ms = jnp.mean(x * x, axis=-1, keepdims=True)
    return x * lax.rsqrt(ms + NORM_EPS) * g * (1.0 + sc) + sh


def _cmul(ar, ai, br, bi):
    return ar * br - ai * bi, ar * bi + ai * br


def _mod_kernel(c_ref, w_ref, b_ref, o_ref):
    o_ref[0] = jnp.dot(_silu(c_ref[...]), w_ref[0], precision=HIGHEST,
                       preferred_element_type=F32) + b_ref[0]


def ada_mod(cond8, w_ada, b_ada):
    tn = 1536
    return pl.pallas_call(
        _mod_kernel,
        grid=(DEPTH, 6 * D // tn),
        in_specs=[pl.BlockSpec((8, D), lambda i, j: (0, 0)),
                  pl.BlockSpec((1, D, tn), lambda i, j: (i, 0, j)),
                  pl.BlockSpec((1, 1, tn), lambda i, j: (i, 0, j))],
        out_specs=pl.BlockSpec((1, 8, tn), lambda i, j: (i, 0, j)),
        out_shape=jax.ShapeDtypeStruct((DEPTH, 8, 6 * D), F32),
        compiler_params=_cparams("parallel", "parallel"),
        name="ada_mod",
    )(cond8, w_ada, b_ada.reshape(DEPTH, 1, 6 * D))


def _embed_kernel(xp_ref, xs_ref, pos_ref, o_ref):
    r = pl.program_id(0)

    @pl.when(r == 0)
    def _():
        o_ref[...] = xp_ref[...]

    @pl.when(r > 0)
    def _():
        o_ref[...] = xs_ref[...] + pos_ref[...]


def embed_tokens(xp, xs, pos):
    tm = 1024
    nt = GROUP_ROWS // tm
    return pl.pallas_call(
        _embed_kernel,
        grid=(N_GROUPS, nt),
        in_specs=[pl.BlockSpec((tm, D), lambda r, i: (jnp.where(r == 0, i, 0), 0)),
                  pl.BlockSpec((tm, D), lambda r, i: (jnp.where(r == 0, 0, (r - 1) * nt + i), 0)),
                  pl.BlockSpec((tm, D), lambda r, i: (i, 0))],
        out_specs=pl.BlockSpec((tm, D), lambda r, i: (r * nt + i, 0)),
        out_shape=jax.ShapeDtypeStruct((N_TOK, D), F32),
        compiler_params=_cparams("parallel", "parallel"),
        name="embed_tokens",
    )(xp, xs, pos)


TM = 1024
TILES_PER_GROUP = GROUP_ROWS // TM


def _group_of_tile(i):
    return i // TILES_PER_GROUP


def _normmod_kernel(x_ref, g_ref, sc_ref, sh_ref, o_ref):
    o_ref[...] = _normmod(x_ref[...], g_ref[...], sc_ref[0], sh_ref[0])


def normmod(x, g, sc, sh):
    return pl.pallas_call(
        _normmod_kernel,
        grid=(N_TOK // TM,),
        in_specs=[pl.BlockSpec((TM, D), lambda i: (i, 0)),
                  pl.BlockSpec((1, D), lambda i: (0, 0)),
                  pl.BlockSpec((1, 1, D), lambda i: (_group_of_tile(i), 0, 0)),
                  pl.BlockSpec((1, 1, D), lambda i: (_group_of_tile(i), 0, 0))],
        out_specs=pl.BlockSpec((TM, D), lambda i: (i, 0)),
        out_shape=jax.ShapeDtypeStruct((N_TOK, D), F32),
        compiler_params=_cparams("parallel"),
        name="normmod",
    )(x, g.reshape(1, D), sc, sh)


def _nm_matmul_kernel(x_ref, g_ref, sc_ref, sh_ref, w_ref, b_ref, o_ref, h_scr):
    @pl.when(pl.program_id(1) == 0)
    def _():
        h_scr[...] = _normmod(x_ref[...], g_ref[...], sc_ref[0], sh_ref[0]).astype(BF16)

    o_ref[...] = jnp.dot(h_scr[...], w_ref[...].astype(BF16),
                         preferred_element_type=F32) + b_ref[...]


def nm_matmul(x, g, sc, sh, w, b, n_out, tn=512, name="nm_matmul"):
    return pl.pallas_call(
        _nm_matmul_kernel,
        grid=(N_TOK // TM, n_out // tn),
        in_specs=[pl.BlockSpec((TM, D), lambda i, j: (i, 0)),
                  pl.BlockSpec((1, D), lambda i, j: (0, 0)),
                  pl.BlockSpec((1, 1, D), lambda i, j: (_group_of_tile(i), 0, 0)),
                  pl.BlockSpec((1, 1, D), lambda i, j: (_group_of_tile(i), 0, 0)),
                  pl.BlockSpec((D, tn), lambda i, j: (0, j)),
                  pl.BlockSpec((1, tn), lambda i, j: (0, j))],
        out_specs=pl.BlockSpec((TM, tn), lambda i, j: (i, j)),
        out_shape=jax.ShapeDtypeStruct((N_TOK, n_out), F32),
        scratch_shapes=[pltpu.VMEM((TM, D), BF16)],
        compiler_params=_cparams("parallel", "arbitrary"),
        name=name,
    )(x, g.reshape(1, D), sc, sh, w, b.reshape(1, -1))


S5_G = 64
S5_H = 16
S5_P = 64
S5_L = 16
S5_GB = 8
S5_NB = S5_G // S5_GB
S5_CH = GROUP_ROWS // S5_L
S5_CTX_CH = CTX_LEN // S5_L
S5_NPOW = 8
S5_XW = S5_L * 128
S5_SW = 2 * S5_GB * S5_P


S5_LH = S5_L * S5_H
S5_NPWR = 24


def _s5_prep_kernel(lr_ref, li_ref, ldt_ref, btr_ref, bti_ref, cxr_ref, cxi_ref,
                    sel0_ref, sel1_ref, exp_ref,
                    kk_ref, wst_ref, wout_ref, apr_ref, api_ref):
    dt = jnp.exp(ldt_ref[0])
    lam_r, lam_i = lr_ref[0], li_ref[0]
    ar, ai = lam_r * dt, lam_i * dt
    pw = lax.broadcasted_iota(jnp.int32, (S5_GB, S5_NPWR, 2 * S5_P), 1).astype(F32)
    ep = jnp.exp(pw * ar)
    pwr, pwi = ep * jnp.cos(pw * ai), ep * jnp.sin(pw * ai)
    den = lam_r * lam_r + lam_i * lam_i
    nr, ni = pwr[:, 1:2] - 1.0, pwi[:, 1:2]
    beta_r = (nr * lam_r + ni * lam_i) / den
    beta_i = (ni * lam_r - nr * lam_i) / den
    bbr, bbi = _cmul(beta_r, beta_i, btr_ref[0], bti_ref[0])

    lane = lax.broadcasted_iota(jnp.int32, (S5_H, 2 * S5_P), 1)
    wst_ref[0, 0] = jnp.zeros((S5_XW, S5_SW), BF16)
    half = S5_SW // 2
    for g in range(S5_GB):
        mine = (lane >= S5_P) if g % 2 else (lane < S5_P)
        col = 128 * (g // 2)
        for s in range(S5_L):
            k = S5_L - 1 - s
            wr, wi = _cmul(pwr[g, k:k + 1], pwi[g, k:k + 1], bbr[g], bbi[g])
            rows = slice(128 * s + S5_H * g, 128 * s + S5_H * (g + 1))
            wst_ref[0, 0, rows, col:col + 128] = jnp.where(mine, wr, 0.0).astype(BF16)
            wst_ref[0, 0, rows, half + col:half + col + 128] = jnp.where(mine, wi, 0.0).astype(BF16)

    lane1 = lax.broadcasted_iota(jnp.int32, (1, 2 * S5_P), 1)

    def group_lanes(a):
        return jnp.concatenate([jnp.where(lane1 < S5_P, a[2 * q], a[2 * q + 1])
                                for q in range(S5_GB // 2)], axis=1)

    pr, pi_ = pwr[:, S5_L:S5_L + 1], pwi[:, S5_L:S5_L + 1]
    for k in range(S5_NPOW):
        apr_ref[0, 0, k:k + 1, :] = group_lanes(pr)
        api_ref[0, 0, k:k + 1, :] = group_lanes(pi_)
        pr, pi_ = _cmul(pr, pi_, pr, pi_)

    tn = (((0,), (0,)), ((), ()))
    glane = (lax.broadcasted_iota(jnp.int32, (1, S5_XW), 1) % 128) // S5_H
    expand = exp_ref[...]
    for g in range(S5_GB):
        pg_r, pg_i = pwr[g, :, :S5_P], pwi[g, :, :S5_P]
        cr, ci = cxr_ref[0, g], cxi_ref[0, g]
        own = glane == g

        def c_times_pow(sel):
            er = lax.dot_general(pg_r, sel, tn, precision=HIGHEST, preferred_element_type=F32)
            ei = lax.dot_general(pg_i, sel, tn, precision=HIGHEST, preferred_element_type=F32)
            return _cmul(cr, ci, er, ei)

        k_r, k_i = c_times_pow(sel0_ref[...])
        kk = (jnp.dot(bbr[g, :, :S5_P], k_r, precision=HIGHEST, preferred_element_type=F32)
              - jnp.dot(bbi[g, :, :S5_P], k_i, precision=HIGHEST, preferred_element_type=F32))
        kk8 = jnp.dot(kk.astype(BF16), expand, preferred_element_type=F32)
        kk_ref[0, 0, S5_H * g:S5_H * (g + 1), :] = jnp.where(own, kk8, 0.0).astype(BF16)
        o_r, o_i = c_times_pow(sel1_ref[...])
        wr8 = jnp.dot(o_r.astype(BF16), expand, preferred_element_type=F32)
        wi8 = jnp.dot((-o_i).astype(BF16), expand, preferred_element_type=F32)
        wout_ref[0, 0, S5_P * g:S5_P * (g + 1), :] = jnp.where(own, wr8, 0.0).astype(BF16)
        wout_ref[0, 0, half + S5_P * g:half + S5_P * (g + 1), :] = jnp.where(own, wi8, 0.0).astype(BF16)


def s5_prepare(lam_re, lam_im, log_dt, b_re, b_im, c_re, c_im):
    half = S5_SW // 2
    lr = jnp.tile(lam_re.reshape(2, S5_G, 1, S5_P), (1, 1, 1, 2))
    li = jnp.tile(lam_im.reshape(2, S5_G, 1, S5_P), (1, 1, 1, 2))
    ldt = log_dt.reshape(2, S5_G, 1, 1)
    btr = jnp.tile(jnp.swapaxes(b_re, 2, 3), (1, 1, 1, 2))
    bti = jnp.tile(jnp.swapaxes(b_im, 2, 3), (1, 1, 1, 2))
    cxr = jnp.tile(jnp.swapaxes(c_re, 2, 3), (1, 1, 1, S5_L))
    cxi = jnp.tile(jnp.swapaxes(c_im, 2, 3), (1, 1, 1, S5_L))
    k = jnp.arange(S5_NPWR)[:, None]
    t = (jnp.arange(S5_LH) // S5_H)[None, :]
    sel0 = (k == t).astype(F32)
    sel1 = (k == t + 1).astype(F32)
    src = jnp.arange(S5_LH)[:, None]
    dst = jnp.arange(S5_XW)[None, :]
    expand = ((src // S5_H == dst // 128) & (src % S5_H == dst % S5_H)).astype(BF16)

    def spec(*tail):
        return pl.BlockSpec((1, S5_GB) + tail, lambda d, j: (d, j) + (0,) * len(tail))

    def const(shape):
        return pl.BlockSpec(shape, lambda d, j: (0,) * len(shape))

    def blk(*tail):
        return pl.BlockSpec((1, 1) + tail, lambda d, j: (d, j) + (0,) * len(tail))

    return pl.pallas_call(
        _s5_prep_kernel,
        grid=(2, S5_NB),
        in_specs=[spec(1, 2 * S5_P), spec(1, 2 * S5_P), spec(1, 1),
                  spec(S5_H, 2 * S5_P), spec(S5_H, 2 * S5_P), spec(S5_P, S5_LH), spec(S5_P, S5_LH),
                  const((S5_NPWR, S5_LH)), const((S5_NPWR, S5_LH)), const((S5_LH, S5_XW))],
        out_specs=[blk(128, S5_XW), blk(S5_XW, S5_SW), blk(S5_SW, S5_XW),
                   blk(S5_NPOW, half), blk(S5_NPOW, half)],
        out_shape=[jax.ShapeDtypeStruct((2, S5_NB, 128, S5_XW), BF16),
                   jax.ShapeDtypeStruct((2, S5_NB, S5_XW, S5_SW), BF16),
                   jax.ShapeDtypeStruct((2, S5_NB, S5_SW, S5_XW), BF16),
                   jax.ShapeDtypeStruct((2, S5_NB, S5_NPOW, half), F32),
                   jax.ShapeDtypeStruct((2, S5_NB, S5_NPOW, half), F32)],
        compiler_params=_cparams("parallel", "parallel"),
        name="s5_prepare",
    )(lr, li, ldt, btr, bti, cxr, cxi, sel0, sel1, expand)


def _s5_scan_body(d, r, h_ref, kk_ref, wst_ref, wout_ref, apr_ref, api_ref, h0r_ref, h0i_ref,
                  y_ref, fr_ref, fi_ref, m8, zr_s, zi_s):
    half = S5_SW // 2

    @pl.when(r == 0)
    def _build():
        for s in range(S5_L):
            if s:
                m8[128 * s:128 * (s + 1), 0:128 * s] = jnp.zeros((128, 128 * s), BF16)
            m8[128 * s:128 * (s + 1), 128 * s:] = kk_ref[0, 0, :, :S5_XW - 128 * s]

    def tloc(s):
        return s if d == 0 else S5_L - 1 - s

    slabs = [h_ref[pl.ds(tloc(s), S5_CH, stride=S5_L), :] for s in range(S5_L)]
    x8 = jnp.concatenate(slabs, axis=1).astype(BF16)
    delta = jnp.dot(x8, wst_ref[0, 0], preferred_element_type=F32)
    zr, zi = delta[:, :half], delta[:, half:]

    row = lax.broadcasted_iota(jnp.int32, (S5_CH, 1), 0)
    is_ctx = r == 0
    pos = jnp.where(is_ctx, row & (S5_CTX_CH - 1), row)
    last = jnp.where(is_ctx, S5_CTX_CH - 1, S5_CH - 1)
    a = jnp.maximum(r - 1, 0)
    lat = (r > 0).astype(F32)
    h0r = h0r_ref[0, 0, pl.ds(a, 1), :] * lat
    h0i = h0i_ref[0, 0, pl.ds(a, 1), :] * lat
    first = (pos == 0) if d == 0 else (pos == last)
    ir, ii = _cmul(apr_ref[0, 0, 0:1, :], api_ref[0, 0, 0:1, :], h0r, h0i)
    zr = zr + jnp.where(first, ir, 0.0)
    zi = zi + jnp.where(first, ii, 0.0)
    for k in range(S5_NPOW):
        m = 1 << k
        akr, aki = apr_ref[0, 0, k:k + 1, :], api_ref[0, 0, k:k + 1, :]
        if d == 0:
            sr, si = pltpu.roll(zr, m, 0), pltpu.roll(zi, m, 0)
            valid = pos >= m
        else:
            sr, si = pltpu.roll(zr, S5_CH - m, 0), pltpu.roll(zi, S5_CH - m, 0)
            valid = pos <= last - m
        pr, pi_ = _cmul(akr, aki, sr, si)
        zr = zr + jnp.where(valid, pr, 0.0)
        zi = zi + jnp.where(valid, pi_, 0.0)
    if d == 0:
        sr, si = pltpu.roll(zr, 1, 0), pltpu.roll(zi, 1, 0)
    else:
        sr, si = pltpu.roll(zr, S5_CH - 1, 0), pltpu.roll(zi, S5_CH - 1, 0)
    sr = jnp.where(first, h0r, sr)
    si = jnp.where(first, h0i, si)
    s_in = jnp.concatenate([sr, si], axis=1).astype(BF16)
    y8 = (jnp.dot(x8, m8[...], preferred_element_type=F32)
          + jnp.dot(s_in, wout_ref[0, 0], preferred_element_type=F32))
    for t in range(S5_L):
        y_ref[0, pl.ds(tloc(t), S5_CH, stride=S5_L), :] = y8[:, 128 * t:128 * (t + 1)]

    @pl.when(r == 0)
    def _fin():
        off = S5_CTX_CH - 1 if d == 0 else 0
        for q in range(half // 128):
            zr_s[q] = zr[:, 128 * q:128 * (q + 1)]
            zi_s[q] = zi[:, 128 * q:128 * (q + 1)]
            fr_ref[0, 0, :, 128 * q:128 * (q + 1)] = zr_s[q, pl.ds(off, N_CTX_SEQ, stride=S5_CTX_CH), :]
            fi_ref[0, 0, :, 128 * q:128 * (q + 1)] = zi_s[q, pl.ds(off, N_CTX_SEQ, stride=S5_CTX_CH), :]


def _s5_scan_kernel(*refs):
    d = pl.program_id(1)
    r = pl.program_id(2)

    @pl.when(d == 0)
    def _():
        _s5_scan_body(0, r, *refs)

    @pl.when(d == 1)
    def _():
        _s5_scan_body(1, r, *refs)


def s5_scan(h, kk8, wst8, wout8, apr, api, h0r, h0i):
    half = S5_SW // 2
    return pl.pallas_call(
        _s5_scan_kernel,
        grid=(S5_NB, 2, N_GROUPS),
        in_specs=[pl.BlockSpec((GROUP_ROWS, 128), lambda j, d, r: (r, j)),
                  pl.BlockSpec((1, 1, 128, S5_XW), lambda j, d, r: (d, j, 0, 0)),
                  pl.BlockSpec((1, 1, S5_XW, S5_SW), lambda j, d, r: (d, j, 0, 0)),
                  pl.BlockSpec((1, 1, S5_SW, S5_XW), lambda j, d, r: (d, j, 0, 0)),
                  pl.BlockSpec((1, 1, S5_NPOW, half), lambda j, d, r: (d, j, 0, 0)),
                  pl.BlockSpec((1, 1, S5_NPOW, half), lambda j, d, r: (d, j, 0, 0)),
                  pl.BlockSpec((1, 1, N_LAT_SEQ, half), lambda j, d, r: (d, j, 0, 0)),
                  pl.BlockSpec((1, 1, N_LAT_SEQ, half), lambda j, d, r: (d, j, 0, 0))],
        out_specs=[pl.BlockSpec((1, GROUP_ROWS, 128), lambda j, d, r: (d, r, j)),
                   pl.BlockSpec((1, 1, N_CTX_SEQ, half), lambda j, d, r: (d, j, 0, 0)),
                   pl.BlockSpec((1, 1, N_CTX_SEQ, half), lambda j, d, r: (d, j, 0, 0))],
        out_shape=[jax.ShapeDtypeStruct((2, N_TOK, D), F32),
                   jax.ShapeDtypeStruct((2, S5_NB, N_CTX_SEQ, half), F32),
                   jax.ShapeDtypeStruct((2, S5_NB, N_CTX_SEQ, half), F32)],
        scratch_shapes=[pltpu.VMEM((S5_XW, S5_XW), BF16),
                        pltpu.VMEM((half // 128, S5_CH, 128), F32),
                        pltpu.VMEM((half // 128, S5_CH, 128), F32)],
        compiler_params=_cparams("arbitrary", "arbitrary", "arbitrary"),
        name="s5_scan",
    )(h, kk8, wst8, wout8, apr, api, h0r, h0i)


def _s5_glu_kernel(h_ref, y0_ref, y1_ref, dsk_ref, wa_ref, wb_ref, ba_ref, bb_ref, x_ref, g1_ref,
                   o_ref, yg_scr):
    @pl.when(pl.program_id(1) == 0)
    def _():
        y = dsk_ref[...] * h_ref[...] + y0_ref[0] + y1_ref[0]
        yg_scr[...] = jax.nn.gelu(y).astype(BF16)

    yg = yg_scr[...]
    a = jnp.dot(yg, wa_ref[...].astype(BF16), preferred_element_type=F32) + ba_ref[...]
    b = jnp.dot(yg, wb_ref[...].astype(BF16), preferred_element_type=F32) + bb_ref[...]
    o_ref[...] = x_ref[...] + g1_ref[0] * (a * jax.nn.sigmoid(b))


def s5_glu(h, y, d_skip, w_glu, b_glu, x, g1, tn=512):
    nj = D // tn
    b2 = b_glu.reshape(1, 2 * D)
    return pl.pallas_call(
        _s5_glu_kernel,
        grid=(N_TOK // TM, nj),
        in_specs=[pl.BlockSpec((TM, D), lambda i, j: (i, 0)),
                  pl.BlockSpec((1, TM, D), lambda i, j: (0, i, 0)),
                  pl.BlockSpec((1, TM, D), lambda i, j: (1, i, 0)),
                  pl.BlockSpec((1, D), lambda i, j: (0, 0)),
                  pl.BlockSpec((D, tn), lambda i, j: (0, j)),
                  pl.BlockSpec((D, tn), lambda i, j: (0, nj + j)),
                  pl.BlockSpec((1, tn), lambda i, j: (0, j)),
                  pl.BlockSpec((1, tn), lambda i, j: (0, nj + j)),
                  pl.BlockSpec((TM, tn), lambda i, j: (i, j)),
                  pl.BlockSpec((1, 1, tn), lambda i, j: (_group_of_tile(i), 0, j))],
        out_specs=pl.BlockSpec((TM, tn), lambda i, j: (i, j)),
        out_shape=jax.ShapeDtypeStruct((N_TOK, D), F32),
        scratch_shapes=[pltpu.VMEM((TM, D), BF16)],
        compiler_params=_cparams("parallel", "arbitrary"),
        name="s5_glu",
    )(h, y, y, d_skip.reshape(1, D), w_glu, w_glu, b2, b2, x, g1)


def s5_layer(x, mods, g_norm, st_re, st_im, lam_re, lam_im, log_dt, b_re, b_im, c_re, c_im,
             d_skip, w_glu, b_glu):
    sh1, sc1, g1 = mods
    h = normmod(x, g_norm, sc1, sh1)
    kk8, wst8, wout8, apr, api = s5_prepare(lam_re, lam_im, log_dt, b_re, b_im, c_re, c_im)

    def h0(st):
        return jnp.transpose(st.reshape(N_LAT_SEQ, 2, S5_NB, S5_GB * S5_P), (1, 2, 0, 3))

    y, fr, fi = s5_scan(h, kk8, wst8, wout8, apr, api, h0(st_re), h0(st_im))

    def fin(f):
        return jnp.transpose(f.reshape(2, S5_NB, N_CTX_SEQ, S5_GB, S5_P), (2, 0, 1, 3, 4)
                             ).reshape(N_CTX_SEQ, 2, S5_G, S5_P)

    x = s5_glu(h, y, d_skip, w_glu, b_glu, x, g1)
    return x, fin(fr), fin(fi)


CHUNK = 64
N_CHUNKS = N_TOK // CHUNK
CTX_CHUNKS = N_CTX_SEQ * CTX_LEN // CHUNK
CH_PER_CTX = CTX_LEN // CHUNK
CH_PER_LAT = LAT_LEN // CHUNK


def _chunk_of_step(d, c):
    return jnp.where(d == 0, c, N_CHUNKS - 1 - c)


def _seq_of_chunk(ce):
    return jnp.where(ce < CTX_CHUNKS, ce // CH_PER_CTX, N_CTX_SEQ + (ce - CTX_CHUNKS) // CH_PER_LAT)


def _chunk_flags(d, ce):
    is_ctx = ce < CTX_CHUNKS
    pos = jnp.where(is_ctx, ce % CH_PER_CTX, (ce - CTX_CHUNKS) % CH_PER_LAT)
    n = jnp.where(is_ctx, CH_PER_CTX, CH_PER_LAT)
    t_first, t_last = pos == 0, pos == n - 1
    if d == 0:
        return is_ctx, t_first, t_last
    return is_ctx, t_last, t_first


def _tri(d, shape, row_axis=0, col_axis=1):
    r = lax.broadcasted_iota(jnp.int32, shape, row_axis)
    c = lax.broadcasted_iota(jnp.int32, shape, col_axis)
    return (r >= c) if d == 0 else (r <= c)


HG_H = 8
HG_K = 128


def _hg_scan_body(d, layer, q_ref, v_ref, z_ref, lbl_ref, s0_ref, o_ref, fin_ref, st_scr):
    ce = _chunk_of_step(d, pl.program_id(1))
    is_ctx, starts, ends = _chunk_flags(d, ce)

    @pl.when(starts & is_ctx)
    def _():
        st_scr[...] = jnp.zeros_like(st_scr)

    @pl.when(starts & jnp.logical_not(is_ctx))
    def _():
        st_scr[...] = s0_ref[0, 0]

    lg = lbl_ref[0]
    e = jnp.exp(lg - jnp.max(lg, axis=0, keepdims=True))
    sm = e / jnp.sum(e, axis=0, keepdims=True)
    lb = jnp.sum(sm[1:layer + 1], axis=0, keepdims=True)
    f = lb + (1.0 - lb) * jax.nn.sigmoid(z_ref[...])
    g = jnp.log(f)
    kk = 1.0 - f
    tri = _tri(d, (CHUNK, CHUNK))
    cum = jnp.dot(tri.astype(F32), g, precision=HIGHEST, preferred_element_type=F32)
    tot = cum[CHUNK - 1:CHUNK] if d == 0 else cum[0:1]
    mid = cum[CHUNK // 2 - 1:CHUNK // 2]
    q = q_ref[...]
    v = v_ref[...]
    qa = (q * jnp.exp(cum - mid)).astype(BF16)
    ka = (kk * jnp.exp(mid - cum)).astype(BF16)
    qs = (q * jnp.exp(cum)).astype(BF16)
    kd = (kk * jnp.exp(tot - cum)).astype(BF16)
    vb = v.astype(BF16)
    etot = jnp.exp(tot)
    nt = (((1,), (1,)), ((), ()))
    tn = (((0,), (0,)), ((), ()))
    for hd in range(HG_H):
        sl = slice(HG_K * hd, HG_K * (hd + 1))
        a = lax.dot_general(qa[:, sl], ka[:, sl], nt, preferred_element_type=F32)
        a = jnp.where(tri, a, 0.0).astype(BF16)
        st = st_scr[hd]
        o = (jnp.dot(a, vb[:, sl], preferred_element_type=F32)
             + lax.dot_general(qs[:, sl], st.astype(BF16), nt, preferred_element_type=F32))
        o_ref[0, :, sl] = o
        st_scr[hd] = st * etot[:, sl] + lax.dot_general(vb[:, sl], kd[:, sl], tn,
                                                        preferred_element_type=F32)

    @pl.when(ends)
    def _():
        for hd in range(HG_H):
            fin_ref[0, 0, hd] = st_scr[hd].T


def _hg_scan_kernel(*refs, layer):
    d = pl.program_id(0)

    @pl.when(d == 0)
    def _():
        _hg_scan_body(0, layer, *refs)

    @pl.when(d == 1)
    def _():
        _hg_scan_body(1, layer, *refs)


def hg_scan(proj, lb_logits, s0t, layer):
    def tok(col):
        return lambda d, c: (_chunk_of_step(d, c), col)

    def lat_idx(d, c):
        return jnp.maximum(_seq_of_chunk(_chunk_of_step(d, c)) - N_CTX_SEQ, 0)

    def fin_idx(d, c):
        return jnp.minimum(_seq_of_chunk(_chunk_of_step(d, c)), N_CTX_SEQ)

    return pl.pallas_call(
        functools.partial(_hg_scan_kernel, layer=layer),
        grid=(2, N_CHUNKS),
        in_specs=[pl.BlockSpec((CHUNK, D), tok(0)),
                  pl.BlockSpec((CHUNK, D), tok(1)),
                  pl.BlockSpec((CHUNK, D), lambda d, c: (_chunk_of_step(d, c), 3 + d)),
                  pl.BlockSpec((1, DEPTH, D), lambda d, c: (d, 0, 0)),
                  pl.BlockSpec((1, 1, HG_H, HG_K, HG_K), lambda d, c: (d, lat_idx(d, c), 0, 0, 0))],
        out_specs=[pl.BlockSpec((1, CHUNK, D), lambda d, c: (d, _chunk_of_step(d, c), 0)),
                   pl.BlockSpec((1, 1, HG_H, HG_K, HG_K), lambda d, c: (fin_idx(d, c), d, 0, 0, 0))],
        out_shape=[jax.ShapeDtypeStruct((2, N_TOK, D), F32),
                   jax.ShapeDtypeStruct((N_CTX_SEQ + 1, 2, HG_H, HG_K, HG_K), F32)],
        scratch_shapes=[pltpu.VMEM((HG_H, HG_K, HG_K), F32)],
        compiler_params=_cparams("arbitrary", "arbitrary"),
        name="hg_scan",
    )(proj, proj, proj, lb_logits, s0t)


def _hg_out_kernel(o0_ref, o1_ref, gate_ref, gn_ref, w_ref, x_ref, g1_ref, out_ref, on_scr):
    @pl.when(pl.program_id(1) == 0)
    def _():
        for hd in range(HG_H):
            sl = slice(HG_K * hd, HG_K * (hd + 1))
            o = o0_ref[0, :, sl] + o1_ref[0, :, sl]
            o = o * lax.rsqrt(jnp.mean(o * o, axis=-1, keepdims=True) + NORM_EPS) * gn_ref[...]
            on_scr[:, sl] = (o * _silu(gate_ref[:, sl])).astype(BF16)

    out_ref[...] = x_ref[...] + g1_ref[0] * jnp.dot(on_scr[...], w_ref[...].astype(BF16),
                                                    preferred_element_type=F32)


def hg_out(o, proj, g_norm, w_o, x, g1, tn=512):
    return pl.pallas_call(
        _hg_out_kernel,
        grid=(N_TOK // TM, D // tn),
        in_specs=[pl.BlockSpec((1, TM, D), lambda i, j: (0, i, 0)),
                  pl.BlockSpec((1, TM, D), lambda i, j: (1, i, 0)),
                  pl.BlockSpec((TM, D), lambda i, j: (i, 2)),
                  pl.BlockSpec((1, HG_K), lambda i, j: (0, 0)),
                  pl.BlockSpec((D, tn), lambda i, j: (0, j)),
                  pl.BlockSpec((TM, tn), lambda i, j: (i, j)),
                  pl.BlockSpec((1, 1, tn), lambda i, j: (_group_of_tile(i), 0, j))],
        out_specs=pl.BlockSpec((TM, tn), lambda i, j: (i, j)),
        out_shape=jax.ShapeDtypeStruct((N_TOK, D), F32),
        scratch_shapes=[pltpu.VMEM((TM, D), BF16)],
        compiler_params=_cparams("parallel", "arbitrary"),
        name="hg_out",
    )(o, o, proj, g_norm.reshape(1, HG_K), w_o, x, g1)


def hgrn_layer(x, mods, g_mix, state, layer, lb_logits, w_qig, w_f, b_f, g_norm, w_o):
    sh1, sc1, g1 = mods
    w5 = jnp.concatenate([w_qig, w_f[0], w_f[1]], axis=1)
    b5 = jnp.concatenate([jnp.zeros((3 * D,), F32), b_f[0], b_f[1]])
    proj = nm_matmul(x, g_mix, sc1, sh1, w5, b5, 5 * D, name="hg_proj")
    s0t = jnp.transpose(state, (1, 0, 2, 4, 3))
    o, fin = hg_scan(proj, lb_logits, s0t, layer)
    x = hg_out(o, proj, g_norm, w_o, x, g1)
    return x, fin[:N_CTX_SEQ]


SSD_INNER = 2 * D
SSD_HEADS = 32
SSD_P = 64
SSD_NG = 4
SSD_N = 128
SSD_XBC = SSD_INNER + 2 * SSD_NG * SSD_N
SSD_ZX = SSD_INNER + SSD_XBC
SSD_CONV = 5
CONV_TM = 256
CONV_HALO = 8


def _ssd_conv_kernel(cur_ref, prev_ref, next_ref, w_ref, b_ref, o_ref, ext):
    i = pl.program_id(0)
    n_ctx_tiles = N_CTX_SEQ * CTX_LEN // CONV_TM
    per_lat = LAT_LEN // CONV_TM
    is_ctx = i < n_ctx_tiles
    k = (i - n_ctx_tiles) % per_lat
    seq_start = is_ctx | (k == 0)
    seq_end = is_ctx | (k == per_lat - 1)
    ext[0:CONV_HALO] = jnp.where(seq_start, 0.0, prev_ref[...])
    ext[CONV_HALO:CONV_HALO + CONV_TM] = cur_ref[...]
    ext[CONV_HALO + CONV_TM:] = jnp.where(seq_end, 0.0, next_ref[...])
    acc = jnp.broadcast_to(b_ref[...], (CONV_TM, D))
    for t in range(SSD_CONV):
        acc = acc + w_ref[t:t + 1, :] * ext[pl.ds(CONV_HALO - SSD_CONV // 2 + t, CONV_TM), :]
    o_ref[...] = _silu(acc)


def ssd_conv(zx, conv_w, conv_b):
    nrb = N_TOK // CONV_HALO
    rpt = CONV_TM // CONV_HALO
    c0 = SSD_INNER // D
    return pl.pallas_call(
        _ssd_conv_kernel,
        grid=(N_TOK // CONV_TM, SSD_XBC // D),
        in_specs=[pl.BlockSpec((CONV_TM, D), lambda i, j: (i, c0 + j)),
                  pl.BlockSpec((CONV_HALO, D), lambda i, j: (jnp.maximum(i * rpt - 1, 0), c0 + j)),
                  pl.BlockSpec((CONV_HALO, D), lambda i, j: (jnp.minimum((i + 1) * rpt, nrb - 1), c0 + j)),
                  pl.BlockSpec((SSD_CONV, D), lambda i, j: (0, j)),
                  pl.BlockSpec((1, D), lambda i, j: (0, j))],
        out_specs=pl.BlockSpec((CONV_TM, D), lambda i, j: (i, j)),
        out_shape=jax.ShapeDtypeStruct((N_TOK, SSD_XBC), F32),
        scratch_shapes=[pltpu.VMEM((CONV_TM + 2 * CONV_HALO, D), F32)],
        compiler_params=_cparams("parallel", "parallel"),
        name="ssd_conv",
    )(zx, zx, zx, conv_w, conv_b.reshape(1, SSD_XBC))


def _ssd_scan_body(d, xlo_ref, xhi_ref, bc_ref, dtr_ref, dtb_ref, alog_ref, h0_ref,
                   y_ref, fin_ref, ht_scr):
    ce = _chunk_of_step(d, pl.program_id(1))
    is_ctx, starts, ends = _chunk_flags(d, ce)

    @pl.when(starts & is_ctx)
    def _():
        ht_scr[...] = jnp.zeros_like(ht_scr)

    @pl.when(starts & jnp.logical_not(is_ctx))
    def _():
        ht_scr[...] = h0_ref[0, 0]

    xr = dtr_ref[...] + dtb_ref[...]
    dt = jnp.maximum(xr, 0.0) + jnp.log(1.0 + jnp.exp(-jnp.abs(xr)))
    dta = dt * (-jnp.exp(alog_ref[...]))
    tri = _tri(d, (CHUNK, CHUNK))
    cum = jnp.dot(tri.astype(F32), dta, precision=HIGHEST, preferred_element_type=F32)
    r = lax.broadcasted_iota(jnp.int32, (CHUNK, 2 * CHUNK), 0)
    cc = lax.broadcasted_iota(jnp.int32, (CHUNK, 2 * CHUNK), 1)
    lo_half = cc < CHUNK
    ccm = jnp.where(lo_half, cc, cc - CHUNK)
    trit = (ccm >= r) if d == 0 else (ccm <= r)
    tn = (((0,), (0,)), ((), ()))
    nt = (((1,), (1,)), ((), ()))
    cumt_lo = lax.dot_general(dta, (trit & lo_half).astype(F32), tn, precision=HIGHEST,
                              preferred_element_type=F32)
    cumt_hi = lax.dot_general(dta, (trit & jnp.logical_not(lo_half)).astype(F32), tn,
                              precision=HIGHEST, preferred_element_type=F32)
    lane = lax.broadcasted_iota(jnp.int32, (CHUNK, 2 * SSD_P), 1)
    first_head = lane < SSD_P
    tri2 = (r >= ccm) if d == 0 else (r <= ccm)
    bc = bc_ref[...]
    for gq in range(SSD_NG):
        bg = bc[:, SSD_N * gq:SSD_N * (gq + 1)].astype(BF16)
        cg = bc[:, SSD_NG * SSD_N + SSD_N * gq:SSD_NG * SSD_N + SSD_N * (gq + 1)].astype(BF16)
        cb2 = lax.dot_general(cg, jnp.concatenate([bg, bg], axis=0), nt,
                              preferred_element_type=F32)
        for pp in range(4 * gq, 4 * gq + 4):
            h1 = SSD_HEADS * d + 2 * pp
            colp = jnp.where(first_head, cum[:, h1:h1 + 1], cum[:, h1 + 1:h1 + 2])
            rowp = cumt_lo[h1:h1 + 1, :] + cumt_hi[h1 + 1:h1 + 2, :]
            lmat = jnp.exp(jnp.where(tri2, colp - rowp, -jnp.inf))
            dtp = jnp.where(first_head, dt[:, h1:h1 + 1], dt[:, h1 + 1:h1 + 2])
            xref = xlo_ref if pp < 8 else xhi_ref
            c0 = 128 * (pp % 8)
            xdt = xref[:, c0:c0 + 128] * dtp
            rhs = jnp.concatenate([jnp.where(first_head, xdt, 0.0),
                                   jnp.where(first_head, 0.0, xdt)], axis=0).astype(BF16)
            y = jnp.dot((cb2 * lmat).astype(BF16), rhs, preferred_element_type=F32)
            ht = ht_scr[:, 128 * pp:128 * (pp + 1)]
            y = y + jnp.dot(cg, ht.astype(BF16), preferred_element_type=F32) * jnp.exp(colp)
            y_ref[0, :, 128 * pp:128 * (pp + 1)] = y
            totp = colp[CHUNK - 1:CHUNK] if d == 0 else colp[0:1]
            xw = (xdt * jnp.exp(totp - colp)).astype(BF16)
            ht_scr[:, 128 * pp:128 * (pp + 1)] = (
                ht * jnp.exp(totp) + lax.dot_general(bg, xw, tn, preferred_element_type=F32))

    @pl.when(ends)
    def _():
        fin_ref[0, 0] = ht_scr[...]


def _ssd_scan_kernel(*refs):
    d = pl.program_id(0)

    @pl.when(d == 0)
    def _():
        _ssd_scan_body(0, *refs)

    @pl.when(d == 1)
    def _():
        _ssd_scan_body(1, *refs)


def ssd_scan(xbc, dtr, dt_bias, a_log, h0t):
    nh2 = 2 * SSD_HEADS
    hp = SSD_HEADS * SSD_P

    def tok(col):
        return lambda d, c: (_chunk_of_step(d, c), col)

    def lat_idx(d, c):
        return jnp.maximum(_seq_of_chunk(_chunk_of_step(d, c)) - N_CTX_SEQ, 0)

    def fin_idx(d, c):
        return jnp.minimum(_seq_of_chunk(_chunk_of_step(d, c)), N_CTX_SEQ)

    return pl.pallas_call(
        _ssd_scan_kernel,
        grid=(2, N_CHUNKS),
        in_specs=[pl.BlockSpec((CHUNK, D), tok(0)),
                  pl.BlockSpec((CHUNK, D), tok(1)),
                  pl.BlockSpec((CHUNK, D), tok(2)),
                  pl.BlockSpec((CHUNK, nh2), tok(0)),
                  pl.BlockSpec((1, nh2), lambda d, c: (0, 0)),
                  pl.BlockSpec((1, nh2), lambda d, c: (0, 0)),
                  pl.BlockSpec((1, 1, SSD_N, hp), lambda d, c: (d, lat_idx(d, c), 0, 0))],
        out_specs=[pl.BlockSpec((1, CHUNK, hp), lambda d, c: (d, _chunk_of_step(d, c), 0)),
                   pl.BlockSpec((1, 1, SSD_N, hp), lambda d, c: (fin_idx(d, c), d, 0, 0))],
        out_shape=[jax.ShapeDtypeStruct((2, N_TOK, hp), F32),
                   jax.ShapeDtypeStruct((N_CTX_SEQ + 1, 2, SSD_N, hp), F32)],
        scratch_shapes=[pltpu.VMEM((SSD_N, hp), F32)],
        compiler_params=_cparams("arbitrary", "arbitrary"),
        name="ssd_scan",
    )(xbc, xbc, xbc, dtr, dt_bias.reshape(1, nh2), a_log.reshape(1, nh2), h0t)


SSD_OUT_TM = 512


def _ssd_out_kernel(xlo_ref, xhi_ref, zlo_ref, zhi_ref, y0_ref, y1_ref, dsk_ref, gn_ref, w_ref,
                    x_ref, g1_ref, out_ref, yn_scr):
    @pl.when(pl.program_id(1) == 0)
    def _():
        halves = []
        ss = jnp.zeros((SSD_OUT_TM, 1), F32)
        for k, (xr, zr) in enumerate(((xlo_ref, zlo_ref), (xhi_ref, zhi_ref))):
            sl = slice(D * k, D * (k + 1))
            y = dsk_ref[:, sl] * xr[...] + y0_ref[0, :, sl] + y1_ref[0, :, sl]
            y = y * _silu(zr[...])
            ss = ss + jnp.sum(y * y, axis=-1, keepdims=True)
            halves.append(y)
        scale = lax.rsqrt(ss / SSD_INNER + NORM_EPS)
        for k, y in enumerate(halves):
            sl = slice(D * k, D * (k + 1))
            yn_scr[:, sl] = (y * scale * gn_ref[:, sl]).astype(BF16)

    out_ref[...] = x_ref[...] + g1_ref[0] * jnp.dot(yn_scr[...], w_ref[...].astype(BF16),
                                                    preferred_element_type=F32)


def ssd_out(xbc, zx, y, d_skip_cols, g_norm, w_out, x, g1, tn=512):
    tm = SSD_OUT_TM
    tpg = GROUP_ROWS // tm
    return pl.pallas_call(
        _ssd_out_kernel,
        grid=(N_TOK // tm, D // tn),
        in_specs=[pl.BlockSpec((tm, D), lambda i, j: (i, 0)),
                  pl.BlockSpec((tm, D), lambda i, j: (i, 1)),
                  pl.BlockSpec((tm, D), lambda i, j: (i, 0)),
                  pl.BlockSpec((tm, D), lambda i, j: (i, 1)),
                  pl.BlockSpec((1, tm, SSD_INNER), lambda i, j: (0, i, 0)),
                  pl.BlockSpec((1, tm, SSD_INNER), lambda i, j: (1, i, 0)),
                  pl.BlockSpec((1, SSD_INNER), lambda i, j: (0, 0)),
                  pl.BlockSpec((1, SSD_INNER), lambda i, j: (0, 0)),
                  pl.BlockSpec((SSD_INNER, tn), lambda i, j: (0, j)),
                  pl.BlockSpec((tm, tn), lambda i, j: (i, j)),
                  pl.BlockSpec((1, 1, tn), lambda i, j: (i // tpg, 0, j))],
        out_specs=pl.BlockSpec((tm, tn), lambda i, j: (i, j)),
        out_shape=jax.ShapeDtypeStruct((N_TOK, D), F32),
        scratch_shapes=[pltpu.VMEM((tm, SSD_INNER), BF16)],
        compiler_params=_cparams("parallel", "arbitrary"),
        name="ssd_out",
    )(xbc, xbc, zx, zx, y, y, d_skip_cols, g_norm.reshape(1, SSD_INNER), w_out, x, g1)


def ssd_layer(x, mods, g_mix, state, w_in, conv_w, conv_b, dt_bias, a_log, d_skip, g_norm, w_out):
    sh1, sc1, g1 = mods
    zx = nm_matmul(x, g_mix, sc1, sh1, w_in, jnp.zeros((SSD_ZX,), F32), SSD_ZX, name="ssd_proj")
    nh2 = 2 * SSD_HEADS
    dtr = nm_matmul(x, g_mix, sc1, sh1, w_in[:, SSD_ZX:], jnp.zeros((nh2,), F32), nh2, tn=nh2,
                    name="ssd_proj_dt")
    xbc = ssd_conv(zx, conv_w, conv_b)
    h0t = jnp.transpose(state, (1, 0, 4, 2, 3)).reshape(2, N_LAT_SEQ, SSD_N, SSD_HEADS * SSD_P)
    y, fin = ssd_scan(xbc, dtr, dt_bias, a_log, h0t)
    dcols = jnp.repeat(d_skip, SSD_P).reshape(1, SSD_INNER)
    x = ssd_out(xbc, zx, y, dcols, g_norm, w_out, x, g1)
    fin = jnp.transpose(fin[:N_CTX_SEQ].reshape(N_CTX_SEQ, 2, SSD_N, SSD_HEADS, SSD_P), (0, 1, 3, 4, 2))
    return x, fin


N_EXP = 16
FF = 2 * D
CAP_CTX = 2 * CTX_LEN // N_EXP
CAP_LAT = 2 * LAT_LEN // N_EXP
SLOTS_PER_GROUP = 512
SLOTS = N_GROUPS * SLOTS_PER_GROUP


def _router_kernel(x_ref, g_ref, sc_ref, sh_ref, wt_ref, h_ref, aff_ref):
    h = _normmod(x_ref[...], g_ref[...], sc_ref[0], sh_ref[0])
    h_ref[...] = h.astype(BF16)
    logits = lax.dot_general(wt_ref[...], h, (((1,), (1,)), ((), ())), precision=HIGHEST,
                             preferred_element_type=F32)
    e = jnp.exp(logits - jnp.max(logits, axis=0, keepdims=True))
    aff_ref[...] = e / jnp.sum(e, axis=0, keepdims=True)


def moe_route(x, g, sc, sh, w_router_t):
    return pl.pallas_call(
        _router_kernel,
        grid=(N_TOK // TM,),
        in_specs=[pl.BlockSpec((TM, D), lambda i: (i, 0)),
                  pl.BlockSpec((1, D), lambda i: (0, 0)),
                  pl.BlockSpec((1, 1, D), lambda i: (_group_of_tile(i), 0, 0)),
                  pl.BlockSpec((1, 1, D), lambda i: (_group_of_tile(i), 0, 0)),
                  pl.BlockSpec((N_EXP, D), lambda i: (0, 0))],
        out_specs=[pl.BlockSpec((TM, D), lambda i: (i, 0)),
                   pl.BlockSpec((N_EXP, TM), lambda i: (0, i))],
        out_shape=[jax.ShapeDtypeStruct((N_TOK, D), BF16),
                   jax.ShapeDtypeStruct((N_EXP, N_TOK), F32)],
        compiler_params=_cparams("parallel"),
        name="moe_router",
    )(x, g.reshape(1, D), sc, sh, w_router_t)


def _lane_prefix_excl(m):
    s, t = m.shape
    r = lax.broadcasted_iota(jnp.int32, (128, 128), 0)
    c = lax.broadcasted_iota(jnp.int32, (128, 128), 1)
    upper = (r <= c).astype(BF16)
    run = jnp.zeros((s, 1), F32)
    out = []
    for k in range(t // 128):
        blk = m[:, 128 * k:128 * (k + 1)]
        inc = jnp.dot(blk.astype(BF16), upper, preferred_element_type=F32) + run
        out.append(inc - blk)
        run = inc[:, 127:128]
    return jnp.concatenate(out, axis=1)


def _select_kernel(a_ref, off_ref, slot_ref, base_ref, *, cap):
    bits = pltpu.bitcast(a_ref[...], jnp.int32)
    s = bits.shape[0]
    capf = float(cap)

    def body(_, lohi):
        lo, hi = lohi
        mid = lo + ((hi - lo + 1) >> 1)
        cnt = jnp.sum((bits >= mid).astype(F32), axis=1, keepdims=True)
        ok = cnt >= capf
        return jnp.where(ok, mid, lo), jnp.where(ok, hi, mid - 1)

    lo0 = jnp.zeros((s, 1), jnp.int32)
    hi0 = jnp.full((s, 1), 0x7F800000, jnp.int32)
    thr, _ = lax.fori_loop(0, 31, body, (lo0, hi0))
    gt = (bits > thr).astype(F32)
    eq = (bits == thr).astype(F32)
    need = capf - jnp.sum(gt, axis=1, keepdims=True)
    sel = gt + eq * (_lane_prefix_excl(eq) < need).astype(F32)
    slot = _lane_prefix_excl(sel) + off_ref[...]
    slot_ref[...] = jnp.where(sel > 0.0, slot, -1.0).astype(jnp.int32)
    t = bits.shape[1]
    tok = lax.broadcasted_iota(jnp.int32, (t, 128), 0)
    tile = lax.broadcasted_iota(jnp.int32, (t, 128), 1)
    ahead = (tok < tile * TOK_TILE).astype(BF16)
    base_ref[...] = jnp.dot(sel.astype(BF16), ahead, preferred_element_type=F32).astype(jnp.int32)


def moe_select(aff, off, cap):
    s, t = aff.shape
    return pl.pallas_call(
        functools.partial(_select_kernel, cap=cap),
        grid=(1,),
        in_specs=[pl.BlockSpec((s, t), lambda i: (0, 0)),
                  pl.BlockSpec((s, 1), lambda i: (0, 0))],
        out_specs=[pl.BlockSpec((s, t), lambda i: (0, 0)),
                   pl.BlockSpec((s, 128), lambda i: (0, 0))],
        out_shape=[jax.ShapeDtypeStruct((s, t), jnp.int32),
                   jax.ShapeDtypeStruct((s, 128), jnp.int32)],
        compiler_params=_cparams("arbitrary"),
        name="moe_select",
    )(aff, off)


TOK_TILE = 256
TILES = GROUP_ROWS // TOK_TILE
WIN = 128
WIN_ALIGN = 16
CNT_STRIDE = TILES + 1


def _tile_windows(cnt_ref, r, e, j, slot_t, fn):
    base = (r * N_EXP + e) * CNT_STRIDE
    c0, c1 = cnt_ref[base + j], cnt_ref[base + j + 1]
    a0 = (c0 // WIN_ALIGN) * WIN_ALIGN
    nwin = jnp.where(c1 > c0, (c1 - a0 + WIN - 1) // WIN, 0)
    row = lax.broadcasted_iota(jnp.int32, (WIN, TOK_TILE), 0)

    def body(w, carry):
        lo = a0 + w * WIN
        start = pl.multiple_of(jnp.minimum(lo, SLOTS_PER_GROUP - WIN), WIN_ALIGN)
        fn(start, (slot_t - start == row) & (slot_t >= lo))
        return carry

    lax.fori_loop(0, nwin, body, 0)


def _gather_kernel(cnt_ref, slot_ref, aff_ref, h_ref, xs_ref, gs_ref, xs_acc, gs_acc):
    xs_acc[...] = jnp.zeros_like(xs_acc)
    gs_acc[...] = jnp.zeros_like(gs_acc)
    for j in range(TILES):
        cols = slice(TOK_TILE * j, TOK_TILE * (j + 1))
        aff_t = aff_ref[0, :, cols]
        h_t = h_ref[cols, :]

        def add(start, oh):
            xs_acc[pl.ds(start, WIN), :] += jnp.dot(oh.astype(BF16), h_t, preferred_element_type=F32)
            gs_acc[pl.ds(start, WIN), :] += jnp.sum(jnp.where(oh, aff_t, 0.0), axis=1, keepdims=True)

        _tile_windows(cnt_ref, pl.program_id(0), pl.program_id(1), j, slot_ref[0, 0, :, cols], add)
    xs_ref[0] = xs_acc[...].astype(BF16)
    gs_ref[0] = gs_acc[...]


def moe_gather(cnt, slot, aff3, h2):
    return pl.pallas_call(
        _gather_kernel,
        grid_spec=pltpu.PrefetchScalarGridSpec(
            num_scalar_prefetch=1,
            grid=(N_GROUPS, N_EXP),
            in_specs=[pl.BlockSpec((1, 1, 1, GROUP_ROWS), lambda r, e, c: (r, e, 0, 0)),
                      pl.BlockSpec((1, 1, GROUP_ROWS), lambda r, e, c: (e, 0, r)),
                      pl.BlockSpec((GROUP_ROWS, D), lambda r, e, c: (r, 0))],
            out_specs=[pl.BlockSpec((1, SLOTS_PER_GROUP, D), lambda r, e, c: (e, r, 0)),
                       pl.BlockSpec((1, SLOTS_PER_GROUP, 1), lambda r, e, c: (e, r, 0))],
            scratch_shapes=[pltpu.VMEM((SLOTS_PER_GROUP, D), F32),
                            pltpu.VMEM((SLOTS_PER_GROUP, 1), F32)]),
        out_shape=[jax.ShapeDtypeStruct((N_EXP, SLOTS, D), BF16),
                   jax.ShapeDtypeStruct((N_EXP, SLOTS, 1), F32)],
        compiler_params=_cparams("parallel", "parallel"),
        name="moe_gather",
    )(cnt, slot, aff3, h2)


FF_TILE = 512


def _ffn_kernel(xs_ref, gs_ref, wg_ref, wu_ref, wd_ref, ys_ref, acc):
    f = pl.program_id(1)
    x = xs_ref[0]
    g = jnp.dot(x, wg_ref[0, 0].astype(BF16), preferred_element_type=F32)
    u = jnp.dot(x, wu_ref[0, 0].astype(BF16), preferred_element_type=F32)
    hid = (_silu(g) * u).astype(BF16)
    contrib = jnp.dot(hid, wd_ref[0, 0].astype(BF16), preferred_element_type=F32)

    @pl.when(f == 0)
    def _():
        acc[...] = contrib

    @pl.when(f > 0)
    def _():
        acc[...] += contrib

    @pl.when(f == FF // FF_TILE - 1)
    def _():
        ys_ref[0] = (acc[...] * gs_ref[0]).astype(BF16)


def moe_ffn(xs, gs, w_gate, w_up, w_down, layer):
    return pl.pallas_call(
        _ffn_kernel,
        grid=(N_EXP, FF // FF_TILE),
        in_specs=[pl.BlockSpec((1, SLOTS, D), lambda e, f: (e, 0, 0)),
                  pl.BlockSpec((1, SLOTS, 1), lambda e, f: (e, 0, 0)),
                  pl.BlockSpec((1, 1, D, FF_TILE), lambda e, f: (layer, e, 0, f)),
                  pl.BlockSpec((1, 1, D, FF_TILE), lambda e, f: (layer, e, 0, f)),
                  pl.BlockSpec((1, 1, FF_TILE, D), lambda e, f: (layer, e, f, 0))],
        out_specs=pl.BlockSpec((1, SLOTS, D), lambda e, f: (e, 0, 0)),
        out_shape=jax.ShapeDtypeStruct((N_EXP, SLOTS, D), BF16),
        scratch_shapes=[pltpu.VMEM((SLOTS, D), F32)],
        compiler_params=_cparams("parallel", "arbitrary"),
        name="moe_ffn",
    )(xs, gs, w_gate, w_up, w_down)


SCAT_TN = 512


def _scatter_kernel(cnt_ref, slot_ref, ys_ref, x_ref, g2_ref, o_ref):
    r, e = pl.program_id(0), pl.program_id(2)

    @pl.when(e == 0)
    def _():
        o_ref[...] = jnp.zeros_like(o_ref)

    for j in range(TILES):
        rows = slice(TOK_TILE * j, TOK_TILE * (j + 1))

        def add(start, oh):
            o_ref[rows, :] += lax.dot_general(oh.astype(BF16), ys_ref[0, pl.ds(start, WIN), :],
                                              (((0,), (0,)), ((), ())), preferred_element_type=F32)

        _tile_windows(cnt_ref, r, e, j, slot_ref[0, 0, :, rows], add)

    @pl.when(e == N_EXP - 1)
    def _():
        o_ref[...] = x_ref[...] + g2_ref[0] * o_ref[...]


def moe_scatter(cnt, slot, ys, x, g2):
    return pl.pallas_call(
        _scatter_kernel,
        grid_spec=pltpu.PrefetchScalarGridSpec(
            num_scalar_prefetch=1,
            grid=(N_GROUPS, D // SCAT_TN, N_EXP),
            in_specs=[pl.BlockSpec((1, 1, 1, GROUP_ROWS), lambda r, c, e, n: (r, e, 0, 0)),
                      pl.BlockSpec((1, SLOTS_PER_GROUP, SCAT_TN), lambda r, c, e, n: (e, r, c)),
                      pl.BlockSpec((GROUP_ROWS, SCAT_TN), lambda r, c, e, n: (r, c)),
                      pl.BlockSpec((1, 1, SCAT_TN), lambda r, c, e, n: (r, 0, c))],
            out_specs=pl.BlockSpec((GROUP_ROWS, SCAT_TN), lambda r, c, e, n: (r, c))),
        out_shape=jax.ShapeDtypeStruct((N_TOK, D), F32),
        compiler_params=_cparams("parallel", "parallel", "arbitrary"),
        name="moe_scatter",
    )(cnt, slot, ys, x, g2)


def moe_layer(x, mods, g_ffn, layer, w_router, w_gate, w_up, w_down):
    sh2, sc2, g2 = mods
    h2, aff = moe_route(x, g_ffn, sc2, sh2, w_router.T)
    aff_ctx = aff[:, :GROUP_ROWS].reshape(N_EXP * N_CTX_SEQ, CTX_LEN)
    off_ctx = jnp.tile(jnp.arange(N_CTX_SEQ, dtype=F32) * CAP_CTX, N_EXP).reshape(-1, 1)
    slot_ctx, _ = moe_select(aff_ctx, off_ctx, CAP_CTX)
    aff_lat = jnp.transpose(aff[:, GROUP_ROWS:].reshape(N_EXP, N_LAT_SEQ, LAT_LEN), (1, 0, 2)
                            ).reshape(N_LAT_SEQ * N_EXP, LAT_LEN)
    slot_lat, base_lat = moe_select(aff_lat, jnp.zeros((N_LAT_SEQ * N_EXP, 1), F32), CAP_LAT)
    slot = jnp.concatenate([slot_ctx.reshape(1, N_EXP, GROUP_ROWS),
                            slot_lat.reshape(N_LAT_SEQ, N_EXP, LAT_LEN)], axis=0
                           ).reshape(N_GROUPS, N_EXP, 1, GROUP_ROWS)
    cnt_ctx = jnp.broadcast_to(jnp.arange(CNT_STRIDE, dtype=jnp.int32) * CAP_CTX, (1, N_EXP, CNT_STRIDE))
    cnt = jnp.concatenate([cnt_ctx, base_lat[:, :CNT_STRIDE].reshape(N_LAT_SEQ, N_EXP, CNT_STRIDE)],
                          axis=0).reshape(-1)
    xs, gs = moe_gather(cnt, slot, aff.reshape(N_EXP, 1, N_TOK), h2)
    ys = moe_ffn(xs, gs, w_gate, w_up, w_down, layer)
    return moe_scatter(cnt, slot, ys, x, g2)


def _final_norm_kernel(x_ref, g_ref, ctx_ref, lat_ref):
    x = x_ref[...]
    y = x * lax.rsqrt(jnp.mean(x * x, axis=-1, keepdims=True) + NORM_EPS) * g_ref[...]
    is_ctx = pl.program_id(0) < TILES_PER_GROUP

    @pl.when(is_ctx)
    def _():
        ctx_ref[...] = y

    @pl.when(jnp.logical_not(is_ctx))
    def _():
        lat_ref[...] = y


def final_norm(x, g):
    t = TILES_PER_GROUP
    return pl.pallas_call(
        _final_norm_kernel,
        grid=(N_TOK // TM,),
        in_specs=[pl.BlockSpec((TM, D), lambda i: (i, 0)),
                  pl.BlockSpec((1, D), lambda i: (0, 0))],
        out_specs=[pl.BlockSpec((TM, D), lambda i: (jnp.minimum(i, t - 1), 0)),
                   pl.BlockSpec((TM, D), lambda i: (jnp.maximum(i - t, 0), 0))],
        out_shape=[jax.ShapeDtypeStruct((GROUP_ROWS, D), F32),
                   jax.ShapeDtypeStruct((N_TOK - GROUP_ROWS, D), F32)],
        compiler_params=_cparams("arbitrary"),
        name="final_norm",
    )(x, g.reshape(1, D))


def _grid_pos_embed():
    rows = LAT_LEN // GRID_W
    quarter = D // 4
    omega = 1.0 / (10000.0 ** (jnp.arange(quarter, dtype=F32) / quarter))
    r = jnp.arange(rows, dtype=F32)[:, None] * omega
    cl = jnp.arange(GRID_W, dtype=F32)[:, None] * omega
    emb_r = jnp.concatenate([jnp.sin(r), jnp.cos(r)], axis=-1)
    emb_c = jnp.concatenate([jnp.sin(cl), jnp.cos(cl)], axis=-1)
    emb = jnp.concatenate([jnp.broadcast_to(emb_r[:, None], (rows, GRID_W, D // 2)),
                           jnp.broadcast_to(emb_c[None], (rows, GRID_W, D // 2))], axis=-1)
    return emb.reshape(LAT_LEN, D)


def kernel(x_prompt, x_sample, state_s5_re, state_s5_im, state_hgrn, state_ssd, c, c_ctx, w_ada, b_ada, norm_mix, norm_ffn, norm_final, s5_lam_re, s5_lam_im, s5_log_dt, s5_b_re, s5_b_im, s5_c_re, s5_c_im, s5_d, s5_w_glu, s5_b_glu, hg_w_qig, hg_w_f, hg_b_f, hg_lb_logits, hg_norm, hg_w_o, ssd_w_in, ssd_conv_w, ssd_conv_b, ssd_dt_bias, ssd_a_log, ssd_d, ssd_norm, ssd_w_out, moe_router, moe_w_gate, moe_w_up, moe_w_down):
    cond8 = jnp.concatenate([c_ctx[None], c, jnp.zeros((5, D), F32)], axis=0)
    mod = ada_mod(cond8, w_ada, b_ada)
    mods = jnp.transpose(mod.reshape(DEPTH, 8, 6, D)[:, :3], (0, 2, 1, 3)).reshape(DEPTH, 6, 3, 1, D)
    x = embed_tokens(x_prompt.reshape(-1, D), x_sample.reshape(-1, D), _grid_pos_embed())
    s5_re, s5_im, hg_fin, ssd_fin = [], [], [], []
    for i in range(DEPTH):
        mix_mods = (mods[i, 0], mods[i, 1], mods[i, 2])
        kind, j = i % 3, i // 3
        if kind == 0:
            x, fr, fi = s5_layer(x, mix_mods, norm_mix[i], state_s5_re[:, j], state_s5_im[:, j],
                                 s5_lam_re[j], s5_lam_im[j], s5_log_dt[j], s5_b_re[j], s5_b_im[j],
                                 s5_c_re[j], s5_c_im[j], s5_d[j], s5_w_glu[j], s5_b_glu[j])
            s5_re.append(fr)
            s5_im.append(fi)
        elif kind == 1:
            x, fh = hgrn_layer(x, mix_mods, norm_mix[i], state_hgrn[:, j], i, hg_lb_logits,
                               hg_w_qig[j], hg_w_f[j], hg_b_f[j], hg_norm[j], hg_w_o[j])
            hg_fin.append(fh)
        else:
            x, fs = ssd_layer(x, mix_mods, norm_mix[i], state_ssd[:, j], ssd_w_in[j], ssd_conv_w[j],
                              ssd_conv_b[j], ssd_dt_bias[j], ssd_a_log[j], ssd_d[j], ssd_norm[j],
                              ssd_w_out[j])
            ssd_fin.append(fs)
        x = moe_layer(x, (mods[i, 3], mods[i, 4], mods[i, 5]), norm_ffn[i], i, moe_router[i],
                      moe_w_gate, moe_w_up, moe_w_down)
    y_ctx, y_lat = final_norm(x, norm_final)
    return (y_ctx.reshape(N_CTX_SEQ, CTX_LEN, D), y_lat.reshape(N_LAT_SEQ, LAT_LEN, D),
            jnp.stack(s5_re, axis=1), jnp.stack(s5_im, axis=1),
            jnp.stack(hg_fin, axis=1), jnp.stack(ssd_fin, axis=1))
```

```python
import functools
import math

import jax
import jax.numpy as jnp
from jax import lax
from jax.experimental import pallas as pl
from jax.experimental.pallas import tpu as pltpu

F32 = jnp.float32
BF16 = jnp.bfloat16
HIGHEST = lax.Precision.HIGHEST

D = 1024
DEPTH = 4
N_CTX_SEQ = 16
CTX_LEN = 256
N_LAT_SEQ = 2
LAT_LEN = 4096
GROUP_ROWS = 4096
N_GROUPS = 3
N_TOK = N_GROUPS * GROUP_ROWS
N_SEQ = N_CTX_SEQ + N_LAT_SEQ
NORM_EPS = 1e-6
GRID_W = 64

VMEM_LIMIT_BYTES = 56 * 1024 * 1024


def _cparams(*sem):
    return pltpu.CompilerParams(dimension_semantics=sem, vmem_limit_bytes=VMEM_LIMIT_BYTES)


def _silu(x):
    return x * jax.nn.sigmoid(x)


def _normmod(x, g, sc, sh):
    ms = jnp.mean(x * x, axis=-1, keepdims=True)
    return x * lax.rsqrt(ms + NORM_EPS) * g * (1.0 + sc) + sh


def _cmul(ar, ai, br, bi):
    return ar * br - ai * bi, ar * bi + ai * br


def _mod_kernel(c_ref, w_ref, b_ref, o_ref):
    o_ref[0] = jnp.dot(_silu(c_ref[...]), w_ref[0], precision=HIGHEST,
                       preferred_element_type=F32) + b_ref[0]


def ada_mod(cond8, w_ada, b_ada):
    tn = 1536
    return pl.pallas_call(
        _mod_kernel,
        grid=(DEPTH, 6 * D // tn),
        in_specs=[pl.BlockSpec((8, D), lambda i, j: (0, 0)),
                  pl.BlockSpec((1, D, tn), lambda i, j: (i, 0, j)),
                  pl.BlockSpec((1, 1, tn), lambda i, j: (i, 0, j))],
        out_specs=pl.BlockSpec((1, 8, tn), lambda i, j: (i, 0, j)),
        out_shape=jax.ShapeDtypeStruct((DEPTH, 8, 6 * D), F32),
        compiler_params=_cparams("parallel", "parallel"),
        name="ada_mod",
    )(cond8, w_ada, b_ada.reshape(DEPTH, 1, 6 * D))


def _embed_kernel(xp_ref, xs_ref, pos_ref, o_ref):
    r = pl.program_id(0)

    @pl.when(r == 0)
    def _():
        o_ref[...] = xp_ref[...]

    @pl.when(r > 0)
    def _():
        o_ref[...] = xs_ref[...] + pos_ref[...]


def embed_tokens(xp, xs, pos):
    tm = 1024
    nt = GROUP_ROWS // tm
    return pl.pallas_call(
        _embed_kernel,
        grid=(N_GROUPS, nt),
        in_specs=[pl.BlockSpec((tm, D), lambda r, i: (jnp.where(r == 0, i, 0), 0)),
                  pl.BlockSpec((tm, D), lambda r, i: (jnp.where(r == 0, 0, (r - 1) * nt + i), 0)),
                  pl.BlockSpec((tm, D), lambda r, i: (i, 0))],
        out_specs=pl.BlockSpec((tm, D), lambda r, i: (r * nt + i, 0)),
        out_shape=jax.ShapeDtypeStruct((N_TOK, D), F32),
        compiler_params=_cparams("parallel", "parallel"),
        name="embed_tokens",
    )(xp, xs, pos)


TM = 1024
TILES_PER_GROUP = GROUP_ROWS // TM


def _group_of_tile(i):
    return i // TILES_PER_GROUP


def _normmod_kernel(x_ref, g_ref, sc_ref, sh_ref, o_ref):
    o_ref[...] = _normmod(x_ref[...], g_ref[...], sc_ref[0], sh_ref[0])


def normmod(x, g, sc, sh):
    return pl.pallas_call(
        _normmod_kernel,
        grid=(N_TOK // TM,),
        in_specs=[pl.BlockSpec((TM, D), lambda i: (i, 0)),
                  pl.BlockSpec((1, D), lambda i: (0, 0)),
                  pl.BlockSpec((1, 1, D), lambda i: (_group_of_tile(i), 0, 0)),
                  pl.BlockSpec((1, 1, D), lambda i: (_group_of_tile(i), 0, 0))],
        out_specs=pl.BlockSpec((TM, D), lambda i: (i, 0)),
        out_shape=jax.ShapeDtypeStruct((N_TOK, D), F32),
        compiler_params=_cparams("parallel"),
        name="normmod",
    )(x, g.reshape(1, D), sc, sh)


def _nm_matmul_kernel(x_ref, g_ref, sc_ref, sh_ref, w_ref, b_ref, o_ref, h_scr):
    @pl.when(pl.program_id(1) == 0)
    def _():
        h_scr[...] = _normmod(x_ref[...], g_ref[...], sc_ref[0], sh_ref[0]).astype(BF16)

    o_ref[...] = jnp.dot(h_scr[...], w_ref[...].astype(BF16),
                         preferred_element_type=F32) + b_ref[...]


def nm_matmul(x, g, sc, sh, w, b, n_out, tn=512, name="nm_matmul"):
    return pl.pallas_call(
        _nm_matmul_kernel,
        grid=(N_TOK // TM, n_out // tn),
        in_specs=[pl.BlockSpec((TM, D), lambda i, j: (i, 0)),
                  pl.BlockSpec((1, D), lambda i, j: (0, 0)),
                  pl.BlockSpec((1, 1, D), lambda i, j: (_group_of_tile(i), 0, 0)),
                  pl.BlockSpec((1, 1, D), lambda i, j: (_group_of_tile(i), 0, 0)),
                  pl.BlockSpec((D, tn), lambda i, j: (0, j)),
                  pl.BlockSpec((1, tn), lambda i, j: (0, j))],
        out_specs=pl.BlockSpec((TM, tn), lambda i, j: (i, j)),
        out_shape=jax.ShapeDtypeStruct((N_TOK, n_out), F32),
        scratch_shapes=[pltpu.VMEM((TM, D), BF16)],
        compiler_params=_cparams("parallel", "arbitrary"),
        name=name,
    )(x, g.reshape(1, D), sc, sh, w, b.reshape(1, -1))


S5_G = 64
S5_H = 16
S5_P = 64
S5_L = 16
S5_GB = 8
S5_NB = S5_G // S5_GB
S5_CH = GROUP_ROWS // S5_L
S5_CTX_CH = CTX_LEN // S5_L
S5_NPOW = 8
S5_XW = S5_L * 128
S5_SW = 2 * S5_GB * S5_P


S5_LH = S5_L * S5_H
S5_NPWR = 24


def _s5_prep_kernel(lr_ref, li_ref, ldt_ref, btr_ref, bti_ref, cxr_ref, cxi_ref,
                    sel0_ref, sel1_ref, exp_ref,
                    kk_ref, wst_ref, wout_ref, apr_ref, api_ref):
    dt = jnp.exp(ldt_ref[0])
    lam_r, lam_i = lr_ref[0], li_ref[0]
    ar, ai = lam_r * dt, lam_i * dt
    pw = lax.broadcasted_iota(jnp.int32, (S5_GB, S5_NPWR, 2 * S5_P), 1).astype(F32)
    ep = jnp.exp(pw * ar)
    pwr, pwi = ep * jnp.cos(pw * ai), ep * jnp.sin(pw * ai)
    den = lam_r * lam_r + lam_i * lam_i
    nr, ni = pwr[:, 1:2] - 1.0, pwi[:, 1:2]
    beta_r = (nr * lam_r + ni * lam_i) / den
    beta_i = (ni * lam_r - nr * lam_i) / den
    bbr, bbi = _cmul(beta_r, beta_i, btr_ref[0], bti_ref[0])

    lane = lax.broadcasted_iota(jnp.int32, (S5_H, 2 * S5_P), 1)
    wst_ref[0, 0] = jnp.zeros((S5_XW, S5_SW), BF16)
    half = S5_SW // 2
    for g in range(S5_GB):
        mine = (lane >= S5_P) if g % 2 else (lane < S5_P)
        col = 128 * (g // 2)
        for s in range(S5_L):
            k = S5_L - 1 - s
            wr, wi = _cmul(pwr[g, k:k + 1], pwi[g, k:k + 1], bbr[g], bbi[g])
            rows = slice(128 * s + S5_H * g, 128 * s + S5_H * (g + 1))
            wst_ref[0, 0, rows, col:col + 128] = jnp.where(mine, wr, 0.0).astype(BF16)
            wst_ref[0, 0, rows, half + col:half + col + 128] = jnp.where(mine, wi, 0.0).astype(BF16)

    lane1 = lax.broadcasted_iota(jnp.int32, (1, 2 * S5_P), 1)

    def group_lanes(a):
        return jnp.concatenate([jnp.where(lane1 < S5_P, a[2 * q], a[2 * q + 1])
                                for q in range(S5_GB // 2)], axis=1)

    pr, pi_ = pwr[:, S5_L:S5_L + 1], pwi[:, S5_L:S5_L + 1]
    for k in range(S5_NPOW):
        apr_ref[0, 0, k:k + 1, :] = group_lanes(pr)
        api_ref[0, 0, k:k + 1, :] = group_lanes(pi_)
        pr, pi_ = _cmul(pr, pi_, pr, pi_)

    tn = (((0,), (0,)), ((), ()))
    kks, wre, wim = [], [], []
    for g in range(S5_GB):
        pg_r, pg_i = pwr[g, :, :S5_P], pwi[g, :, :S5_P]
        cr, ci = cxr_ref[0, g], cxi_ref[0, g]

        def c_times_pow(sel):
            er = lax.dot_general(pg_r, sel, tn, precision=HIGHEST, preferred_element_type=F32)
            ei = lax.dot_general(pg_i, sel, tn, precision=HIGHEST, preferred_element_type=F32)
            return _cmul(cr, ci, er, ei)

        k_r, k_i = c_times_pow(sel0_ref[...])
        kks.append(jnp.dot(bbr[g, :, :S5_P], k_r, precision=HIGHEST, preferred_element_type=F32)
                   - jnp.dot(bbi[g, :, :S5_P], k_i, precision=HIGHEST, preferred_element_type=F32))
        o_r, o_i = c_times_pow(sel1_ref[...])
        wre.append(o_r)
        wim.append(-o_i)
    glane = (lax.broadcasted_iota(jnp.int32, (1, S5_XW), 1) % 128) // S5_H

    def spread(parts, rows_per_group):
        a = jnp.concatenate(parts, axis=0).astype(BF16)
        a = jnp.dot(a, exp_ref[...], preferred_element_type=F32)
        grow = lax.broadcasted_iota(jnp.int32, (a.shape[0], 1), 0) // rows_per_group % S5_GB
        return jnp.where(grow == glane, a, 0.0).astype(BF16)

    wout_ref[0, 0] = spread(wre + wim, S5_P)
    kk_ref[0, 0] = spread(kks, S5_H)


def s5_prepare(lam_re, lam_im, log_dt, b_re, b_im, c_re, c_im):
    half = S5_SW // 2
    lr = jnp.tile(lam_re.reshape(2, S5_G, 1, S5_P), (1, 1, 1, 2))
    li = jnp.tile(lam_im.reshape(2, S5_G, 1, S5_P), (1, 1, 1, 2))
    ldt = log_dt.reshape(2, S5_G, 1, 1)
    btr = jnp.tile(jnp.swapaxes(b_re, 2, 3), (1, 1, 1, 2))
    bti = jnp.tile(jnp.swapaxes(b_im, 2, 3), (1, 1, 1, 2))
    cxr = jnp.tile(jnp.swapaxes(c_re, 2, 3), (1, 1, 1, S5_L))
    cxi = jnp.tile(jnp.swapaxes(c_im, 2, 3), (1, 1, 1, S5_L))
    k = jnp.arange(S5_NPWR)[:, None]
    t = (jnp.arange(S5_LH) // S5_H)[None, :]
    sel0 = (k == t).astype(F32)
    sel1 = (k == t + 1).astype(F32)
    src = jnp.arange(S5_LH)[:, None]
    dst = jnp.arange(S5_XW)[None, :]
    expand = ((src // S5_H == dst // 128) & (src % S5_H == dst % S5_H)).astype(BF16)

    def spec(*tail):
        return pl.BlockSpec((1, S5_GB) + tail, lambda d, j: (d, j) + (0,) * len(tail))

    def const(shape):
        return pl.BlockSpec(shape, lambda d, j: (0,) * len(shape))

    def blk(*tail):
        return pl.BlockSpec((1, 1) + tail, lambda d, j: (d, j) + (0,) * len(tail))

    return pl.pallas_call(
        _s5_prep_kernel,
        grid=(2, S5_NB),
        in_specs=[spec(1, 2 * S5_P), spec(1, 2 * S5_P), spec(1, 1),
                  spec(S5_H, 2 * S5_P), spec(S5_H, 2 * S5_P), spec(S5_P, S5_LH), spec(S5_P, S5_LH),
                  const((S5_NPWR, S5_LH)), const((S5_NPWR, S5_LH)), const((S5_LH, S5_XW))],
        out_specs=[blk(128, S5_XW), blk(S5_XW, S5_SW), blk(S5_SW, S5_XW),
                   blk(S5_NPOW, half), blk(S5_NPOW, half)],
        out_shape=[jax.ShapeDtypeStruct((2, S5_NB, 128, S5_XW), BF16),
                   jax.ShapeDtypeStruct((2, S5_NB, S5_XW, S5_SW), BF16),
                   jax.ShapeDtypeStruct((2, S5_NB, S5_SW, S5_XW), BF16),
                   jax.ShapeDtypeStruct((2, S5_NB, S5_NPOW, half), F32),
                   jax.ShapeDtypeStruct((2, S5_NB, S5_NPOW, half), F32)],
        compiler_params=_cparams("parallel", "parallel"),
        name="s5_prepare",
    )(lr, li, ldt, btr, bti, cxr, cxi, sel0, sel1, expand)


def _s5_scan_body(d, r, h_ref, kk_ref, wst_ref, wout_ref, apr_ref, api_ref, h0r_ref, h0i_ref,
                  y_ref, fr_ref, fi_ref, m8, zr_s, zi_s):
    half = S5_SW // 2

    @pl.when(r == 0)
    def _build():
        for s in range(S5_L):
            if s:
                m8[128 * s:128 * (s + 1), 0:128 * s] = jnp.zeros((128, 128 * s), BF16)
            m8[128 * s:128 * (s + 1), 128 * s:] = kk_ref[0, 0, :, :S5_XW - 128 * s]

    def tloc(s):
        return s if d == 0 else S5_L - 1 - s

    slabs = [h_ref[pl.ds(tloc(s), S5_CH, stride=S5_L), :] for s in range(S5_L)]
    x8 = jnp.concatenate(slabs, axis=1).astype(BF16)
    delta = jnp.dot(x8, wst_ref[0, 0], preferred_element_type=F32)
    zr, zi = delta[:, :half], delta[:, half:]

    row = lax.broadcasted_iota(jnp.int32, (S5_CH, 1), 0)
    is_ctx = r == 0
    pos = jnp.where(is_ctx, row & (S5_CTX_CH - 1), row)
    last = jnp.where(is_ctx, S5_CTX_CH - 1, S5_CH - 1)
    a = jnp.maximum(r - 1, 0)
    lat = (r > 0).astype(F32)
    h0r = h0r_ref[0, 0, pl.ds(a, 1), :] * lat
    h0i = h0i_ref[0, 0, pl.ds(a, 1), :] * lat
    first = (pos == 0) if d == 0 else (pos == last)
    ir, ii = _cmul(apr_ref[0, 0, 0:1, :], api_ref[0, 0, 0:1, :], h0r, h0i)
    zr = zr + jnp.where(first, ir, 0.0)
    zi = zi + jnp.where(first, ii, 0.0)
    for k in range(S5_NPOW):
        m = 1 << k
        akr, aki = apr_ref[0, 0, k:k + 1, :], api_ref[0, 0, k:k + 1, :]
        if d == 0:
            sr, si = pltpu.roll(zr, m, 0), pltpu.roll(zi, m, 0)
            valid = pos >= m
        else:
            sr, si = pltpu.roll(zr, S5_CH - m, 0), pltpu.roll(zi, S5_CH - m, 0)
            valid = pos <= last - m
        pr, pi_ = _cmul(akr, aki, sr, si)
        zr = zr + jnp.where(valid, pr, 0.0)
        zi = zi + jnp.where(valid, pi_, 0.0)
    if d == 0:
        sr, si = pltpu.roll(zr, 1, 0), pltpu.roll(zi, 1, 0)
    else:
        sr, si = pltpu.roll(zr, S5_CH - 1, 0), pltpu.roll(zi, S5_CH - 1, 0)
    sr = jnp.where(first, h0r, sr)
    si = jnp.where(first, h0i, si)
    s_in = jnp.concatenate([sr, si], axis=1).astype(BF16)
    y8 = (jnp.dot(x8, m8[...], preferred_element_type=F32)
          + jnp.dot(s_in, wout_ref[0, 0], preferred_element_type=F32))
    for t in range(S5_L):
        y_ref[0, pl.ds(tloc(t), S5_CH, stride=S5_L), :] = y8[:, 128 * t:128 * (t + 1)]

    @pl.when(r == 0)
    def _fin():
        off = S5_CTX_CH - 1 if d == 0 else 0
        for q in range(half // 128):
            zr_s[q] = zr[:, 128 * q:128 * (q + 1)]
            zi_s[q] = zi[:, 128 * q:128 * (q + 1)]
            fr_ref[0, 0, :, 128 * q:128 * (q + 1)] = zr_s[q, pl.ds(off, N_CTX_SEQ, stride=S5_CTX_CH), :]
            fi_ref[0, 0, :, 128 * q:128 * (q + 1)] = zi_s[q, pl.ds(off, N_CTX_SEQ, stride=S5_CTX_CH), :]


def _s5_scan_kernel(*refs):
    d = pl.program_id(1)
    r = pl.program_id(2)

    @pl.when(d == 0)
    def _():
        _s5_scan_body(0, r, *refs)

    @pl.when(d == 1)
    def _():
        _s5_scan_body(1, r, *refs)


def s5_scan(h, kk8, wst8, wout8, apr, api, h0r, h0i):
    half = S5_SW // 2
    return pl.pallas_call(
        _s5_scan_kernel,
        grid=(S5_NB, 2, N_GROUPS),
        in_specs=[pl.BlockSpec((GROUP_ROWS, 128), lambda j, d, r: (r, j)),
                  pl.BlockSpec((1, 1, 128, S5_XW), lambda j, d, r: (d, j, 0, 0)),
                  pl.BlockSpec((1, 1, S5_XW, S5_SW), lambda j, d, r: (d, j, 0, 0)),
                  pl.BlockSpec((1, 1, S5_SW, S5_XW), lambda j, d, r: (d, j, 0, 0)),
                  pl.BlockSpec((1, 1, S5_NPOW, half), lambda j, d, r: (d, j, 0, 0)),
                  pl.BlockSpec((1, 1, S5_NPOW, half), lambda j, d, r: (d, j, 0, 0)),
                  pl.BlockSpec((1, 1, N_LAT_SEQ, half), lambda j, d, r: (d, j, 0, 0)),
                  pl.BlockSpec((1, 1, N_LAT_SEQ, half), lambda j, d, r: (d, j, 0, 0))],
        out_specs=[pl.BlockSpec((1, GROUP_ROWS, 128), lambda j, d, r: (d, r, j)),
                   pl.BlockSpec((1, 1, N_CTX_SEQ, half), lambda j, d, r: (d, j, 0, 0)),
                   pl.BlockSpec((1, 1, N_CTX_SEQ, half), lambda j, d, r: (d, j, 0, 0))],
        out_shape=[jax.ShapeDtypeStruct((2, N_TOK, D), F32),
                   jax.ShapeDtypeStruct((2, S5_NB, N_CTX_SEQ, half), F32),
                   jax.ShapeDtypeStruct((2, S5_NB, N_CTX_SEQ, half), F32)],
        scratch_shapes=[pltpu.VMEM((S5_XW, S5_XW), BF16),
                        pltpu.VMEM((half // 128, S5_CH, 128), F32),
                        pltpu.VMEM((half // 128, S5_CH, 128), F32)],
        compiler_params=_cparams("arbitrary", "arbitrary", "arbitrary"),
        name="s5_scan",
    )(h, kk8, wst8, wout8, apr, api, h0r, h0i)


def _s5_glu_kernel(h_ref, y0_ref, y1_ref, dsk_ref, wa_ref, wb_ref, ba_ref, bb_ref, x_ref, g1_ref,
                   o_ref, yg_scr):
    @pl.when(pl.program_id(1) == 0)
    def _():
        y = dsk_ref[...] * h_ref[...] + y0_ref[0] + y1_ref[0]
        yg_scr[...] = jax.nn.gelu(y).astype(BF16)

    yg = yg_scr[...]
    a = jnp.dot(yg, wa_ref[...].astype(BF16), preferred_element_type=F32) + ba_ref[...]
    b = jnp.dot(yg, wb_ref[...].astype(BF16), preferred_element_type=F32) + bb_ref[...]
    o_ref[...] = x_ref[...] + g1_ref[0] * (a * jax.nn.sigmoid(b))


def s5_glu(h, y, d_skip, w_glu, b_glu, x, g1, tn=512):
    nj = D // tn
    b2 = b_glu.reshape(1, 2 * D)
    return pl.pallas_call(
        _s5_glu_kernel,
        grid=(N_TOK // TM, nj),
        in_specs=[pl.BlockSpec((TM, D), lambda i, j: (i, 0)),
                  pl.BlockSpec((1, TM, D), lambda i, j: (0, i, 0)),
                  pl.BlockSpec((1, TM, D), lambda i, j: (1, i, 0)),
                  pl.BlockSpec((1, D), lambda i, j: (0, 0)),
                  pl.BlockSpec((D, tn), lambda i, j: (0, j)),
                  pl.BlockSpec((D, tn), lambda i, j: (0, nj + j)),
                  pl.BlockSpec((1, tn), lambda i, j: (0, j)),
                  pl.BlockSpec((1, tn), lambda i, j: (0, nj + j)),
                  pl.BlockSpec((TM, tn), lambda i, j: (i, j)),
                  pl.BlockSpec((1, 1, tn), lambda i, j: (_group_of_tile(i), 0, j))],
        out_specs=pl.BlockSpec((TM, tn), lambda i, j: (i, j)),
        out_shape=jax.ShapeDtypeStruct((N_TOK, D), F32),
        scratch_shapes=[pltpu.VMEM((TM, D), BF16)],
        compiler_params=_cparams("parallel", "arbitrary"),
        name="s5_glu",
    )(h, y, y, d_skip.reshape(1, D), w_glu, w_glu, b2, b2, x, g1)


def s5_layer(x, mods, g_norm, st_re, st_im, lam_re, lam_im, log_dt, b_re, b_im, c_re, c_im,
             d_skip, w_glu, b_glu):
    sh1, sc1, g1 = mods
    h = normmod(x, g_norm, sc1, sh1)
    kk8, wst8, wout8, apr, api = s5_prepare(lam_re, lam_im, log_dt, b_re, b_im, c_re, c_im)

    def h0(st):
        return jnp.transpose(st.reshape(N_LAT_SEQ, 2, S5_NB, S5_GB * S5_P), (1, 2, 0, 3))

    y, fr, fi = s5_scan(h, kk8, wst8, wout8, apr, api, h0(st_re), h0(st_im))

    def fin(f):
        return jnp.transpose(f.reshape(2, S5_NB, N_CTX_SEQ, S5_GB, S5_P), (2, 0, 1, 3, 4)
                             ).reshape(N_CTX_SEQ, 2, S5_G, S5_P)

    x = s5_glu(h, y, d_skip, w_glu, b_glu, x, g1)
    return x, fin(fr), fin(fi)


CHUNK = 64
N_CHUNKS = N_TOK // CHUNK
CTX_CHUNKS = N_CTX_SEQ * CTX_LEN // CHUNK
CH_PER_CTX = CTX_LEN // CHUNK
CH_PER_LAT = LAT_LEN // CHUNK


def _chunk_of_step(d, c):
    return jnp.where(d == 0, c, N_CHUNKS - 1 - c)


def _seq_of_chunk(ce):
    return jnp.where(ce < CTX_CHUNKS, ce // CH_PER_CTX, N_CTX_SEQ + (ce - CTX_CHUNKS) // CH_PER_LAT)


def _chunk_flags(d, ce):
    is_ctx = ce < CTX_CHUNKS
    pos = jnp.where(is_ctx, ce % CH_PER_CTX, (ce - CTX_CHUNKS) % CH_PER_LAT)
    n = jnp.where(is_ctx, CH_PER_CTX, CH_PER_LAT)
    t_first, t_last = pos == 0, pos == n - 1
    if d == 0:
        return is_ctx, t_first, t_last
    return is_ctx, t_last, t_first


def _tri(d, shape, row_axis=0, col_axis=1):
    r = lax.broadcasted_iota(jnp.int32, shape, row_axis)
    c = lax.broadcasted_iota(jnp.int32, shape, col_axis)
    return (r >= c) if d == 0 else (r <= c)


HG_H = 8
HG_K = 128
HG_SAFE_SPAN = 60.0


def _hg_scan_body(d, layer, q_ref, v_ref, z_ref, lbl_ref, s0_ref, o_ref, fin_ref,
                  st_scr, inter_scr, cum_scr, k_scr):
    ce = _chunk_of_step(d, pl.program_id(1))
    is_ctx, starts, ends = _chunk_flags(d, ce)

    @pl.when(starts & is_ctx)
    def _():
        st_scr[...] = jnp.zeros_like(st_scr)

    @pl.when(starts & jnp.logical_not(is_ctx))
    def _():
        st_scr[...] = s0_ref[0, 0]

    lg = lbl_ref[0]
    e = jnp.exp(lg - jnp.max(lg, axis=0, keepdims=True))
    sm = e / jnp.sum(e, axis=0, keepdims=True)
    lb = jnp.sum(sm[1:layer + 1], axis=0, keepdims=True)
    f = lb + (1.0 - lb) * jax.nn.sigmoid(z_ref[...])
    g = jnp.log(f)
    kk = 1.0 - f
    tri = _tri(d, (CHUNK, CHUNK))
    cum = jnp.dot(tri.astype(F32), g, precision=HIGHEST, preferred_element_type=F32)
    tot = cum[CHUNK - 1:CHUNK] if d == 0 else cum[0:1]
    mid = cum[CHUNK // 2 - 1:CHUNK // 2]
    q = q_ref[...]
    v = v_ref[...]
    qa = (q * jnp.exp(cum - mid)).astype(BF16)
    ka = (kk * jnp.exp(mid - cum)).astype(BF16)
    qs = (q * jnp.exp(cum)).astype(BF16)
    kd = (kk * jnp.exp(tot - cum)).astype(BF16)
    vb = v.astype(BF16)
    etot = jnp.exp(tot)
    nt = (((1,), (1,)), ((), ()))
    tn = (((0,), (0,)), ((), ()))
    for hd in range(HG_H):
        sl = slice(HG_K * hd, HG_K * (hd + 1))
        a = lax.dot_general(qa[:, sl], ka[:, sl], nt, preferred_element_type=F32)
        a = jnp.where(tri, a, 0.0).astype(BF16)
        st = st_scr[hd]
        inter = lax.dot_general(qs[:, sl], st.astype(BF16), nt, preferred_element_type=F32)
        inter_scr[:, sl] = inter
        o_ref[0, :, sl] = jnp.dot(a, vb[:, sl], preferred_element_type=F32) + inter
        st_scr[hd] = st * etot[:, sl] + lax.dot_general(vb[:, sl], kd[:, sl], tn,
                                                        preferred_element_type=F32)

    @pl.when(jnp.max(jnp.abs(cum - mid)) > HG_SAFE_SPAN)
    def _():
        cum_scr[...] = cum
        k_scr[...] = kk
        c_idx = lax.broadcasted_iota(jnp.int32, (D, 128), 0) // HG_K
        h_idx = lax.broadcasted_iota(jnp.int32, (D, 128), 1)
        head_sum = (c_idx == h_idx).astype(F32)
        c_idx_t = lax.broadcasted_iota(jnp.int32, (128, D), 1) // HG_K
        h_idx_t = lax.broadcasted_iota(jnp.int32, (128, D), 0)
        head_bcast = (c_idx_t == h_idx_t).astype(F32)
        row = lax.broadcasted_iota(jnp.int32, (CHUNK, 1), 0)

        def source_row(s, acc):
            seen = (row >= s) if d == 0 else (row <= s)
            w = jnp.exp(jnp.where(seen, cum - cum_scr[pl.ds(s, 1), :], -jnp.inf))
            p = q * k_scr[pl.ds(s, 1), :] * w
            a_s = jnp.dot(p, head_sum, precision=HIGHEST, preferred_element_type=F32)
            a_s = jnp.dot(a_s, head_bcast, precision=HIGHEST, preferred_element_type=F32)
            return acc + a_s * v_ref[pl.ds(s, 1), :]

        intra = lax.fori_loop(0, CHUNK, source_row, jnp.zeros((CHUNK, D), F32))
        o_ref[0] = inter_scr[...] + intra

    @pl.when(ends)
    def _():
        for hd in range(HG_H):
            fin_ref[0, 0, hd] = st_scr[hd].T


def _hg_scan_kernel(*refs, layer):
    d = pl.program_id(0)

    @pl.when(d == 0)
    def _():
        _hg_scan_body(0, layer, *refs)

    @pl.when(d == 1)
    def _():
        _hg_scan_body(1, layer, *refs)


def hg_scan(proj, lb_logits, s0t, layer):
    def tok(col):
        return lambda d, c: (_chunk_of_step(d, c), col)

    def lat_idx(d, c):
        return jnp.maximum(_seq_of_chunk(_chunk_of_step(d, c)) - N_CTX_SEQ, 0)

    def fin_idx(d, c):
        return jnp.minimum(_seq_of_chunk(_chunk_of_step(d, c)), N_CTX_SEQ)

    return pl.pallas_call(
        functools.partial(_hg_scan_kernel, layer=layer),
        grid=(2, N_CHUNKS),
        in_specs=[pl.BlockSpec((CHUNK, D), tok(0)),
                  pl.BlockSpec((CHUNK, D), tok(1)),
                  pl.BlockSpec((CHUNK, D), lambda d, c: (_chunk_of_step(d, c), 3 + d)),
                  pl.BlockSpec((1, DEPTH, D), lambda d, c: (d, 0, 0)),
                  pl.BlockSpec((1, 1, HG_H, HG_K, HG_K), lambda d, c: (d, lat_idx(d, c), 0, 0, 0))],
        out_specs=[pl.BlockSpec((1, CHUNK, D), lambda d, c: (d, _chunk_of_step(d, c), 0)),
                   pl.BlockSpec((1, 1, HG_H, HG_K, HG_K), lambda d, c: (fin_idx(d, c), d, 0, 0, 0))],
        out_shape=[jax.ShapeDtypeStruct((2, N_TOK, D), F32),
                   jax.ShapeDtypeStruct((N_CTX_SEQ + 1, 2, HG_H, HG_K, HG_K), F32)],
        scratch_shapes=[pltpu.VMEM((HG_H, HG_K, HG_K), F32),
                        pltpu.VMEM((CHUNK, D), F32), pltpu.VMEM((CHUNK, D), F32),
                        pltpu.VMEM((CHUNK, D), F32)],
        compiler_params=_cparams("arbitrary", "arbitrary"),
        name="hg_scan",
    )(proj, proj, proj, lb_logits, s0t)


def _hg_out_kernel(o0_ref, o1_ref, gate_ref, gn_ref, w_ref, x_ref, g1_ref, out_ref, on_scr):
    @pl.when(pl.program_id(1) == 0)
    def _():
        for hd in range(HG_H):
            sl = slice(HG_K * hd, HG_K * (hd + 1))
            o = o0_ref[0, :, sl] + o1_ref[0, :, sl]
            o = o * lax.rsqrt(jnp.mean(o * o, axis=-1, keepdims=True) + NORM_EPS) * gn_ref[...]
            on_scr[:, sl] = (o * _silu(gate_ref[:, sl])).astype(BF16)

    out_ref[...] = x_ref[...] + g1_ref[0] * jnp.dot(on_scr[...], w_ref[...].astype(BF16),
                                                    preferred_element_type=F32)


def hg_out(o, proj, g_norm, w_o, x, g1, tn=512):
    return pl.pallas_call(
        _hg_out_kernel,
        grid=(N_TOK // TM, D // tn),
        in_specs=[pl.BlockSpec((1, TM, D), lambda i, j: (0, i, 0)),
                  pl.BlockSpec((1, TM, D), lambda i, j: (1, i, 0)),
                  pl.BlockSpec((TM, D), lambda i, j: (i, 2)),
                  pl.BlockSpec((1, HG_K), lambda i, j: (0, 0)),
                  pl.BlockSpec((D, tn), lambda i, j: (0, j)),
                  pl.BlockSpec((TM, tn), lambda i, j: (i, j)),
                  pl.BlockSpec((1, 1, tn), lambda i, j: (_group_of_tile(i), 0, j))],
        out_specs=pl.BlockSpec((TM, tn), lambda i, j: (i, j)),
        out_shape=jax.ShapeDtypeStruct((N_TOK, D), F32),
        scratch_shapes=[pltpu.VMEM((TM, D), BF16)],
        compiler_params=_cparams("parallel", "arbitrary"),
        name="hg_out",
    )(o, o, proj, g_norm.reshape(1, HG_K), w_o, x, g1)


def hgrn_layer(x, mods, g_mix, state, layer, lb_logits, w_qig, w_f, b_f, g_norm, w_o):
    sh1, sc1, g1 = mods
    w5 = jnp.concatenate([w_qig, w_f[0], w_f[1]], axis=1)
    b5 = jnp.concatenate([jnp.zeros((3 * D,), F32), b_f[0], b_f[1]])
    proj = nm_matmul(x, g_mix, sc1, sh1, w5, b5, 5 * D, name="hg_proj")
    s0t = jnp.transpose(state, (1, 0, 2, 4, 3))
    o, fin = hg_scan(proj, lb_logits, s0t, layer)
    x = hg_out(o, proj, g_norm, w_o, x, g1)
    return x, fin[:N_CTX_SEQ]


SSD_INNER = 2 * D
SSD_HEADS = 32
SSD_P = 64
SSD_NG = 4
SSD_N = 128
SSD_XBC = SSD_INNER + 2 * SSD_NG * SSD_N
SSD_ZX = SSD_INNER + SSD_XBC
SSD_CONV = 5
CONV_TM = 256
CONV_HALO = 8


def _ssd_conv_kernel(cur_ref, prev_ref, next_ref, w_ref, b_ref, o_ref, ext):
    i = pl.program_id(0)
    n_ctx_tiles = N_CTX_SEQ * CTX_LEN // CONV_TM
    per_lat = LAT_LEN // CONV_TM
    is_ctx = i < n_ctx_tiles
    k = (i - n_ctx_tiles) % per_lat
    seq_start = is_ctx | (k == 0)
    seq_end = is_ctx | (k == per_lat - 1)
    ext[0:CONV_HALO] = jnp.where(seq_start, 0.0, prev_ref[...])
    ext[CONV_HALO:CONV_HALO + CONV_TM] = cur_ref[...]
    ext[CONV_HALO + CONV_TM:] = jnp.where(seq_end, 0.0, next_ref[...])
    acc = jnp.broadcast_to(b_ref[...], (CONV_TM, D))
    for t in range(SSD_CONV):
        acc = acc + w_ref[t:t + 1, :] * ext[pl.ds(CONV_HALO - SSD_CONV // 2 + t, CONV_TM), :]
    o_ref[...] = _silu(acc)


def ssd_conv(zx, conv_w, conv_b):
    nrb = N_TOK // CONV_HALO
    rpt = CONV_TM // CONV_HALO
    c0 = SSD_INNER // D
    return pl.pallas_call(
        _ssd_conv_kernel,
        grid=(N_TOK // CONV_TM, SSD_XBC // D),
        in_specs=[pl.BlockSpec((CONV_TM, D), lambda i, j: (i, c0 + j)),
                  pl.BlockSpec((CONV_HALO, D), lambda i, j: (jnp.maximum(i * rpt - 1, 0), c0 + j)),
                  pl.BlockSpec((CONV_HALO, D), lambda i, j: (jnp.minimum((i + 1) * rpt, nrb - 1), c0 + j)),
                  pl.BlockSpec((SSD_CONV, D), lambda i, j: (0, j)),
                  pl.BlockSpec((1, D), lambda i, j: (0, j))],
        out_specs=pl.BlockSpec((CONV_TM, D), lambda i, j: (i, j)),
        out_shape=jax.ShapeDtypeStruct((N_TOK, SSD_XBC), F32),
        scratch_shapes=[pltpu.VMEM((CONV_TM + 2 * CONV_HALO, D), F32)],
        compiler_params=_cparams("parallel", "parallel"),
        name="ssd_conv",
    )(zx, zx, zx, conv_w, conv_b.reshape(1, SSD_XBC))


def _ssd_scan_body(d, xlo_ref, xhi_ref, bc_ref, dtr_ref, dtb_ref, alog_ref, h0_ref,
                   y_ref, fin_ref, ht_scr):
    ce = _chunk_of_step(d, pl.program_id(1))
    is_ctx, starts, ends = _chunk_flags(d, ce)

    @pl.when(starts & is_ctx)
    def _():
        ht_scr[...] = jnp.zeros_like(ht_scr)

    @pl.when(starts & jnp.logical_not(is_ctx))
    def _():
        ht_scr[...] = h0_ref[0, 0]

    xr = dtr_ref[...] + dtb_ref[...]
    dt = jnp.maximum(xr, 0.0) + jnp.log(1.0 + jnp.exp(-jnp.abs(xr)))
    dta = dt * (-jnp.exp(alog_ref[...]))
    tri = _tri(d, (CHUNK, CHUNK))
    cum = jnp.dot(tri.astype(F32), dta, precision=HIGHEST, preferred_element_type=F32)
    r = lax.broadcasted_iota(jnp.int32, (CHUNK, 2 * CHUNK), 0)
    cc = lax.broadcasted_iota(jnp.int32, (CHUNK, 2 * CHUNK), 1)
    lo_half = cc < CHUNK
    ccm = jnp.where(lo_half, cc, cc - CHUNK)
    trit = (ccm >= r) if d == 0 else (ccm <= r)
    tn = (((0,), (0,)), ((), ()))
    nt = (((1,), (1,)), ((), ()))
    cumt_lo = lax.dot_general(dta, (trit & lo_half).astype(F32), tn, precision=HIGHEST,
                              preferred_element_type=F32)
    cumt_hi = lax.dot_general(dta, (trit & jnp.logical_not(lo_half)).astype(F32), tn,
                              precision=HIGHEST, preferred_element_type=F32)
    lane = lax.broadcasted_iota(jnp.int32, (CHUNK, 2 * SSD_P), 1)
    first_head = lane < SSD_P
    tri2 = (r >= ccm) if d == 0 else (r <= ccm)
    bc = bc_ref[...]
    for gq in range(SSD_NG):
        bg = bc[:, SSD_N * gq:SSD_N * (gq + 1)].astype(BF16)
        cg = bc[:, SSD_NG * SSD_N + SSD_N * gq:SSD_NG * SSD_N + SSD_N * (gq + 1)].astype(BF16)
        cb2 = lax.dot_general(cg, jnp.concatenate([bg, bg], axis=0), nt,
                              preferred_element_type=F32)
        for pp in range(4 * gq, 4 * gq + 4):
            h1 = SSD_HEADS * d + 2 * pp
            colp = jnp.where(first_head, cum[:, h1:h1 + 1], cum[:, h1 + 1:h1 + 2])
            rowp = cumt_lo[h1:h1 + 1, :] + cumt_hi[h1 + 1:h1 + 2, :]
            lmat = jnp.exp(jnp.where(tri2, colp - rowp, -jnp.inf))
            dtp = jnp.where(first_head, dt[:, h1:h1 + 1], dt[:, h1 + 1:h1 + 2])
            xref = xlo_ref if pp < 8 else xhi_ref
            c0 = 128 * (pp % 8)
            xdt = xref[:, c0:c0 + 128] * dtp
            rhs = jnp.concatenate([jnp.where(first_head, xdt, 0.0),
                                   jnp.where(first_head, 0.0, xdt)], axis=0).astype(BF16)
            y = jnp.dot((cb2 * lmat).astype(BF16), rhs, preferred_element_type=F32)
            ht = ht_scr[:, 128 * pp:128 * (pp + 1)]
            y = y + jnp.dot(cg, ht.astype(BF16), preferred_element_type=F32) * jnp.exp(colp)
            y_ref[0, :, 128 * pp:128 * (pp + 1)] = y
            totp = colp[CHUNK - 1:CHUNK] if d == 0 else colp[0:1]
            xw = (xdt * jnp.exp(totp - colp)).astype(BF16)
            ht_scr[:, 128 * pp:128 * (pp + 1)] = (
                ht * jnp.exp(totp) + lax.dot_general(bg, xw, tn, preferred_element_type=F32))

    @pl.when(ends)
    def _():
        fin_ref[0, 0] = ht_scr[...]


def _ssd_scan_kernel(*refs):
    d = pl.program_id(0)

    @pl.when(d == 0)
    def _():
        _ssd_scan_body(0, *refs)

    @pl.when(d == 1)
    def _():
        _ssd_scan_body(1, *refs)


def ssd_scan(xbc, dtr, dt_bias, a_log, h0t):
    nh2 = 2 * SSD_HEADS
    hp = SSD_HEADS * SSD_P

    def tok(col):
        return lambda d, c: (_chunk_of_step(d, c), col)

    def lat_idx(d, c):
        return jnp.maximum(_seq_of_chunk(_chunk_of_step(d, c)) - N_CTX_SEQ, 0)

    def fin_idx(d, c):
        return jnp.minimum(_seq_of_chunk(_chunk_of_step(d, c)), N_CTX_SEQ)

    return pl.pallas_call(
        _ssd_scan_kernel,
        grid=(2, N_CHUNKS),
        in_specs=[pl.BlockSpec((CHUNK, D), tok(0)),
                  pl.BlockSpec((CHUNK, D), tok(1)),
                  pl.BlockSpec((CHUNK, D), tok(2)),
                  pl.BlockSpec((CHUNK, nh2), tok(0)),
                  pl.BlockSpec((1, nh2), lambda d, c: (0, 0)),
                  pl.BlockSpec((1, nh2), lambda d, c: (0, 0)),
                  pl.BlockSpec((1, 1, SSD_N, hp), lambda d, c: (d, lat_idx(d, c), 0, 0))],
        out_specs=[pl.BlockSpec((1, CHUNK, hp), lambda d, c: (d, _chunk_of_step(d, c), 0)),
                   pl.BlockSpec((1, 1, SSD_N, hp), lambda d, c: (fin_idx(d, c), d, 0, 0))],
        out_shape=[jax.ShapeDtypeStruct((2, N_TOK, hp), F32),
                   jax.ShapeDtypeStruct((N_CTX_SEQ + 1, 2, SSD_N, hp), F32)],
        scratch_shapes=[pltpu.VMEM((SSD_N, hp), F32)],
        compiler_params=_cparams("arbitrary", "arbitrary"),
        name="ssd_scan",
    )(xbc, xbc, xbc, dtr, dt_bias.reshape(1, nh2), a_log.reshape(1, nh2), h0t)


SSD_OUT_TM = 512


def _ssd_out_kernel(xlo_ref, xhi_ref, zlo_ref, zhi_ref, y0_ref, y1_ref, dsk_ref, gn_ref, w_ref,
                    x_ref, g1_ref, out_ref, yn_scr):
    @pl.when(pl.program_id(1) == 0)
    def _():
        halves = []
        ss = jnp.zeros((SSD_OUT_TM, 1), F32)
        for k, (xr, zr) in enumerate(((xlo_ref, zlo_ref), (xhi_ref, zhi_ref))):
            sl = slice(D * k, D * (k + 1))
            y = dsk_ref[:, sl] * xr[...] + y0_ref[0, :, sl] + y1_ref[0, :, sl]
            y = y * _silu(zr[...])
            ss = ss + jnp.sum(y * y, axis=-1, keepdims=True)
            halves.append(y)
        scale = lax.rsqrt(ss / SSD_INNER + NORM_EPS)
        for k, y in enumerate(halves):
            sl = slice(D * k, D * (k + 1))
            yn_scr[:, sl] = (y * scale * gn_ref[:, sl]).astype(BF16)

    out_ref[...] = x_ref[...] + g1_ref[0] * jnp.dot(yn_scr[...], w_ref[...].astype(BF16),
                                                    preferred_element_type=F32)


def ssd_out(xbc, zx, y, d_skip_cols, g_norm, w_out, x, g1, tn=512):
    tm = SSD_OUT_TM
    tpg = GROUP_ROWS // tm
    return pl.pallas_call(
        _ssd_out_kernel,
        grid=(N_TOK // tm, D // tn),
        in_specs=[pl.BlockSpec((tm, D), lambda i, j: (i, 0)),
                  pl.BlockSpec((tm, D), lambda i, j: (i, 1)),
                  pl.BlockSpec((tm, D), lambda i, j: (i, 0)),
                  pl.BlockSpec((tm, D), lambda i, j: (i, 1)),
                  pl.BlockSpec((1, tm, SSD_INNER), lambda i, j: (0, i, 0)),
                  pl.BlockSpec((1, tm, SSD_INNER), lambda i, j: (1, i, 0)),
                  pl.BlockSpec((1, SSD_INNER), lambda i, j: (0, 0)),
                  pl.BlockSpec((1, SSD_INNER), lambda i, j: (0, 0)),
                  pl.BlockSpec((SSD_INNER, tn), lambda i, j: (0, j)),
                  pl.BlockSpec((tm, tn), lambda i, j: (i, j)),
                  pl.BlockSpec((1, 1, tn), lambda i, j: (i // tpg, 0, j))],
        out_specs=pl.BlockSpec((tm, tn), lambda i, j: (i, j)),
        out_shape=jax.ShapeDtypeStruct((N_TOK, D), F32),
        scratch_shapes=[pltpu.VMEM((tm, SSD_INNER), BF16)],
        compiler_params=_cparams("parallel", "arbitrary"),
        name="ssd_out",
    )(xbc, xbc, zx, zx, y, y, d_skip_cols, g_norm.reshape(1, SSD_INNER), w_out, x, g1)


def ssd_layer(x, mods, g_mix, state, w_in, conv_w, conv_b, dt_bias, a_log, d_skip, g_norm, w_out):
    sh1, sc1, g1 = mods
    zx = nm_matmul(x, g_mix, sc1, sh1, w_in, jnp.zeros((SSD_ZX,), F32), SSD_ZX, name="ssd_proj")
    nh2 = 2 * SSD_HEADS
    dtr = nm_matmul(x, g_mix, sc1, sh1, w_in[:, SSD_ZX:], jnp.zeros((nh2,), F32), nh2, tn=nh2,
                    name="ssd_proj_dt")
    xbc = ssd_conv(zx, conv_w, conv_b)
    h0t = jnp.transpose(state, (1, 0, 4, 2, 3)).reshape(2, N_LAT_SEQ, SSD_N, SSD_HEADS * SSD_P)
    y, fin = ssd_scan(xbc, dtr, dt_bias, a_log, h0t)
    dcols = jnp.repeat(d_skip, SSD_P).reshape(1, SSD_INNER)
    x = ssd_out(xbc, zx, y, dcols, g_norm, w_out, x, g1)
    fin = jnp.transpose(fin[:N_CTX_SEQ].reshape(N_CTX_SEQ, 2, SSD_N, SSD_HEADS, SSD_P), (0, 1, 3, 4, 2))
    return x, fin


N_EXP = 16
FF = 2 * D
CAP_CTX = 2 * CTX_LEN // N_EXP
CAP_LAT = 2 * LAT_LEN // N_EXP
SLOTS_PER_GROUP = 512
SLOTS = N_GROUPS * SLOTS_PER_GROUP


def _router_kernel(x_ref, g_ref, sc_ref, sh_ref, wt_ref, h_ref, aff_ref):
    h = _normmod(x_ref[...], g_ref[...], sc_ref[0], sh_ref[0])
    h_ref[...] = h.astype(BF16)
    logits = lax.dot_general(wt_ref[...], h, (((1,), (1,)), ((), ())), precision=HIGHEST,
                             preferred_element_type=F32)
    e = jnp.exp(logits - jnp.max(logits, axis=0, keepdims=True))
    aff_ref[...] = e / jnp.sum(e, axis=0, keepdims=True)


def moe_route(x, g, sc, sh, w_router_t):
    return pl.pallas_call(
        _router_kernel,
        grid=(N_TOK // TM,),
        in_specs=[pl.BlockSpec((TM, D), lambda i: (i, 0)),
                  pl.BlockSpec((1, D), lambda i: (0, 0)),
                  pl.BlockSpec((1, 1, D), lambda i: (_group_of_tile(i), 0, 0)),
                  pl.BlockSpec((1, 1, D), lambda i: (_group_of_tile(i), 0, 0)),
                  pl.BlockSpec((N_EXP, D), lambda i: (0, 0))],
        out_specs=[pl.BlockSpec((TM, D), lambda i: (i, 0)),
                   pl.BlockSpec((N_EXP, TM), lambda i: (0, i))],
        out_shape=[jax.ShapeDtypeStruct((N_TOK, D), BF16),
                   jax.ShapeDtypeStruct((N_EXP, N_TOK), F32)],
        compiler_params=_cparams("parallel"),
        name="moe_router",
    )(x, g.reshape(1, D), sc, sh, w_router_t)


def _lane_prefix_excl(m):
    s, t = m.shape
    r = lax.broadcasted_iota(jnp.int32, (128, 128), 0)
    c = lax.broadcasted_iota(jnp.int32, (128, 128), 1)
    upper = (r <= c).astype(BF16)
    run = jnp.zeros((s, 1), F32)
    out = []
    for k in range(t // 128):
        blk = m[:, 128 * k:128 * (k + 1)]
        inc = jnp.dot(blk.astype(BF16), upper, preferred_element_type=F32) + run
        out.append(inc - blk)
        run = inc[:, 127:128]
    return jnp.concatenate(out, axis=1)


def _select_kernel(a_ref, off_ref, slot_ref, base_ref, *, cap):
    bits = pltpu.bitcast(a_ref[...], jnp.int32)
    s = bits.shape[0]
    capf = float(cap)

    def body(_, lohi):
        lo, hi = lohi
        mid = lo + ((hi - lo + 1) >> 1)
        cnt = jnp.sum((bits >= mid).astype(F32), axis=1, keepdims=True)
        ok = cnt >= capf
        return jnp.where(ok, mid, lo), jnp.where(ok, hi, mid - 1)

    lo0 = jnp.zeros((s, 1), jnp.int32)
    hi0 = jnp.full((s, 1), 0x7F800000, jnp.int32)
    thr, _ = lax.fori_loop(0, 31, body, (lo0, hi0))
    gt = (bits > thr).astype(F32)
    eq = (bits == thr).astype(F32)
    need = capf - jnp.sum(gt, axis=1, keepdims=True)
    sel = gt + eq * (_lane_prefix_excl(eq) < need).astype(F32)
    slot = _lane_prefix_excl(sel) + off_ref[...]
    slot_ref[...] = jnp.where(sel > 0.0, slot, -1.0).astype(jnp.int32)
    t = bits.shape[1]
    tok = lax.broadcasted_iota(jnp.int32, (t, 128), 0)
    tile = lax.broadcasted_iota(jnp.int32, (t, 128), 1)
    ahead = (tok < tile * TOK_TILE).astype(BF16)
    base_ref[...] = jnp.dot(sel.astype(BF16), ahead, preferred_element_type=F32).astype(jnp.int32)


def moe_select(aff, off, cap):
    s, t = aff.shape
    return pl.pallas_call(
        functools.partial(_select_kernel, cap=cap),
        grid=(1,),
        in_specs=[pl.BlockSpec((s, t), lambda i: (0, 0)),
                  pl.BlockSpec((s, 1), lambda i: (0, 0))],
        out_specs=[pl.BlockSpec((s, t), lambda i: (0, 0)),
                   pl.BlockSpec((s, 128), lambda i: (0, 0))],
        out_shape=[jax.ShapeDtypeStruct((s, t), jnp.int32),
                   jax.ShapeDtypeStruct((s, 128), jnp.int32)],
        compiler_params=_cparams("arbitrary"),
        name="moe_select",
    )(aff, off)


TOK_TILE = 256
TILES = GROUP_ROWS // TOK_TILE
CNT_STRIDE = TILES + 1
SLOT_BLK = 128
TILES_PER_BLK = TILES * SLOT_BLK // SLOTS_PER_GROUP
GATHER_MARGIN = 2
BAND_MARGIN = 48


def _gather_kernel(cnt_ref, slot_ref, aff_ref, h_ref, xs_ref, gs_ref):
    base = (pl.program_id(0) * N_EXP + pl.program_id(1)) * CNT_STRIDE

    def gather(k, toks):
        n = toks.stop - toks.start
        s = lax.broadcasted_iota(jnp.int32, (SLOT_BLK, n), 0) + SLOT_BLK * k
        oh = s == slot_ref[0, 0, :, toks]
        rows = slice(SLOT_BLK * k, SLOT_BLK * (k + 1))
        xs_ref[0, rows, :] = jnp.dot(oh.astype(BF16), h_ref[toks, :],
                                     preferred_element_type=F32).astype(BF16)
        gs_ref[0, rows, :] = jnp.sum(jnp.where(oh, aff_ref[0, :, toks], 0.0), axis=1, keepdims=True)

    for k in range(SLOTS_PER_GROUP // SLOT_BLK):
        lo = max(TILES_PER_BLK * k - GATHER_MARGIN, 0)
        hi = min(TILES_PER_BLK * (k + 1) + GATHER_MARGIN, TILES)
        in_band = (cnt_ref[base + lo] <= SLOT_BLK * k) & (cnt_ref[base + hi] >= SLOT_BLK * (k + 1))

        @pl.when(in_band)
        def _():
            gather(k, slice(TOK_TILE * lo, TOK_TILE * hi))

        @pl.when(jnp.logical_not(in_band))
        def _():
            gather(k, slice(0, GROUP_ROWS))


def moe_gather(cnt, slot, aff3, h2):
    return pl.pallas_call(
        _gather_kernel,
        grid_spec=pltpu.PrefetchScalarGridSpec(
            num_scalar_prefetch=1,
            grid=(N_GROUPS, N_EXP),
            in_specs=[pl.BlockSpec((1, 1, 1, GROUP_ROWS), lambda r, e, c: (r, e, 0, 0)),
                      pl.BlockSpec((1, 1, GROUP_ROWS), lambda r, e, c: (e, 0, r)),
                      pl.BlockSpec((GROUP_ROWS, D), lambda r, e, c: (r, 0))],
            out_specs=[pl.BlockSpec((1, SLOTS_PER_GROUP, D), lambda r, e, c: (e, r, 0)),
                       pl.BlockSpec((1, SLOTS_PER_GROUP, 1), lambda r, e, c: (e, r, 0))]),
        out_shape=[jax.ShapeDtypeStruct((N_EXP, SLOTS, D), BF16),
                   jax.ShapeDtypeStruct((N_EXP, SLOTS, 1), F32)],
        compiler_params=_cparams("parallel", "parallel"),
        name="moe_gather",
    )(cnt, slot, aff3, h2)


FF_TILE = 512


def _ffn_kernel(xs_ref, gs_ref, wg_ref, wu_ref, wd_ref, ys_ref, acc):
    f = pl.program_id(1)
    x = xs_ref[0]
    g = jnp.dot(x, wg_ref[0, 0].astype(BF16), preferred_element_type=F32)
    u = jnp.dot(x, wu_ref[0, 0].astype(BF16), preferred_element_type=F32)
    hid = (_silu(g) * u).astype(BF16)
    contrib = jnp.dot(hid, wd_ref[0, 0].astype(BF16), preferred_element_type=F32)

    @pl.when(f == 0)
    def _():
        acc[...] = contrib

    @pl.when(f > 0)
    def _():
        acc[...] += contrib

    @pl.when(f == FF // FF_TILE - 1)
    def _():
        ys_ref[0] = (acc[...] * gs_ref[0]).astype(BF16)


def moe_ffn(xs, gs, w_gate, w_up, w_down, layer):
    return pl.pallas_call(
        _ffn_kernel,
        grid=(N_EXP, FF // FF_TILE),
        in_specs=[pl.BlockSpec((1, SLOTS, D), lambda e, f: (e, 0, 0)),
                  pl.BlockSpec((1, SLOTS, 1), lambda e, f: (e, 0, 0)),
                  pl.BlockSpec((1, 1, D, FF_TILE), lambda e, f: (layer, e, 0, f)),
                  pl.BlockSpec((1, 1, D, FF_TILE), lambda e, f: (layer, e, 0, f)),
                  pl.BlockSpec((1, 1, FF_TILE, D), lambda e, f: (layer, e, f, 0))],
        out_specs=pl.BlockSpec((1, SLOTS, D), lambda e, f: (e, 0, 0)),
        out_shape=jax.ShapeDtypeStruct((N_EXP, SLOTS, D), BF16),
        scratch_shapes=[pltpu.VMEM((SLOTS, D), F32)],
        compiler_params=_cparams("parallel", "arbitrary"),
        name="moe_ffn",
    )(xs, gs, w_gate, w_up, w_down)


SCAT_TN = 512


def _scatter_kernel(cnt_ref, slot_ref, ys_ref, x_ref, g2_ref, o_ref):
    r, j = pl.program_id(0), pl.program_id(2)
    cap_tile = SLOTS_PER_GROUP // TILES
    start = pl.multiple_of(jnp.clip(cap_tile * j - BAND_MARGIN, 0, SLOTS_PER_GROUP - SLOT_BLK), 16)
    in_band = True
    for e in range(N_EXP):
        base = (r * N_EXP + e) * CNT_STRIDE + j
        in_band = in_band & (cnt_ref[base] >= start) & (cnt_ref[base + 1] <= start + SLOT_BLK)

    def scatter(first, n):
        lane = lax.broadcasted_iota(jnp.int32, (TOK_TILE, n), 1) + first
        acc = jnp.zeros((TOK_TILE, SCAT_TN), F32)
        for e in range(N_EXP):
            oh = (slot_ref[0, :, e:e + 1] == lane).astype(BF16)
            acc = acc + jnp.dot(oh, ys_ref[e, pl.ds(first, n), :], preferred_element_type=F32)
        o_ref[...] = x_ref[...] + g2_ref[0] * acc

    @pl.when(in_band)
    def _():
        scatter(start, SLOT_BLK)

    @pl.when(jnp.logical_not(in_band))
    def _():
        scatter(0, SLOTS_PER_GROUP)


def moe_scatter(cnt, slot_t, ys, x, g2):
    return pl.pallas_call(
        _scatter_kernel,
        grid_spec=pltpu.PrefetchScalarGridSpec(
            num_scalar_prefetch=1,
            grid=(N_GROUPS, D // SCAT_TN, TILES),
            in_specs=[pl.BlockSpec((1, TOK_TILE, 128), lambda r, c, j, n: (r, j, 0)),
                      pl.BlockSpec((N_EXP, SLOTS_PER_GROUP, SCAT_TN), lambda r, c, j, n: (0, r, c)),
                      pl.BlockSpec((TOK_TILE, SCAT_TN), lambda r, c, j, n: (r * TILES + j, c)),
                      pl.BlockSpec((1, 1, SCAT_TN), lambda r, c, j, n: (r, 0, c))],
            out_specs=pl.BlockSpec((TOK_TILE, SCAT_TN), lambda r, c, j, n: (r * TILES + j, c))),
        out_shape=jax.ShapeDtypeStruct((N_TOK, D), F32),
        compiler_params=_cparams("parallel", "parallel", "arbitrary"),
        name="moe_scatter",
    )(cnt, slot_t, ys, x, g2)


def moe_layer(x, mods, g_ffn, layer, w_router, w_gate, w_up, w_down):
    sh2, sc2, g2 = mods
    h2, aff = moe_route(x, g_ffn, sc2, sh2, w_router.T)
    aff_ctx = aff[:, :GROUP_ROWS].reshape(N_EXP * N_CTX_SEQ, CTX_LEN)
    off_ctx = jnp.tile(jnp.arange(N_CTX_SEQ, dtype=F32) * CAP_CTX, N_EXP).reshape(-1, 1)
    slot_ctx, _ = moe_select(aff_ctx, off_ctx, CAP_CTX)
    aff_lat = jnp.transpose(aff[:, GROUP_ROWS:].reshape(N_EXP, N_LAT_SEQ, LAT_LEN), (1, 0, 2)
                            ).reshape(N_LAT_SEQ * N_EXP, LAT_LEN)
    slot_lat, base_lat = moe_select(aff_lat, jnp.zeros((N_LAT_SEQ * N_EXP, 1), F32), CAP_LAT)
    slot = jnp.concatenate([slot_ctx.reshape(1, N_EXP, GROUP_ROWS),
                            slot_lat.reshape(N_LAT_SEQ, N_EXP, LAT_LEN)], axis=0)
    slot_t = jnp.pad(jnp.transpose(slot, (0, 2, 1)), ((0, 0), (0, 0), (0, 128 - N_EXP)),
                     constant_values=-1)
    slot = slot.reshape(N_GROUPS, N_EXP, 1, GROUP_ROWS)
    cnt_ctx = jnp.broadcast_to(jnp.arange(CNT_STRIDE, dtype=jnp.int32) * CAP_CTX, (1, N_EXP, CNT_STRIDE))
    cnt = jnp.concatenate([cnt_ctx, base_lat[:, :CNT_STRIDE].reshape(N_LAT_SEQ, N_EXP, CNT_STRIDE)],
                          axis=0).reshape(-1)
    xs, gs = moe_gather(cnt, slot, aff.reshape(N_EXP, 1, N_TOK), h2)
    ys = moe_ffn(xs, gs, w_gate, w_up, w_down, layer)
    return moe_scatter(cnt, slot_t, ys, x, g2)


def _final_norm_kernel(x_ref, g_ref, ctx_ref, lat_ref):
    x = x_ref[...]
    y = x * lax.rsqrt(jnp.mean(x * x, axis=-1, keepdims=True) + NORM_EPS) * g_ref[...]
    is_ctx = pl.program_id(0) < TILES_PER_GROUP

    @pl.when(is_ctx)
    def _():
        ctx_ref[...] = y

    @pl.when(jnp.logical_not(is_ctx))
    def _():
        lat_ref[...] = y


def final_norm(x, g):
    t = TILES_PER_GROUP
    return pl.pallas_call(
        _final_norm_kernel,
        grid=(N_TOK // TM,),
        in_specs=[pl.BlockSpec((TM, D), lambda i: (i, 0)),
                  pl.BlockSpec((1, D), lambda i: (0, 0))],
        out_specs=[pl.BlockSpec((TM, D), lambda i: (jnp.minimum(i, t - 1), 0)),
                   pl.BlockSpec((TM, D), lambda i: (jnp.maximum(i - t, 0), 0))],
        out_shape=[jax.ShapeDtypeStruct((GROUP_ROWS, D), F32),
                   jax.ShapeDtypeStruct((N_TOK - GROUP_ROWS, D), F32)],
        compiler_params=_cparams("arbitrary"),
        name="final_norm",
    )(x, g.reshape(1, D))


def _grid_pos_embed():
    rows = LAT_LEN // GRID_W
    quarter = D // 4
    omega = 1.0 / (10000.0 ** (jnp.arange(quarter, dtype=F32) / quarter))
    r = jnp.arange(rows, dtype=F32)[:, None] * omega
    cl = jnp.arange(GRID_W, dtype=F32)[:, None] * omega
    emb_r = jnp.concatenate([jnp.sin(r), jnp.cos(r)], axis=-1)
    emb_c = jnp.concatenate([jnp.sin(cl), jnp.cos(cl)], axis=-1)
    emb = jnp.concatenate([jnp.broadcast_to(emb_r[:, None], (rows, GRID_W, D // 2)),
                           jnp.broadcast_to(emb_c[None], (rows, GRID_W, D // 2))], axis=-1)
    return emb.reshape(LAT_LEN, D)


def kernel(x_prompt, x_sample, state_s5_re, state_s5_im, state_hgrn, state_ssd, c, c_ctx, w_ada, b_ada, norm_mix, norm_ffn, norm_final, s5_lam_re, s5_lam_im, s5_log_dt, s5_b_re, s5_b_im, s5_c_re, s5_c_im, s5_d, s5_w_glu, s5_b_glu, hg_w_qig, hg_w_f, hg_b_f, hg_lb_logits, hg_norm, hg_w_o, ssd_w_in, ssd_conv_w, ssd_conv_b, ssd_dt_bias, ssd_a_log, ssd_d, ssd_norm, ssd_w_out, moe_router, moe_w_gate, moe_w_up, moe_w_down):
    cond8 = jnp.concatenate([c_ctx[None], c, jnp.zeros((5, D), F32)], axis=0)
    mod = ada_mod(cond8, w_ada, b_ada)
    mods = jnp.transpose(mod.reshape(DEPTH, 8, 6, D)[:, :3], (0, 2, 1, 3)).reshape(DEPTH, 6, 3, 1, D)
    x = embed_tokens(x_prompt.reshape(-1, D), x_sample.reshape(-1, D), _grid_pos_embed())
    s5_re, s5_im, hg_fin, ssd_fin = [], [], [], []
    for i in range(DEPTH):
        mix_mods = (mods[i, 0], mods[i, 1], mods[i, 2])
        kind, j = i % 3, i // 3
        if kind == 0:
            x, fr, fi = s5_layer(x, mix_mods, norm_mix[i], state_s5_re[:, j], state_s5_im[:, j],
                                 s5_lam_re[j], s5_lam_im[j], s5_log_dt[j], s5_b_re[j], s5_b_im[j],
                                 s5_c_re[j], s5_c_im[j], s5_d[j], s5_w_glu[j], s5_b_glu[j])
            s5_re.append(fr)
            s5_im.append(fi)
        elif kind == 1:
            x, fh = hgrn_layer(x, mix_mods, norm_mix[i], state_hgrn[:, j], i, hg_lb_logits,
                               hg_w_qig[j], hg_w_f[j], hg_b_f[j], hg_norm[j], hg_w_o[j])
            hg_fin.append(fh)
        else:
            x, fs = ssd_layer(x, mix_mods, norm_mix[i], state_ssd[:, j], ssd_w_in[j], ssd_conv_w[j],
                              ssd_conv_b[j], ssd_dt_bias[j], ssd_a_log[j], ssd_d[j], ssd_norm[j],
                              ssd_w_out[j])
            ssd_fin.append(fs)
        x = moe_layer(x, (mods[i, 3], mods[i, 4], mods[i, 5]), norm_ffn[i], i, moe_router[i],
                      moe_w_gate, moe_w_up, moe_w_down)
    y_ctx, y_lat = final_norm(x, norm_final)
    return (y_ctx.reshape(N_CTX_SEQ, CTX_LEN, D), y_lat.reshape(N_LAT_SEQ, LAT_LEN, D),
            jnp.stack(s5_re, axis=1), jnp.stack(s5_im, axis=1),
            jnp.stack(hg_fin, axis=1), jnp.stack(ssd_fin, axis=1))
```

```python
import functools
import math

import jax
import jax.numpy as jnp
from jax import lax
from jax.experimental import pallas as pl
from jax.experimental.pallas import tpu as pltpu

F32 = jnp.float32
BF16 = jnp.bfloat16
HIGHEST = lax.Precision.HIGHEST

D = 1024
DEPTH = 4
N_CTX_SEQ = 16
CTX_LEN = 256
N_LAT_SEQ = 2
LAT_LEN = 4096
GROUP_ROWS = 4096
N_GROUPS = 3
N_TOK = N_GROUPS * GROUP_ROWS
N_SEQ = N_CTX_SEQ + N_LAT_SEQ
NORM_EPS = 1e-6
GRID_W = 64

VMEM_LIMIT_BYTES = 56 * 1024 * 1024


def _cparams(*sem):
    return pltpu.CompilerParams(dimension_semantics=sem, vmem_limit_bytes=VMEM_LIMIT_BYTES)


def _silu(x):
    return x * jax.nn.sigmoid(x)


def _normmod(x, g, sc, sh):
    ms = jnp.mean(x * x, axis=-1, keepdims=True)
    return x * lax.rsqrt(ms + NORM_EPS) * g * (1.0 + sc) + sh


def _cmul(ar, ai, br, bi):
    return ar * br - ai * bi, ar * bi + ai * br


def _mod_kernel(c_ref, w_ref, b_ref, o_ref):
    o_ref[0] = jnp.dot(_silu(c_ref[...]), w_ref[0], precision=HIGHEST,
                       preferred_element_type=F32) + b_ref[0]


def ada_mod(cond8, w_ada, b_ada):
    tn = 1536
    return pl.pallas_call(
        _mod_kernel,
        grid=(DEPTH, 6 * D // tn),
        in_specs=[pl.BlockSpec((8, D), lambda i, j: (0, 0)),
                  pl.BlockSpec((1, D, tn), lambda i, j: (i, 0, j)),
                  pl.BlockSpec((1, 1, tn), lambda i, j: (i, 0, j))],
        out_specs=pl.BlockSpec((1, 8, tn), lambda i, j: (i, 0, j)),
        out_shape=jax.ShapeDtypeStruct((DEPTH, 8, 6 * D), F32),
        compiler_params=_cparams("parallel", "parallel"),
        name="ada_mod",
    )(cond8, w_ada, b_ada.reshape(DEPTH, 1, 6 * D))


def _embed_kernel(xp_ref, xs_ref, pos_ref, o_ref):
    r = pl.program_id(0)

    @pl.when(r == 0)
    def _():
        o_ref[...] = xp_ref[...]

    @pl.when(r > 0)
    def _():
        o_ref[...] = xs_ref[...] + pos_ref[...]


def embed_tokens(xp, xs, pos):
    tm = 1024
    nt = GROUP_ROWS // tm
    return pl.pallas_call(
        _embed_kernel,
        grid=(N_GROUPS, nt),
        in_specs=[pl.BlockSpec((tm, D), lambda r, i: (jnp.where(r == 0, i, 0), 0)),
                  pl.BlockSpec((tm, D), lambda r, i: (jnp.where(r == 0, 0, (r - 1) * nt + i), 0)),
                  pl.BlockSpec((tm, D), lambda r, i: (i, 0))],
        out_specs=pl.BlockSpec((tm, D), lambda r, i: (r * nt + i, 0)),
        out_shape=jax.ShapeDtypeStruct((N_TOK, D), F32),
        compiler_params=_cparams("parallel", "parallel"),
        name="embed_tokens",
    )(xp, xs, pos)


TM = 1024
TILES_PER_GROUP = GROUP_ROWS // TM


def _group_of_tile(i):
    return i // TILES_PER_GROUP


def _normmod_kernel(x_ref, g_ref, sc_ref, sh_ref, o_ref):
    o_ref[...] = _normmod(x_ref[...], g_ref[...], sc_ref[0], sh_ref[0])


def normmod(x, g, sc, sh):
    return pl.pallas_call(
        _normmod_kernel,
        grid=(N_TOK // TM,),
        in_specs=[pl.BlockSpec((TM, D), lambda i: (i, 0)),
                  pl.BlockSpec((1, D), lambda i: (0, 0)),
                  pl.BlockSpec((1, 1, D), lambda i: (_group_of_tile(i), 0, 0)),
                  pl.BlockSpec((1, 1, D), lambda i: (_group_of_tile(i), 0, 0))],
        out_specs=pl.BlockSpec((TM, D), lambda i: (i, 0)),
        out_shape=jax.ShapeDtypeStruct((N_TOK, D), F32),
        compiler_params=_cparams("parallel"),
        name="normmod",
    )(x, g.reshape(1, D), sc, sh)


def _nm_matmul_kernel(x_ref, g_ref, sc_ref, sh_ref, w_ref, b_ref, o_ref, h_scr):
    @pl.when(pl.program_id(1) == 0)
    def _():
        h_scr[...] = _normmod(x_ref[...], g_ref[...], sc_ref[0], sh_ref[0]).astype(BF16)

    o_ref[...] = jnp.dot(h_scr[...], w_ref[...].astype(BF16),
                         preferred_element_type=F32) + b_ref[...]


def nm_matmul(x, g, sc, sh, w, b, n_out, tn=512, name="nm_matmul"):
    return pl.pallas_call(
        _nm_matmul_kernel,
        grid=(N_TOK // TM, n_out // tn),
        in_specs=[pl.BlockSpec((TM, D), lambda i, j: (i, 0)),
                  pl.BlockSpec((1, D), lambda i, j: (0, 0)),
                  pl.BlockSpec((1, 1, D), lambda i, j: (_group_of_tile(i), 0, 0)),
                  pl.BlockSpec((1, 1, D), lambda i, j: (_group_of_tile(i), 0, 0)),
                  pl.BlockSpec((D, tn), lambda i, j: (0, j)),
                  pl.BlockSpec((1, tn), lambda i, j: (0, j))],
        out_specs=pl.BlockSpec((TM, tn), lambda i, j: (i, j)),
        out_shape=jax.ShapeDtypeStruct((N_TOK, n_out), F32),
        scratch_shapes=[pltpu.VMEM((TM, D), BF16)],
        compiler_params=_cparams("parallel", "arbitrary"),
        name=name,
    )(x, g.reshape(1, D), sc, sh, w, b.reshape(1, -1))


S5_G = 64
S5_H = 16
S5_P = 64
S5_L = 16
S5_GB = 8
S5_NB = S5_G // S5_GB
S5_CH = GROUP_ROWS // S5_L
S5_CTX_CH = CTX_LEN // S5_L
S5_NPOW = 8
S5_XW = S5_L * 128
S5_SW = 2 * S5_GB * S5_P


S5_LH = S5_L * S5_H
S5_NPWR = 24


def _s5_prep_kernel(lr_ref, li_ref, ldt_ref, btr_ref, bti_ref, cxr_ref, cxi_ref,
                    sel0_ref, sel1_ref, exp_ref,
                    kk_ref, wst_ref, wout_ref, apr_ref, api_ref):
    dt = jnp.exp(ldt_ref[0])
    lam_r, lam_i = lr_ref[0], li_ref[0]
    ar, ai = lam_r * dt, lam_i * dt
    pw = lax.broadcasted_iota(jnp.int32, (S5_GB, S5_NPWR, 2 * S5_P), 1).astype(F32)
    ep = jnp.exp(pw * ar)
    pwr, pwi = ep * jnp.cos(pw * ai), ep * jnp.sin(pw * ai)
    den = lam_r * lam_r + lam_i * lam_i
    nr, ni = pwr[:, 1:2] - 1.0, pwi[:, 1:2]
    beta_r = (nr * lam_r + ni * lam_i) / den
    beta_i = (ni * lam_r - nr * lam_i) / den
    bbr, bbi = _cmul(beta_r, beta_i, btr_ref[0], bti_ref[0])

    lane = lax.broadcasted_iota(jnp.int32, (S5_H, 2 * S5_P), 1)
    wst_ref[0, 0] = jnp.zeros((S5_XW, S5_SW), BF16)
    half = S5_SW // 2
    for g in range(S5_GB):
        mine = (lane >= S5_P) if g % 2 else (lane < S5_P)
        col = 128 * (g // 2)
        for s in range(S5_L):
            k = S5_L - 1 - s
            wr, wi = _cmul(pwr[g, k:k + 1], pwi[g, k:k + 1], bbr[g], bbi[g])
            rows = slice(128 * s + S5_H * g, 128 * s + S5_H * (g + 1))
            wst_ref[0, 0, rows, col:col + 128] = jnp.where(mine, wr, 0.0).astype(BF16)
            wst_ref[0, 0, rows, half + col:half + col + 128] = jnp.where(mine, wi, 0.0).astype(BF16)

    lane1 = lax.broadcasted_iota(jnp.int32, (1, 2 * S5_P), 1)

    def group_lanes(a):
        return jnp.concatenate([jnp.where(lane1 < S5_P, a[2 * q], a[2 * q + 1])
                                for q in range(S5_GB // 2)], axis=1)

    pr, pi_ = pwr[:, S5_L:S5_L + 1], pwi[:, S5_L:S5_L + 1]
    for k in range(S5_NPOW):
        apr_ref[0, 0, k:k + 1, :] = group_lanes(pr)
        api_ref[0, 0, k:k + 1, :] = group_lanes(pi_)
        pr, pi_ = _cmul(pr, pi_, pr, pi_)

    tn = (((0,), (0,)), ((), ()))
    kks, wre, wim = [], [], []
    for g in range(S5_GB):
        pg_r, pg_i = pwr[g, :, :S5_P], pwi[g, :, :S5_P]
        cr, ci = cxr_ref[0, g], cxi_ref[0, g]

        def c_times_pow(sel):
            er = lax.dot_general(pg_r, sel, tn, precision=HIGHEST, preferred_element_type=F32)
            ei = lax.dot_general(pg_i, sel, tn, precision=HIGHEST, preferred_element_type=F32)
            return _cmul(cr, ci, er, ei)

        k_r, k_i = c_times_pow(sel0_ref[...])
        kks.append(jnp.dot(bbr[g, :, :S5_P], k_r, precision=HIGHEST, preferred_element_type=F32)
                   - jnp.dot(bbi[g, :, :S5_P], k_i, precision=HIGHEST, preferred_element_type=F32))
        o_r, o_i = c_times_pow(sel1_ref[...])
        wre.append(o_r)
        wim.append(-o_i)
    glane = (lax.broadcasted_iota(jnp.int32, (1, S5_XW), 1) % 128) // S5_H

    def spread(parts, rows_per_group):
        a = jnp.concatenate(parts, axis=0).astype(BF16)
        a = jnp.dot(a, exp_ref[...], preferred_element_type=F32)
        grow = lax.broadcasted_iota(jnp.int32, (a.shape[0], 1), 0) // rows_per_group % S5_GB
        return jnp.where(grow == glane, a, 0.0).astype(BF16)

    wout_ref[0, 0] = spread(wre + wim, S5_P)
    kk_ref[0, 0] = spread(kks, S5_H)


def s5_prepare(lam_re, lam_im, log_dt, b_re, b_im, c_re, c_im):
    half = S5_SW // 2
    lr = jnp.tile(lam_re.reshape(2, S5_G, 1, S5_P), (1, 1, 1, 2))
    li = jnp.tile(lam_im.reshape(2, S5_G, 1, S5_P), (1, 1, 1, 2))
    ldt = log_dt.reshape(2, S5_G, 1, 1)
    btr = jnp.tile(jnp.swapaxes(b_re, 2, 3), (1, 1, 1, 2))
    bti = jnp.tile(jnp.swapaxes(b_im, 2, 3), (1, 1, 1, 2))
    cxr = jnp.tile(jnp.swapaxes(c_re, 2, 3), (1, 1, 1, S5_L))
    cxi = jnp.tile(jnp.swapaxes(c_im, 2, 3), (1, 1, 1, S5_L))
    k = jnp.arange(S5_NPWR)[:, None]
    t = (jnp.arange(S5_LH) // S5_H)[None, :]
    sel0 = (k == t).astype(F32)
    sel1 = (k == t + 1).astype(F32)
    src = jnp.arange(S5_LH)[:, None]
    dst = jnp.arange(S5_XW)[None, :]
    expand = ((src // S5_H == dst // 128) & (src % S5_H == dst % S5_H)).astype(BF16)

    def spec(*tail):
        return pl.BlockSpec((1, S5_GB) + tail, lambda d, j: (d, j) + (0,) * len(tail))

    def const(shape):
        return pl.BlockSpec(shape, lambda d, j: (0,) * len(shape))

    def blk(*tail):
        return pl.BlockSpec((1, 1) + tail, lambda d, j: (d, j) + (0,) * len(tail))

    return pl.pallas_call(
        _s5_prep_kernel,
        grid=(2, S5_NB),
        in_specs=[spec(1, 2 * S5_P), spec(1, 2 * S5_P), spec(1, 1),
                  spec(S5_H, 2 * S5_P), spec(S5_H, 2 * S5_P), spec(S5_P, S5_LH), spec(S5_P, S5_LH),
                  const((S5_NPWR, S5_LH)), const((S5_NPWR, S5_LH)), const((S5_LH, S5_XW))],
        out_specs=[blk(128, S5_XW), blk(S5_XW, S5_SW), blk(S5_SW, S5_XW),
                   blk(S5_NPOW, half), blk(S5_NPOW, half)],
        out_shape=[jax.ShapeDtypeStruct((2, S5_NB, 128, S5_XW), BF16),
                   jax.ShapeDtypeStruct((2, S5_NB, S5_XW, S5_SW), BF16),
                   jax.ShapeDtypeStruct((2, S5_NB, S5_SW, S5_XW), BF16),
                   jax.ShapeDtypeStruct((2, S5_NB, S5_NPOW, half), F32),
                   jax.ShapeDtypeStruct((2, S5_NB, S5_NPOW, half), F32)],
        compiler_params=_cparams("parallel", "parallel"),
        name="s5_prepare",
    )(lr, li, ldt, btr, bti, cxr, cxi, sel0, sel1, expand)


def _s5_scan_body(d, r, h_ref, kk_ref, wst_ref, wout_ref, apr_ref, api_ref, h0r_ref, h0i_ref,
                  y_ref, fr_ref, fi_ref, m8, zr_s, zi_s):
    half = S5_SW // 2

    @pl.when(r == 0)
    def _build():
        for s in range(S5_L):
            if s:
                m8[128 * s:128 * (s + 1), 0:128 * s] = jnp.zeros((128, 128 * s), BF16)
            m8[128 * s:128 * (s + 1), 128 * s:] = kk_ref[0, 0, :, :S5_XW - 128 * s]

    def tloc(s):
        return s if d == 0 else S5_L - 1 - s

    slabs = [h_ref[pl.ds(tloc(s), S5_CH, stride=S5_L), :] for s in range(S5_L)]
    x8 = jnp.concatenate(slabs, axis=1).astype(BF16)
    delta = jnp.dot(x8, wst_ref[0, 0], preferred_element_type=F32)
    zr, zi = delta[:, :half], delta[:, half:]

    row = lax.broadcasted_iota(jnp.int32, (S5_CH, 1), 0)
    is_ctx = r == 0
    pos = jnp.where(is_ctx, row & (S5_CTX_CH - 1), row)
    last = jnp.where(is_ctx, S5_CTX_CH - 1, S5_CH - 1)
    a = jnp.maximum(r - 1, 0)
    lat = (r > 0).astype(F32)
    h0r = h0r_ref[0, 0, pl.ds(a, 1), :] * lat
    h0i = h0i_ref[0, 0, pl.ds(a, 1), :] * lat
    first = (pos == 0) if d == 0 else (pos == last)
    ir, ii = _cmul(apr_ref[0, 0, 0:1, :], api_ref[0, 0, 0:1, :], h0r, h0i)
    zr = zr + jnp.where(first, ir, 0.0)
    zi = zi + jnp.where(first, ii, 0.0)
    for k in range(S5_NPOW):
        m = 1 << k
        akr, aki = apr_ref[0, 0, k:k + 1, :], api_ref[0, 0, k:k + 1, :]
        if d == 0:
            sr, si = pltpu.roll(zr, m, 0), pltpu.roll(zi, m, 0)
            valid = pos >= m
        else:
            sr, si = pltpu.roll(zr, S5_CH - m, 0), pltpu.roll(zi, S5_CH - m, 0)
            valid = pos <= last - m
        pr, pi_ = _cmul(akr, aki, sr, si)
        zr = zr + jnp.where(valid, pr, 0.0)
        zi = zi + jnp.where(valid, pi_, 0.0)
    if d == 0:
        sr, si = pltpu.roll(zr, 1, 0), pltpu.roll(zi, 1, 0)
    else:
        sr, si = pltpu.roll(zr, S5_CH - 1, 0), pltpu.roll(zi, S5_CH - 1, 0)
    sr = jnp.where(first, h0r, sr)
    si = jnp.where(first, h0i, si)
    s_in = jnp.concatenate([sr, si], axis=1).astype(BF16)
    y8 = (jnp.dot(x8, m8[...], preferred_element_type=F32)
          + jnp.dot(s_in, wout_ref[0, 0], preferred_element_type=F32))
    for t in range(S5_L):
        y_ref[0, pl.ds(tloc(t), S5_CH, stride=S5_L), :] = y8[:, 128 * t:128 * (t + 1)]

    @pl.when(r == 0)
    def _fin():
        off = S5_CTX_CH - 1 if d == 0 else 0
        for q in range(half // 128):
            zr_s[q] = zr[:, 128 * q:128 * (q + 1)]
            zi_s[q] = zi[:, 128 * q:128 * (q + 1)]
            fr_ref[0, 0, :, 128 * q:128 * (q + 1)] = zr_s[q, pl.ds(off, N_CTX_SEQ, stride=S5_CTX_CH), :]
            fi_ref[0, 0, :, 128 * q:128 * (q + 1)] = zi_s[q, pl.ds(off, N_CTX_SEQ, stride=S5_CTX_CH), :]


def _s5_scan_kernel(*refs):
    d = pl.program_id(1)
    r = pl.program_id(2)

    @pl.when(d == 0)
    def _():
        _s5_scan_body(0, r, *refs)

    @pl.when(d == 1)
    def _():
        _s5_scan_body(1, r, *refs)


def s5_scan(h, kk8, wst8, wout8, apr, api, h0r, h0i):
    half = S5_SW // 2
    return pl.pallas_call(
        _s5_scan_kernel,
        grid=(S5_NB, 2, N_GROUPS),
        in_specs=[pl.BlockSpec((GROUP_ROWS, 128), lambda j, d, r: (r, j)),
                  pl.BlockSpec((1, 1, 128, S5_XW), lambda j, d, r: (d, j, 0, 0)),
                  pl.BlockSpec((1, 1, S5_XW, S5_SW), lambda j, d, r: (d, j, 0, 0)),
                  pl.BlockSpec((1, 1, S5_SW, S5_XW), lambda j, d, r: (d, j, 0, 0)),
                  pl.BlockSpec((1, 1, S5_NPOW, half), lambda j, d, r: (d, j, 0, 0)),
                  pl.BlockSpec((1, 1, S5_NPOW, half), lambda j, d, r: (d, j, 0, 0)),
                  pl.BlockSpec((1, 1, N_LAT_SEQ, half), lambda j, d, r: (d, j, 0, 0)),
                  pl.BlockSpec((1, 1, N_LAT_SEQ, half), lambda j, d, r: (d, j, 0, 0))],
        out_specs=[pl.BlockSpec((1, GROUP_ROWS, 128), lambda j, d, r: (d, r, j)),
                   pl.BlockSpec((1, 1, N_CTX_SEQ, half), lambda j, d, r: (d, j, 0, 0)),
                   pl.BlockSpec((1, 1, N_CTX_SEQ, half), lambda j, d, r: (d, j, 0, 0))],
        out_shape=[jax.ShapeDtypeStruct((2, N_TOK, D), F32),
                   jax.ShapeDtypeStruct((2, S5_NB, N_CTX_SEQ, half), F32),
                   jax.ShapeDtypeStruct((2, S5_NB, N_CTX_SEQ, half), F32)],
        scratch_shapes=[pltpu.VMEM((S5_XW, S5_XW), BF16),
                        pltpu.VMEM((half // 128, S5_CH, 128), F32),
                        pltpu.VMEM((half // 128, S5_CH, 128), F32)],
        compiler_params=_cparams("arbitrary", "arbitrary", "arbitrary"),
        name="s5_scan",
    )(h, kk8, wst8, wout8, apr, api, h0r, h0i)


def _s5_glu_kernel(h_ref, y0_ref, y1_ref, dsk_ref, wa_ref, wb_ref, ba_ref, bb_ref, x_ref, g1_ref,
                   o_ref, yg_scr):
    @pl.when(pl.program_id(1) == 0)
    def _():
        y = dsk_ref[...] * h_ref[...] + y0_ref[0] + y1_ref[0]
        yg_scr[...] = jax.nn.gelu(y).astype(BF16)

    yg = yg_scr[...]
    a = jnp.dot(yg, wa_ref[...].astype(BF16), preferred_element_type=F32) + ba_ref[...]
    b = jnp.dot(yg, wb_ref[...].astype(BF16), preferred_element_type=F32) + bb_ref[...]
    o_ref[...] = x_ref[...] + g1_ref[0] * (a * jax.nn.sigmoid(b))


def s5_glu(h, y, d_skip, w_glu, b_glu, x, g1, tn=512):
    nj = D // tn
    b2 = b_glu.reshape(1, 2 * D)
    return pl.pallas_call(
        _s5_glu_kernel,
        grid=(N_TOK // TM, nj),
        in_specs=[pl.BlockSpec((TM, D), lambda i, j: (i, 0)),
                  pl.BlockSpec((1, TM, D), lambda i, j: (0, i, 0)),
                  pl.BlockSpec((1, TM, D), lambda i, j: (1, i, 0)),
                  pl.BlockSpec((1, D), lambda i, j: (0, 0)),
                  pl.BlockSpec((D, tn), lambda i, j: (0, j)),
                  pl.BlockSpec((D, tn), lambda i, j: (0, nj + j)),
                  pl.BlockSpec((1, tn), lambda i, j: (0, j)),
                  pl.BlockSpec((1, tn), lambda i, j: (0, nj + j)),
                  pl.BlockSpec((TM, tn), lambda i, j: (i, j)),
                  pl.BlockSpec((1, 1, tn), lambda i, j: (_group_of_tile(i), 0, j))],
        out_specs=pl.BlockSpec((TM, tn), lambda i, j: (i, j)),
        out_shape=jax.ShapeDtypeStruct((N_TOK, D), F32),
        scratch_shapes=[pltpu.VMEM((TM, D), BF16)],
        compiler_params=_cparams("parallel", "arbitrary"),
        name="s5_glu",
    )(h, y, y, d_skip.reshape(1, D), w_glu, w_glu, b2, b2, x, g1)


def s5_layer(x, mods, g_norm, st_re, st_im, lam_re, lam_im, log_dt, b_re, b_im, c_re, c_im,
             d_skip, w_glu, b_glu):
    sh1, sc1, g1 = mods
    h = normmod(x, g_norm, sc1, sh1)
    kk8, wst8, wout8, apr, api = s5_prepare(lam_re, lam_im, log_dt, b_re, b_im, c_re, c_im)

    def h0(st):
        return jnp.transpose(st.reshape(N_LAT_SEQ, 2, S5_NB, S5_GB * S5_P), (1, 2, 0, 3))

    y, fr, fi = s5_scan(h, kk8, wst8, wout8, apr, api, h0(st_re), h0(st_im))

    def fin(f):
        return jnp.transpose(f.reshape(2, S5_NB, N_CTX_SEQ, S5_GB, S5_P), (2, 0, 1, 3, 4)
                             ).reshape(N_CTX_SEQ, 2, S5_G, S5_P)

    x = s5_glu(h, y, d_skip, w_glu, b_glu, x, g1)
    return x, fin(fr), fin(fi)


CHUNK = 64
N_CHUNKS = N_TOK // CHUNK
CTX_CHUNKS = N_CTX_SEQ * CTX_LEN // CHUNK
CH_PER_CTX = CTX_LEN // CHUNK
CH_PER_LAT = LAT_LEN // CHUNK


def _chunk_of_step(d, c):
    return jnp.where(d == 0, c, N_CHUNKS - 1 - c)


def _seq_of_chunk(ce):
    return jnp.where(ce < CTX_CHUNKS, ce // CH_PER_CTX, N_CTX_SEQ + (ce - CTX_CHUNKS) // CH_PER_LAT)


def _chunk_flags(d, ce):
    is_ctx = ce < CTX_CHUNKS
    pos = jnp.where(is_ctx, ce % CH_PER_CTX, (ce - CTX_CHUNKS) % CH_PER_LAT)
    n = jnp.where(is_ctx, CH_PER_CTX, CH_PER_LAT)
    t_first, t_last = pos == 0, pos == n - 1
    if d == 0:
        return is_ctx, t_first, t_last
    return is_ctx, t_last, t_first


def _tri(d, shape, row_axis=0, col_axis=1):
    r = lax.broadcasted_iota(jnp.int32, shape, row_axis)
    c = lax.broadcasted_iota(jnp.int32, shape, col_axis)
    return (r >= c) if d == 0 else (r <= c)


HG_H = 8
HG_K = 128
HG_SAFE_SPAN = 60.0


def _hg_decay(d, layer, z_ref, lbl_ref):
    lg = lbl_ref[d]
    e = jnp.exp(lg - jnp.max(lg, axis=0, keepdims=True))
    sm = e / jnp.sum(e, axis=0, keepdims=True)
    lb = jnp.sum(sm[1:layer + 1], axis=0, keepdims=True)
    f = lb + (1.0 - lb) * jax.nn.sigmoid(z_ref[...])
    g = jnp.log(f)
    g_hi = g.astype(BF16)
    g_lo = (g - g_hi.astype(F32)).astype(BF16)
    tri = _tri(d, (CHUNK, CHUNK)).astype(BF16)
    cum = (jnp.dot(tri, g_hi, preferred_element_type=F32)
           + jnp.dot(tri, g_lo, preferred_element_type=F32))
    return 1.0 - f, cum


def _hg_main(d, layer, q_ref, v_ref, z_ref, lbl_ref, o_ref, st_scr, inter_scr):
    kk, cum = _hg_decay(d, layer, z_ref, lbl_ref)
    tri = _tri(d, (CHUNK, CHUNK))
    head, tot = (cum[0:1], cum[CHUNK - 1:CHUNK]) if d == 0 else (cum[CHUNK - 1:CHUNK], cum[0:1])
    mid = cum[CHUNK // 2 - 1:CHUNK // 2]
    q = q_ref[...]
    v = v_ref[...]
    qa = (q * jnp.exp(cum - mid)).astype(BF16)
    ka = (kk * jnp.exp(mid - cum)).astype(BF16)
    qs = (q * jnp.exp(cum)).astype(BF16)
    kd = (kk * jnp.exp(tot - cum)).astype(BF16)
    vb = v.astype(BF16)
    etot = jnp.exp(tot)
    nt = (((1,), (1,)), ((), ()))
    tn = (((0,), (0,)), ((), ()))
    for hd in range(HG_H):
        sl = slice(HG_K * hd, HG_K * (hd + 1))
        a = lax.dot_general(qa[:, sl], ka[:, sl], nt, preferred_element_type=F32)
        a = jnp.where(tri, a, 0.0).astype(BF16)
        st = st_scr[hd]
        inter = lax.dot_general(qs[:, sl], st.astype(BF16), nt, preferred_element_type=F32)
        inter_scr[:, sl] = inter
        o_ref[:, sl] = jnp.dot(a, vb[:, sl], preferred_element_type=F32) + inter
        st_scr[hd] = st * etot[:, sl] + lax.dot_general(vb[:, sl], kd[:, sl], tn,
                                                        preferred_element_type=F32)

    return jnp.max(jnp.maximum(head - mid, mid - tot))


def _hg_exact_intra(d, layer, q_ref, v_ref, z_ref, lbl_ref, o_ref, inter_scr, cum_scr, k_scr):
    kk, cum = _hg_decay(d, layer, z_ref, lbl_ref)
    cum_scr[...] = cum
    k_scr[...] = kk
    q = q_ref[...]
    c_idx = lax.broadcasted_iota(jnp.int32, (D, 128), 0) // HG_K
    h_idx = lax.broadcasted_iota(jnp.int32, (D, 128), 1)
    head_sum = (c_idx == h_idx).astype(F32)
    c_idx_t = lax.broadcasted_iota(jnp.int32, (128, D), 1) // HG_K
    h_idx_t = lax.broadcasted_iota(jnp.int32, (128, D), 0)
    head_bcast = (c_idx_t == h_idx_t).astype(F32)
    row = lax.broadcasted_iota(jnp.int32, (CHUNK, 1), 0)

    def source_row(s, acc):
        seen = (row >= s) if d == 0 else (row <= s)
        w = jnp.exp(jnp.where(seen, cum - cum_scr[pl.ds(s, 1), :], -jnp.inf))
        p = q * k_scr[pl.ds(s, 1), :] * w
        a_s = jnp.dot(p, head_sum, precision=HIGHEST, preferred_element_type=F32)
        a_s = jnp.dot(a_s, head_bcast, precision=HIGHEST, preferred_element_type=F32)
        return acc + a_s * v_ref[pl.ds(s, 1), :]

    intra = lax.fori_loop(0, CHUNK, source_row, jnp.zeros((CHUNK, D), F32))
    o_ref[...] = inter_scr[...] + intra


def _hg_scan_kernel(q0, v0, z0, q1, v1, z1, lbl_ref, s00, s01, o0, o1, fin0, fin1,
                    st0, st1, inter0, inter1, cum_scr, k_scr, *, layer):
    c = pl.program_id(0)
    dirs = ((0, q0, v0, z0, s00, o0, fin0, st0, inter0), (1, q1, v1, z1, s01, o1, fin1, st1, inter1))
    flags = [_chunk_flags(d, _chunk_of_step(d, c)) for d in (0, 1)]

    def set_state(st, value_fn):
        st[...] = value_fn()

    for (d, q, v, z, s0, o, fin, st, inter), (is_ctx, starts, ends) in zip(dirs, flags):
        pl.when(starts & is_ctx)(functools.partial(set_state, st, lambda: jnp.zeros((HG_H, HG_K, HG_K), F32)))
        pl.when(starts & jnp.logical_not(is_ctx))(functools.partial(set_state, st, lambda s0=s0: s0[0, 0]))
    spans = [_hg_main(d, layer, q, v, z, lbl_ref, o, st, inter)
             for (d, q, v, z, s0, o, fin, st, inter) in dirs]
    for (d, q, v, z, s0, o, fin, st, inter), span in zip(dirs, spans):
        pl.when(span > HG_SAFE_SPAN)(functools.partial(
            _hg_exact_intra, d, layer, q, v, z, lbl_ref, o, inter, cum_scr, k_scr))

    def write_final(fin, st):
        for hd in range(HG_H):
            fin[0, hd] = st[hd].T

    for (d, q, v, z, s0, o, fin, st, inter), (is_ctx, starts, ends) in zip(dirs, flags):
        pl.when(ends)(functools.partial(write_final, fin, st))


def hg_scan(proj, lb_logits, s0t, layer):
    def tok(d, col):
        return lambda c: (_chunk_of_step(d, c), col)

    def lat_idx(d):
        return lambda c: (d, jnp.maximum(_seq_of_chunk(_chunk_of_step(d, c)) - N_CTX_SEQ, 0), 0, 0, 0)

    def fin_idx(d):
        return lambda c: (jnp.minimum(_seq_of_chunk(_chunk_of_step(d, c)), N_CTX_SEQ), 0, 0, 0)

    state = (HG_H, HG_K, HG_K)
    return pl.pallas_call(
        functools.partial(_hg_scan_kernel, layer=layer),
        grid=(N_CHUNKS,),
        in_specs=[pl.BlockSpec((CHUNK, D), tok(0, 0)), pl.BlockSpec((CHUNK, D), tok(0, 1)),
                  pl.BlockSpec((CHUNK, D), tok(0, 3)),
                  pl.BlockSpec((CHUNK, D), tok(1, 0)), pl.BlockSpec((CHUNK, D), tok(1, 1)),
                  pl.BlockSpec((CHUNK, D), tok(1, 4)),
                  pl.BlockSpec((2, DEPTH, D), lambda c: (0, 0, 0)),
                  pl.BlockSpec((1, 1) + state, lat_idx(0)), pl.BlockSpec((1, 1) + state, lat_idx(1))],
        out_specs=[pl.BlockSpec((CHUNK, D), tok(0, 0)), pl.BlockSpec((CHUNK, D), tok(1, 0)),
                   pl.BlockSpec((1,) + state, fin_idx(0)), pl.BlockSpec((1,) + state, fin_idx(1))],
        out_shape=[jax.ShapeDtypeStruct((N_TOK, D), F32), jax.ShapeDtypeStruct((N_TOK, D), F32),
                   jax.ShapeDtypeStruct((N_CTX_SEQ + 1,) + state, F32),
                   jax.ShapeDtypeStruct((N_CTX_SEQ + 1,) + state, F32)],
        scratch_shapes=[pltpu.VMEM(state, F32), pltpu.VMEM(state, F32),
                        pltpu.VMEM((CHUNK, D), F32), pltpu.VMEM((CHUNK, D), F32),
                        pltpu.VMEM((CHUNK, D), F32), pltpu.VMEM((CHUNK, D), F32)],
        compiler_params=_cparams("arbitrary"),
        name="hg_scan",
    )(proj, proj, proj, proj, proj, proj, lb_logits, s0t, s0t)


def _hg_out_kernel(o0_ref, o1_ref, gate_ref, gn_ref, w_ref, x_ref, g1_ref, out_ref, on_scr):
    @pl.when(pl.program_id(1) == 0)
    def _():
        for hd in range(HG_H):
            sl = slice(HG_K * hd, HG_K * (hd + 1))
            o = o0_ref[:, sl] + o1_ref[:, sl]
            o = o * lax.rsqrt(jnp.mean(o * o, axis=-1, keepdims=True) + NORM_EPS) * gn_ref[...]
            on_scr[:, sl] = (o * _silu(gate_ref[:, sl])).astype(BF16)

    out_ref[...] = x_ref[...] + g1_ref[0] * jnp.dot(on_scr[...], w_ref[...].astype(BF16),
                                                    preferred_element_type=F32)


def hg_out(o_fwd, o_bwd, proj, g_norm, w_o, x, g1, tn=512):
    return pl.pallas_call(
        _hg_out_kernel,
        grid=(N_TOK // TM, D // tn),
        in_specs=[pl.BlockSpec((TM, D), lambda i, j: (i, 0)),
                  pl.BlockSpec((TM, D), lambda i, j: (i, 0)),
                  pl.BlockSpec((TM, D), lambda i, j: (i, 2)),
                  pl.BlockSpec((1, HG_K), lambda i, j: (0, 0)),
                  pl.BlockSpec((D, tn), lambda i, j: (0, j)),
                  pl.BlockSpec((TM, tn), lambda i, j: (i, j)),
                  pl.BlockSpec((1, 1, tn), lambda i, j: (_group_of_tile(i), 0, j))],
        out_specs=pl.BlockSpec((TM, tn), lambda i, j: (i, j)),
        out_shape=jax.ShapeDtypeStruct((N_TOK, D), F32),
        scratch_shapes=[pltpu.VMEM((TM, D), BF16)],
        compiler_params=_cparams("parallel", "arbitrary"),
        name="hg_out",
    )(o_fwd, o_bwd, proj, g_norm.reshape(1, HG_K), w_o, x, g1)


def hgrn_layer(x, mods, g_mix, state, layer, lb_logits, w_qig, w_f, b_f, g_norm, w_o):
    sh1, sc1, g1 = mods
    w5 = jnp.concatenate([w_qig, w_f[0], w_f[1]], axis=1)
    b5 = jnp.concatenate([jnp.zeros((3 * D,), F32), b_f[0], b_f[1]])
    proj = nm_matmul(x, g_mix, sc1, sh1, w5, b5, 5 * D, name="hg_proj")
    s0t = jnp.transpose(state, (1, 0, 2, 4, 3))
    o_fwd, o_bwd, fin_fwd, fin_bwd = hg_scan(proj, lb_logits, s0t, layer)
    x = hg_out(o_fwd, o_bwd, proj, g_norm, w_o, x, g1)
    return x, jnp.stack([fin_fwd[:N_CTX_SEQ], fin_bwd[:N_CTX_SEQ]], axis=1)


SSD_INNER = 2 * D
SSD_HEADS = 32
SSD_P = 64
SSD_NG = 4
SSD_N = 128
SSD_XBC = SSD_INNER + 2 * SSD_NG * SSD_N
SSD_ZX = SSD_INNER + SSD_XBC
SSD_CONV = 5
CONV_TM = 256
CONV_HALO = 8


def _ssd_conv_kernel(cur_ref, prev_ref, next_ref, w_ref, b_ref, o_ref, ext):
    i = pl.program_id(0)
    n_ctx_tiles = N_CTX_SEQ * CTX_LEN // CONV_TM
    per_lat = LAT_LEN // CONV_TM
    is_ctx = i < n_ctx_tiles
    k = (i - n_ctx_tiles) % per_lat
    seq_start = is_ctx | (k == 0)
    seq_end = is_ctx | (k == per_lat - 1)
    ext[0:CONV_HALO] = jnp.where(seq_start, 0.0, prev_ref[...])
    ext[CONV_HALO:CONV_HALO + CONV_TM] = cur_ref[...]
    ext[CONV_HALO + CONV_TM:] = jnp.where(seq_end, 0.0, next_ref[...])
    acc = jnp.broadcast_to(b_ref[...], (CONV_TM, D))
    for t in range(SSD_CONV):
        acc = acc + w_ref[t:t + 1, :] * ext[pl.ds(CONV_HALO - SSD_CONV // 2 + t, CONV_TM), :]
    o_ref[...] = _silu(acc)


def ssd_conv(zx, conv_w, conv_b):
    nrb = N_TOK // CONV_HALO
    rpt = CONV_TM // CONV_HALO
    c0 = SSD_INNER // D
    return pl.pallas_call(
        _ssd_conv_kernel,
        grid=(N_TOK // CONV_TM, SSD_XBC // D),
        in_specs=[pl.BlockSpec((CONV_TM, D), lambda i, j: (i, c0 + j)),
                  pl.BlockSpec((CONV_HALO, D), lambda i, j: (jnp.maximum(i * rpt - 1, 0), c0 + j)),
                  pl.BlockSpec((CONV_HALO, D), lambda i, j: (jnp.minimum((i + 1) * rpt, nrb - 1), c0 + j)),
                  pl.BlockSpec((SSD_CONV, D), lambda i, j: (0, j)),
                  pl.BlockSpec((1, D), lambda i, j: (0, j))],
        out_specs=pl.BlockSpec((CONV_TM, D), lambda i, j: (i, j)),
        out_shape=jax.ShapeDtypeStruct((N_TOK, SSD_XBC), F32),
        scratch_shapes=[pltpu.VMEM((CONV_TM + 2 * CONV_HALO, D), F32)],
        compiler_params=_cparams("parallel", "parallel"),
        name="ssd_conv",
    )(zx, zx, zx, conv_w, conv_b.reshape(1, SSD_XBC))


def _ssd_main(d, xlo_ref, xhi_ref, bc_ref, dtr_ref, dtb_ref, alog_ref, y_ref, ht_scr):
    xr = dtr_ref[...] + dtb_ref[...]
    dt = jnp.maximum(xr, 0.0) + jnp.log(1.0 + jnp.exp(-jnp.abs(xr)))
    dta = dt * (-jnp.exp(alog_ref[...]))
    tri = _tri(d, (CHUNK, CHUNK))
    cum = jnp.dot(tri.astype(F32), dta, precision=HIGHEST, preferred_element_type=F32)
    r = lax.broadcasted_iota(jnp.int32, (CHUNK, 2 * CHUNK), 0)
    cc = lax.broadcasted_iota(jnp.int32, (CHUNK, 2 * CHUNK), 1)
    lo_half = cc < CHUNK
    ccm = jnp.where(lo_half, cc, cc - CHUNK)
    trit = (ccm >= r) if d == 0 else (ccm <= r)
    tn = (((0,), (0,)), ((), ()))
    nt = (((1,), (1,)), ((), ()))
    cumt_lo = lax.dot_general(dta, (trit & lo_half).astype(F32), tn, precision=HIGHEST,
                              preferred_element_type=F32)
    cumt_hi = lax.dot_general(dta, (trit & jnp.logical_not(lo_half)).astype(F32), tn,
                              precision=HIGHEST, preferred_element_type=F32)
    lane = lax.broadcasted_iota(jnp.int32, (CHUNK, 2 * SSD_P), 1)
    first_head = lane < SSD_P
    tri2 = (r >= ccm) if d == 0 else (r <= ccm)
    bc = bc_ref[...]
    for gq in range(SSD_NG):
        bg = bc[:, SSD_N * gq:SSD_N * (gq + 1)].astype(BF16)
        cg = bc[:, SSD_NG * SSD_N + SSD_N * gq:SSD_NG * SSD_N + SSD_N * (gq + 1)].astype(BF16)
        cb2 = lax.dot_general(cg, jnp.concatenate([bg, bg], axis=0), nt,
                              preferred_element_type=F32)
        for pp in range(4 * gq, 4 * gq + 4):
            h1 = SSD_HEADS * d + 2 * pp
            colp = jnp.where(first_head, cum[:, h1:h1 + 1], cum[:, h1 + 1:h1 + 2])
            rowp = cumt_lo[h1:h1 + 1, :] + cumt_hi[h1 + 1:h1 + 2, :]
            lmat = jnp.exp(jnp.where(tri2, colp - rowp, -jnp.inf))
            dtp = jnp.where(first_head, dt[:, h1:h1 + 1], dt[:, h1 + 1:h1 + 2])
            xref = xlo_ref if pp < 8 else xhi_ref
            c0 = 128 * (pp % 8)
            xdt = xref[:, c0:c0 + 128] * dtp
            rhs = jnp.concatenate([jnp.where(first_head, xdt, 0.0),
                                   jnp.where(first_head, 0.0, xdt)], axis=0).astype(BF16)
            y = jnp.dot((cb2 * lmat).astype(BF16), rhs, preferred_element_type=F32)
            ht = ht_scr[:, 128 * pp:128 * (pp + 1)]
            y = y + jnp.dot(cg, ht.astype(BF16), preferred_element_type=F32) * jnp.exp(colp)
            y_ref[:, 128 * pp:128 * (pp + 1)] = y
            totp = colp[CHUNK - 1:CHUNK] if d == 0 else colp[0:1]
            xw = (xdt * jnp.exp(totp - colp)).astype(BF16)
            ht_scr[:, 128 * pp:128 * (pp + 1)] = (
                ht * jnp.exp(totp) + lax.dot_general(bg, xw, tn, preferred_element_type=F32))


def _ssd_scan_kernel(xlo0, xhi0, bc0, dtr0, xlo1, xhi1, bc1, dtr1, dtb_ref, alog_ref, h00, h01,
                     y0, y1, fin0, fin1, ht0, ht1):
    c = pl.program_id(0)
    dirs = ((0, xlo0, xhi0, bc0, dtr0, h00, y0, fin0, ht0), (1, xlo1, xhi1, bc1, dtr1, h01, y1, fin1, ht1))
    flags = [_chunk_flags(d, _chunk_of_step(d, c)) for d in (0, 1)]

    def set_state(ht, value_fn):
        ht[...] = value_fn()

    for (d, xlo, xhi, bc, dtr, h0, y, fin, ht), (is_ctx, starts, ends) in zip(dirs, flags):
        pl.when(starts & is_ctx)(functools.partial(
            set_state, ht, lambda: jnp.zeros((SSD_N, SSD_HEADS * SSD_P), F32)))
        pl.when(starts & jnp.logical_not(is_ctx))(functools.partial(set_state, ht, lambda h0=h0: h0[0, 0]))
    for (d, xlo, xhi, bc, dtr, h0, y, fin, ht) in dirs:
        _ssd_main(d, xlo, xhi, bc, dtr, dtb_ref, alog_ref, y, ht)
    def write_final(fin, ht):
        fin[0] = ht[...]

    for (d, xlo, xhi, bc, dtr, h0, y, fin, ht), (is_ctx, starts, ends) in zip(dirs, flags):
        pl.when(ends)(functools.partial(write_final, fin, ht))


def ssd_scan(xbc, dtr, dt_bias, a_log, h0t):
    nh2 = 2 * SSD_HEADS
    hp = SSD_HEADS * SSD_P

    def tok(d, col):
        return lambda c: (_chunk_of_step(d, c), col)

    def lat_idx(d):
        return lambda c: (d, jnp.maximum(_seq_of_chunk(_chunk_of_step(d, c)) - N_CTX_SEQ, 0), 0, 0)

    def fin_idx(d):
        return lambda c: (jnp.minimum(_seq_of_chunk(_chunk_of_step(d, c)), N_CTX_SEQ), 0, 0)

    def tok_specs(d):
        return [pl.BlockSpec((CHUNK, D), tok(d, 0)), pl.BlockSpec((CHUNK, D), tok(d, 1)),
                pl.BlockSpec((CHUNK, D), tok(d, 2)), pl.BlockSpec((CHUNK, nh2), tok(d, 0))]

    return pl.pallas_call(
        _ssd_scan_kernel,
        grid=(N_CHUNKS,),
        in_specs=tok_specs(0) + tok_specs(1) + [
            pl.BlockSpec((1, nh2), lambda c: (0, 0)), pl.BlockSpec((1, nh2), lambda c: (0, 0)),
            pl.BlockSpec((1, 1, SSD_N, hp), lat_idx(0)), pl.BlockSpec((1, 1, SSD_N, hp), lat_idx(1))],
        out_specs=[pl.BlockSpec((CHUNK, hp), tok(0, 0)), pl.BlockSpec((CHUNK, hp), tok(1, 0)),
                   pl.BlockSpec((1, SSD_N, hp), fin_idx(0)), pl.BlockSpec((1, SSD_N, hp), fin_idx(1))],
        out_shape=[jax.ShapeDtypeStruct((N_TOK, hp), F32), jax.ShapeDtypeStruct((N_TOK, hp), F32),
                   jax.ShapeDtypeStruct((N_CTX_SEQ + 1, SSD_N, hp), F32),
                   jax.ShapeDtypeStruct((N_CTX_SEQ + 1, SSD_N, hp), F32)],
        scratch_shapes=[pltpu.VMEM((SSD_N, hp), F32), pltpu.VMEM((SSD_N, hp), F32)],
        compiler_params=_cparams("arbitrary"),
        name="ssd_scan",
    )(xbc, xbc, xbc, dtr, xbc, xbc, xbc, dtr, dt_bias.reshape(1, nh2), a_log.reshape(1, nh2), h0t, h0t)


SSD_OUT_TM = 512


def _ssd_out_kernel(xlo_ref, xhi_ref, zlo_ref, zhi_ref, y0_ref, y1_ref, dsk_ref, gn_ref, w_ref,
                    x_ref, g1_ref, out_ref, yn_scr):
    @pl.when(pl.program_id(1) == 0)
    def _():
        halves = []
        ss = jnp.zeros((SSD_OUT_TM, 1), F32)
        for k, (xr, zr) in enumerate(((xlo_ref, zlo_ref), (xhi_ref, zhi_ref))):
            sl = slice(D * k, D * (k + 1))
            y = dsk_ref[:, sl] * xr[...] + y0_ref[:, sl] + y1_ref[:, sl]
            y = y * _silu(zr[...])
            ss = ss + jnp.sum(y * y, axis=-1, keepdims=True)
            halves.append(y)
        scale = lax.rsqrt(ss / SSD_INNER + NORM_EPS)
        for k, y in enumerate(halves):
            sl = slice(D * k, D * (k + 1))
            yn_scr[:, sl] = (y * scale * gn_ref[:, sl]).astype(BF16)

    out_ref[...] = x_ref[...] + g1_ref[0] * jnp.dot(yn_scr[...], w_ref[...].astype(BF16),
                                                    preferred_element_type=F32)


def ssd_out(xbc, zx, y_fwd, y_bwd, d_skip_cols, g_norm, w_out, x, g1, tn=512):
    tm = SSD_OUT_TM
    tpg = GROUP_ROWS // tm
    return pl.pallas_call(
        _ssd_out_kernel,
        grid=(N_TOK // tm, D // tn),
        in_specs=[pl.BlockSpec((tm, D), lambda i, j: (i, 0)),
                  pl.BlockSpec((tm, D), lambda i, j: (i, 1)),
                  pl.BlockSpec((tm, D), lambda i, j: (i, 0)),
                  pl.BlockSpec((tm, D), lambda i, j: (i, 1)),
                  pl.BlockSpec((tm, SSD_INNER), lambda i, j: (i, 0)),
                  pl.BlockSpec((tm, SSD_INNER), lambda i, j: (i, 0)),
                  pl.BlockSpec((1, SSD_INNER), lambda i, j: (0, 0)),
                  pl.BlockSpec((1, SSD_INNER), lambda i, j: (0, 0)),
                  pl.BlockSpec((SSD_INNER, tn), lambda i, j: (0, j)),
                  pl.BlockSpec((tm, tn), lambda i, j: (i, j)),
                  pl.BlockSpec((1, 1, tn), lambda i, j: (i // tpg, 0, j))],
        out_specs=pl.BlockSpec((tm, tn), lambda i, j: (i, j)),
        out_shape=jax.ShapeDtypeStruct((N_TOK, D), F32),
        scratch_shapes=[pltpu.VMEM((tm, SSD_INNER), BF16)],
        compiler_params=_cparams("parallel", "arbitrary"),
        name="ssd_out",
    )(xbc, xbc, zx, zx, y_fwd, y_bwd, d_skip_cols, g_norm.reshape(1, SSD_INNER), w_out, x, g1)


def ssd_layer(x, mods, g_mix, state, w_in, conv_w, conv_b, dt_bias, a_log, d_skip, g_norm, w_out):
    sh1, sc1, g1 = mods
    zx = nm_matmul(x, g_mix, sc1, sh1, w_in, jnp.zeros((SSD_ZX,), F32), SSD_ZX, name="ssd_proj")
    nh2 = 2 * SSD_HEADS
    dtr = nm_matmul(x, g_mix, sc1, sh1, w_in[:, SSD_ZX:], jnp.zeros((nh2,), F32), nh2, tn=nh2,
                    name="ssd_proj_dt")
    xbc = ssd_conv(zx, conv_w, conv_b)
    h0t = jnp.transpose(state, (1, 0, 4, 2, 3)).reshape(2, N_LAT_SEQ, SSD_N, SSD_HEADS * SSD_P)
    y_fwd, y_bwd, fin_fwd, fin_bwd = ssd_scan(xbc, dtr, dt_bias, a_log, h0t)
    dcols = jnp.repeat(d_skip, SSD_P).reshape(1, SSD_INNER)
    x = ssd_out(xbc, zx, y_fwd, y_bwd, dcols, g_norm, w_out, x, g1)
    fin = jnp.stack([fin_fwd[:N_CTX_SEQ], fin_bwd[:N_CTX_SEQ]], axis=1)
    fin = jnp.transpose(fin.reshape(N_CTX_SEQ, 2, SSD_N, SSD_HEADS, SSD_P), (0, 1, 3, 4, 2))
    return x, fin


N_EXP = 16
FF = 2 * D
CAP_CTX = 2 * CTX_LEN // N_EXP
CAP_LAT = 2 * LAT_LEN // N_EXP
SLOTS_PER_GROUP = 512
SLOTS = N_GROUPS * SLOTS_PER_GROUP


def _router_kernel(x_ref, g_ref, sc_ref, sh_ref, wt_ref, h_ref, aff_ref):
    h = _normmod(x_ref[...], g_ref[...], sc_ref[0], sh_ref[0])
    h_ref[...] = h.astype(BF16)
    logits = lax.dot_general(wt_ref[...], h, (((1,), (1,)), ((), ())), precision=HIGHEST,
                             preferred_element_type=F32)
    e = jnp.exp(logits - jnp.max(logits, axis=0, keepdims=True))
    aff_ref[...] = e / jnp.sum(e, axis=0, keepdims=True)


def moe_route(x, g, sc, sh, w_router_t):
    return pl.pallas_call(
        _router_kernel,
        grid=(N_TOK // TM,),
        in_specs=[pl.BlockSpec((TM, D), lambda i: (i, 0)),
                  pl.BlockSpec((1, D), lambda i: (0, 0)),
                  pl.BlockSpec((1, 1, D), lambda i: (_group_of_tile(i), 0, 0)),
                  pl.BlockSpec((1, 1, D), lambda i: (_group_of_tile(i), 0, 0)),
                  pl.BlockSpec((N_EXP, D), lambda i: (0, 0))],
        out_specs=[pl.BlockSpec((TM, D), lambda i: (i, 0)),
                   pl.BlockSpec((N_EXP, TM), lambda i: (0, i))],
        out_shape=[jax.ShapeDtypeStruct((N_TOK, D), BF16),
                   jax.ShapeDtypeStruct((N_EXP, N_TOK), F32)],
        compiler_params=_cparams("parallel"),
        name="moe_router",
    )(x, g.reshape(1, D), sc, sh, w_router_t)


def _lane_prefix_excl(m):
    s, t = m.shape
    r = lax.broadcasted_iota(jnp.int32, (128, 128), 0)
    c = lax.broadcasted_iota(jnp.int32, (128, 128), 1)
    upper = (r <= c).astype(BF16)
    run = jnp.zeros((s, 1), F32)
    out = []
    for k in range(t // 128):
        blk = m[:, 128 * k:128 * (k + 1)]
        inc = jnp.dot(blk.astype(BF16), upper, preferred_element_type=F32) + run
        out.append(inc - blk)
        run = inc[:, 127:128]
    return jnp.concatenate(out, axis=1)


def _select_kernel(a_ref, off_ref, slot_ref, base_ref, *, cap):
    bits = pltpu.bitcast(a_ref[...], jnp.int32)
    s = bits.shape[0]
    capf = float(cap)

    def body(_, lohi):
        lo, hi = lohi
        mid = lo + ((hi - lo + 1) >> 1)
        cnt = jnp.sum((bits >= mid).astype(F32), axis=1, keepdims=True)
        ok = cnt >= capf
        return jnp.where(ok, mid, lo), jnp.where(ok, hi, mid - 1)

    lo0 = jnp.zeros((s, 1), jnp.int32)
    hi0 = jnp.full((s, 1), 0x7F800000, jnp.int32)
    thr, _ = lax.fori_loop(0, 31, body, (lo0, hi0))
    gt = (bits > thr).astype(F32)
    eq = (bits == thr).astype(F32)
    need = capf - jnp.sum(gt, axis=1, keepdims=True)
    sel = gt + eq * (_lane_prefix_excl(eq) < need).astype(F32)
    slot = _lane_prefix_excl(sel) + off_ref[...]
    slot_ref[...] = jnp.where(sel > 0.0, slot, -1.0).astype(jnp.int32)
    t = bits.shape[1]
    tok = lax.broadcasted_iota(jnp.int32, (t, 128), 0)
    tile = lax.broadcasted_iota(jnp.int32, (t, 128), 1)
    ahead = (tok < tile * TOK_TILE).astype(BF16)
    base_ref[...] = jnp.dot(sel.astype(BF16), ahead, preferred_element_type=F32).astype(jnp.int32)


def moe_select(aff, off, cap):
    s, t = aff.shape
    return pl.pallas_call(
        functools.partial(_select_kernel, cap=cap),
        grid=(1,),
        in_specs=[pl.BlockSpec((s, t), lambda i: (0, 0)),
                  pl.BlockSpec((s, 1), lambda i: (0, 0))],
        out_specs=[pl.BlockSpec((s, t), lambda i: (0, 0)),
                   pl.BlockSpec((s, 128), lambda i: (0, 0))],
        out_shape=[jax.ShapeDtypeStruct((s, t), jnp.int32),
                   jax.ShapeDtypeStruct((s, 128), jnp.int32)],
        compiler_params=_cparams("arbitrary"),
        name="moe_select",
    )(aff, off)


TOK_TILE = 256
TILES = GROUP_ROWS // TOK_TILE
CNT_STRIDE = TILES + 1
SLOT_BLK = 128
TILES_PER_BLK = TILES * SLOT_BLK // SLOTS_PER_GROUP
GATHER_MARGIN = 2
BAND_MARGIN = 48
SLOT_SHIFT = 4
SLOT_SPLIT = 1 << SLOT_SHIFT


def _gather_kernel(cnt_ref, slot_ref, aff_ref, h_ref, xs_ref, gs_ref):
    base = (pl.program_id(0) * N_EXP + pl.program_id(1)) * CNT_STRIDE

    def gather(k, toks):
        n = toks.stop - toks.start
        s = lax.broadcasted_iota(jnp.int32, (SLOT_BLK, n), 0) + SLOT_BLK * k
        oh = s == slot_ref[0, 0, :, toks]
        rows = slice(SLOT_BLK * k, SLOT_BLK * (k + 1))
        xs_ref[0, rows, :] = jnp.dot(oh.astype(BF16), h_ref[toks, :],
                                     preferred_element_type=F32).astype(BF16)
        gs_ref[0, rows, :] = jnp.sum(jnp.where(oh, aff_ref[0, :, toks], 0.0), axis=1, keepdims=True)

    for k in range(SLOTS_PER_GROUP // SLOT_BLK):
        lo = max(TILES_PER_BLK * k - GATHER_MARGIN, 0)
        hi = min(TILES_PER_BLK * (k + 1) + GATHER_MARGIN, TILES)
        in_band = (cnt_ref[base + lo] <= SLOT_BLK * k) & (cnt_ref[base + hi] >= SLOT_BLK * (k + 1))

        @pl.when(in_band)
        def _():
            gather(k, slice(TOK_TILE * lo, TOK_TILE * hi))

        @pl.when(jnp.logical_not(in_band))
        def _():
            gather(k, slice(0, GROUP_ROWS))


def moe_gather(cnt, slot, aff3, h2):
    return pl.pallas_call(
        _gather_kernel,
        grid_spec=pltpu.PrefetchScalarGridSpec(
            num_scalar_prefetch=1,
            grid=(N_GROUPS, N_EXP),
            in_specs=[pl.BlockSpec((1, 1, 1, GROUP_ROWS), lambda r, e, c: (r, e, 0, 0)),
                      pl.BlockSpec((1, 1, GROUP_ROWS), lambda r, e, c: (e, 0, r)),
                      pl.BlockSpec((GROUP_ROWS, D), lambda r, e, c: (r, 0))],
            out_specs=[pl.BlockSpec((1, SLOTS_PER_GROUP, D), lambda r, e, c: (e, r, 0)),
                       pl.BlockSpec((1, SLOTS_PER_GROUP, 1), lambda r, e, c: (e, r, 0))]),
        out_shape=[jax.ShapeDtypeStruct((N_EXP, SLOTS, D), BF16),
                   jax.ShapeDtypeStruct((N_EXP, SLOTS, 1), F32)],
        compiler_params=_cparams("parallel", "parallel"),
        name="moe_gather",
    )(cnt, slot, aff3, h2)


FF_TILE = 512


def _ffn_kernel(xs_ref, gs_ref, wg_ref, wu_ref, wd_ref, ys_ref, acc):
    f = pl.program_id(1)
    x = xs_ref[0]
    g = jnp.dot(x, wg_ref[0, 0].astype(BF16), preferred_element_type=F32)
    u = jnp.dot(x, wu_ref[0, 0].astype(BF16), preferred_element_type=F32)
    hid = (_silu(g) * u).astype(BF16)
    contrib = jnp.dot(hid, wd_ref[0, 0].astype(BF16), preferred_element_type=F32)

    @pl.when(f == 0)
    def _():
        acc[...] = contrib

    @pl.when(f > 0)
    def _():
        acc[...] += contrib

    @pl.when(f == FF // FF_TILE - 1)
    def _():
        ys_ref[0] = (acc[...] * gs_ref[0]).astype(BF16)


def moe_ffn(xs, gs, w_gate, w_up, w_down, layer):
    return pl.pallas_call(
        _ffn_kernel,
        grid=(N_EXP, FF // FF_TILE),
        in_specs=[pl.BlockSpec((1, SLOTS, D), lambda e, f: (e, 0, 0)),
                  pl.BlockSpec((1, SLOTS, 1), lambda e, f: (e, 0, 0)),
                  pl.BlockSpec((1, 1, D, FF_TILE), lambda e, f: (layer, e, 0, f)),
                  pl.BlockSpec((1, 1, D, FF_TILE), lambda e, f: (layer, e, 0, f)),
                  pl.BlockSpec((1, 1, FF_TILE, D), lambda e, f: (layer, e, f, 0))],
        out_specs=pl.BlockSpec((1, SLOTS, D), lambda e, f: (e, 0, 0)),
        out_shape=jax.ShapeDtypeStruct((N_EXP, SLOTS, D), BF16),
        scratch_shapes=[pltpu.VMEM((SLOTS, D), F32)],
        compiler_params=_cparams("parallel", "arbitrary"),
        name="moe_ffn",
    )(xs, gs, w_gate, w_up, w_down)


SCAT_TN = 512


def _scatter_kernel(cnt_ref, slot_ref, rep_ref, ys_ref, x_ref, g2_ref, o_ref):
    r, j = pl.program_id(0), pl.program_id(2)
    slot = slot_ref[0]
    spread = (SLOT_SPLIT * jnp.dot((slot >> SLOT_SHIFT).astype(BF16), rep_ref[...],
                                   preferred_element_type=F32)
              + jnp.dot((slot & (SLOT_SPLIT - 1)).astype(BF16), rep_ref[...],
                        preferred_element_type=F32))
    cap_tile = SLOTS_PER_GROUP // TILES
    start = pl.multiple_of(jnp.clip(cap_tile * j - BAND_MARGIN, 0, SLOTS_PER_GROUP - SLOT_BLK), 16)
    in_band = True
    for e in range(N_EXP):
        base = (r * N_EXP + e) * CNT_STRIDE + j
        in_band = in_band & (cnt_ref[base] >= start) & (cnt_ref[base + 1] <= start + SLOT_BLK)

    def scatter(first, n_blocks):
        lane = lax.broadcasted_iota(jnp.int32, (TOK_TILE, SLOT_BLK), 1).astype(F32)
        acc = jnp.zeros((TOK_TILE, SCAT_TN), F32)
        for b in range(n_blocks):
            lo = first + SLOT_BLK * b
            want = lane + jnp.asarray(lo).astype(F32)
            for e in range(N_EXP):
                oh = (spread[:, SLOT_BLK * e:SLOT_BLK * (e + 1)] == want).astype(BF16)
                acc = acc + jnp.dot(oh, ys_ref[e, pl.ds(lo, SLOT_BLK), :], preferred_element_type=F32)
        o_ref[...] = x_ref[...] + g2_ref[0] * acc

    @pl.when(in_band)
    def _():
        scatter(start, 1)

    @pl.when(jnp.logical_not(in_band))
    def _():
        scatter(0, SLOTS_PER_GROUP // SLOT_BLK)


def moe_scatter(cnt, slot_t, ys, x, g2):
    rep = (jnp.arange(128)[:, None] == jnp.arange(N_EXP * SLOT_BLK)[None, :] // SLOT_BLK).astype(BF16)
    return pl.pallas_call(
        _scatter_kernel,
        grid_spec=pltpu.PrefetchScalarGridSpec(
            num_scalar_prefetch=1,
            grid=(N_GROUPS, D // SCAT_TN, TILES),
            in_specs=[pl.BlockSpec((1, TOK_TILE, 128), lambda r, c, j, n: (r, j, 0)),
                      pl.BlockSpec((128, N_EXP * SLOT_BLK), lambda r, c, j, n: (0, 0)),
                      pl.BlockSpec((N_EXP, SLOTS_PER_GROUP, SCAT_TN), lambda r, c, j, n: (0, r, c)),
                      pl.BlockSpec((TOK_TILE, SCAT_TN), lambda r, c, j, n: (r * TILES + j, c)),
                      pl.BlockSpec((1, 1, SCAT_TN), lambda r, c, j, n: (r, 0, c))],
            out_specs=pl.BlockSpec((TOK_TILE, SCAT_TN), lambda r, c, j, n: (r * TILES + j, c))),
        out_shape=jax.ShapeDtypeStruct((N_TOK, D), F32),
        compiler_params=_cparams("parallel", "parallel", "arbitrary"),
        name="moe_scatter",
    )(cnt, slot_t, rep, ys, x, g2)


def moe_layer(x, mods, g_ffn, layer, w_router, w_gate, w_up, w_down):
    sh2, sc2, g2 = mods
    h2, aff = moe_route(x, g_ffn, sc2, sh2, w_router.T)
    aff_ctx = aff[:, :GROUP_ROWS].reshape(N_EXP * N_CTX_SEQ, CTX_LEN)
    off_ctx = jnp.tile(jnp.arange(N_CTX_SEQ, dtype=F32) * CAP_CTX, N_EXP).reshape(-1, 1)
    slot_ctx, _ = moe_select(aff_ctx, off_ctx, CAP_CTX)
    aff_lat = jnp.transpose(aff[:, GROUP_ROWS:].reshape(N_EXP, N_LAT_SEQ, LAT_LEN), (1, 0, 2)
                            ).reshape(N_LAT_SEQ * N_EXP, LAT_LEN)
    slot_lat, base_lat = moe_select(aff_lat, jnp.zeros((N_LAT_SEQ * N_EXP, 1), F32), CAP_LAT)
    slot = jnp.concatenate([slot_ctx.reshape(1, N_EXP, GROUP_ROWS),
                            slot_lat.reshape(N_LAT_SEQ, N_EXP, LAT_LEN)], axis=0)
    slot_t = jnp.pad(jnp.transpose(slot, (0, 2, 1)), ((0, 0), (0, 0), (0, 128 - N_EXP)),
                     constant_values=-1)
    slot = slot.reshape(N_GROUPS, N_EXP, 1, GROUP_ROWS)
    cnt_ctx = jnp.broadcast_to(jnp.arange(CNT_STRIDE, dtype=jnp.int32) * CAP_CTX, (1, N_EXP, CNT_STRIDE))
    cnt = jnp.concatenate([cnt_ctx, base_lat[:, :CNT_STRIDE].reshape(N_LAT_SEQ, N_EXP, CNT_STRIDE)],
                          axis=0).reshape(-1)
    xs, gs = moe_gather(cnt, slot, aff.reshape(N_EXP, 1, N_TOK), h2)
    ys = moe_ffn(xs, gs, w_gate, w_up, w_down, layer)
    return moe_scatter(cnt, slot_t, ys, x, g2)


def _final_norm_kernel(x_ref, g_ref, ctx_ref, lat_ref):
    x = x_ref[...]
    y = x * lax.rsqrt(jnp.mean(x * x, axis=-1, keepdims=True) + NORM_EPS) * g_ref[...]
    is_ctx = pl.program_id(0) < TILES_PER_GROUP

    @pl.when(is_ctx)
    def _():
        ctx_ref[...] = y

    @pl.when(jnp.logical_not(is_ctx))
    def _():
        lat_ref[...] = y


def final_norm(x, g):
    t = TILES_PER_GROUP
    return pl.pallas_call(
        _final_norm_kernel,
        grid=(N_TOK // TM,),
        in_specs=[pl.BlockSpec((TM, D), lambda i: (i, 0)),
                  pl.BlockSpec((1, D), lambda i: (0, 0))],
        out_specs=[pl.BlockSpec((TM, D), lambda i: (jnp.minimum(i, t - 1), 0)),
                   pl.BlockSpec((TM, D), lambda i: (jnp.maximum(i - t, 0), 0))],
        out_shape=[jax.ShapeDtypeStruct((GROUP_ROWS, D), F32),
                   jax.ShapeDtypeStruct((N_TOK - GROUP_ROWS, D), F32)],
        compiler_params=_cparams("arbitrary"),
        name="final_norm",
    )(x, g.reshape(1, D))


def _grid_pos_embed():
    rows = LAT_LEN // GRID_W
    quarter = D // 4
    omega = 1.0 / (10000.0 ** (jnp.arange(quarter, dtype=F32) / quarter))
    r = jnp.arange(rows, dtype=F32)[:, None] * omega
    cl = jnp.arange(GRID_W, dtype=F32)[:, None] * omega
    emb_r = jnp.concatenate([jnp.sin(r), jnp.cos(r)], axis=-1)
    emb_c = jnp.concatenate([jnp.sin(cl), jnp.cos(cl)], axis=-1)
    emb = jnp.concatenate([jnp.broadcast_to(emb_r[:, None], (rows, GRID_W, D // 2)),
                           jnp.broadcast_to(emb_c[None], (rows, GRID_W, D // 2))], axis=-1)
    return emb.reshape(LAT_LEN, D)


def kernel(x_prompt, x_sample, state_s5_re, state_s5_im, state_hgrn, state_ssd, c, c_ctx, w_ada, b_ada, norm_mix, norm_ffn, norm_final, s5_lam_re, s5_lam_im, s5_log_dt, s5_b_re, s5_b_im, s5_c_re, s5_c_im, s5_d, s5_w_glu, s5_b_glu, hg_w_qig, hg_w_f, hg_b_f, hg_lb_logits, hg_norm, hg_w_o, ssd_w_in, ssd_conv_w, ssd_conv_b, ssd_dt_bias, ssd_a_log, ssd_d, ssd_norm, ssd_w_out, moe_router, moe_w_gate, moe_w_up, moe_w_down):
    cond8 = jnp.concatenate([c_ctx[None], c, jnp.zeros((5, D), F32)], axis=0)
    mod = ada_mod(cond8, w_ada, b_ada)
    mods = jnp.transpose(mod.reshape(DEPTH, 8, 6, D)[:, :3], (0, 2, 1, 3)).reshape(DEPTH, 6, 3, 1, D)
    x = embed_tokens(x_prompt.reshape(-1, D), x_sample.reshape(-1, D), _grid_pos_embed())
    s5_re, s5_im, hg_fin, ssd_fin = [], [], [], []
    for i in range(DEPTH):
        mix_mods = (mods[i, 0], mods[i, 1], mods[i, 2])
        kind, j = i % 3, i // 3
        if kind == 0:
            x, fr, fi = s5_layer(x, mix_mods, norm_mix[i], state_s5_re[:, j], state_s5_im[:, j],
                                 s5_lam_re[j], s5_lam_im[j], s5_log_dt[j], s5_b_re[j], s5_b_im[j],
                                 s5_c_re[j], s5_c_im[j], s5_d[j], s5_w_glu[j], s5_b_glu[j])
            s5_re.append(fr)
            s5_im.append(fi)
        elif kind == 1:
            x, fh = hgrn_layer(x, mix_mods, norm_mix[i], state_hgrn[:, j], i, hg_lb_logits,
                               hg_w_qig[j], hg_w_f[j], hg_b_f[j], hg_norm[j], hg_w_o[j])
            hg_fin.append(fh)
        else:
            x, fs = ssd_layer(x, mix_mods, norm_mix[i], state_ssd[:, j], ssd_w_in[j], ssd_conv_w[j],
                              ssd_conv_b[j], ssd_dt_bias[j], ssd_a_log[j], ssd_d[j], ssd_norm[j],
                              ssd_w_out[j])
            ssd_fin.append(fs)
        x = moe_layer(x, (mods[i, 3], mods[i, 4], mods[i, 5]), norm_ffn[i], i, moe_router[i],
                      moe_w_gate, moe_w_up, moe_w_down)
    y_ctx, y_lat = final_norm(x, norm_final)
    return (y_ctx.reshape(N_CTX_SEQ, CTX_LEN, D), y_lat.reshape(N_LAT_SEQ, LAT_LEN, D),
            jnp.stack(s5_re, axis=1), jnp.stack(s5_im, axis=1),
            jnp.stack(hg_fin, axis=1), jnp.stack(ssd_fin, axis=1))
```

```python
import functools
import math

import jax
import jax.numpy as jnp
from jax import lax
from jax.experimental import pallas as pl
from jax.experimental.pallas import tpu as pltpu

F32 = jnp.float32
BF16 = jnp.bfloat16
HIGHEST = lax.Precision.HIGHEST

D = 1024
DEPTH = 4
N_CTX_SEQ = 16
CTX_LEN = 256
N_LAT_SEQ = 2
LAT_LEN = 4096
GROUP_ROWS = 4096
N_GROUPS = 3
N_TOK = N_GROUPS * GROUP_ROWS
N_SEQ = N_CTX_SEQ + N_LAT_SEQ
NORM_EPS = 1e-6
GRID_W = 64

VMEM_LIMIT_BYTES = 56 * 1024 * 1024


def _cparams(*sem):
    return pltpu.CompilerParams(dimension_semantics=sem, vmem_limit_bytes=VMEM_LIMIT_BYTES)


def _silu(x):
    return x * jax.nn.sigmoid(x)


def _normmod(x, g, sc, sh):
    ms = jnp.mean(x * x, axis=-1, keepdims=True)
    return x * lax.rsqrt(ms + NORM_EPS) * g * (1.0 + sc) + sh


def _cmul(ar, ai, br, bi):
    return ar * br - ai * bi, ar * bi + ai * br


def _mod_kernel(c_ref, w_ref, b_ref, o_ref):
    o_ref[0] = jnp.dot(_silu(c_ref[...]), w_ref[0], precision=HIGHEST,
                       preferred_element_type=F32) + b_ref[0]


def ada_mod(cond8, w_ada, b_ada):
    tn = 1536
    return pl.pallas_call(
        _mod_kernel,
        grid=(DEPTH, 6 * D // tn),
        in_specs=[pl.BlockSpec((8, D), lambda i, j: (0, 0)),
                  pl.BlockSpec((1, D, tn), lambda i, j: (i, 0, j)),
                  pl.BlockSpec((1, 1, tn), lambda i, j: (i, 0, j))],
        out_specs=pl.BlockSpec((1, 8, tn), lambda i, j: (i, 0, j)),
        out_shape=jax.ShapeDtypeStruct((DEPTH, 8, 6 * D), F32),
        compiler_params=_cparams("parallel", "parallel"),
        name="ada_mod",
    )(cond8, w_ada, b_ada.reshape(DEPTH, 1, 6 * D))


def _embed_kernel(xp_ref, xs_ref, pos_ref, o_ref):
    r = pl.program_id(0)

    @pl.when(r == 0)
    def _():
        o_ref[...] = xp_ref[...]

    @pl.when(r > 0)
    def _():
        o_ref[...] = xs_ref[...] + pos_ref[...]


def embed_tokens(xp, xs, pos):
    tm = 1024
    nt = GROUP_ROWS // tm
    return pl.pallas_call(
        _embed_kernel,
        grid=(N_GROUPS, nt),
        in_specs=[pl.BlockSpec((tm, D), lambda r, i: (jnp.where(r == 0, i, 0), 0)),
                  pl.BlockSpec((tm, D), lambda r, i: (jnp.where(r == 0, 0, (r - 1) * nt + i), 0)),
                  pl.BlockSpec((tm, D), lambda r, i: (i, 0))],
        out_specs=pl.BlockSpec((tm, D), lambda r, i: (r * nt + i, 0)),
        out_shape=jax.ShapeDtypeStruct((N_TOK, D), F32),
        compiler_params=_cparams("parallel", "parallel"),
        name="embed_tokens",
    )(xp, xs, pos)


TM = 1024
TILES_PER_GROUP = GROUP_ROWS // TM


def _group_of_tile(i):
    return i // TILES_PER_GROUP


def _normmod_kernel(x_ref, g_ref, sc_ref, sh_ref, o_ref):
    o_ref[...] = _normmod(x_ref[...], g_ref[...], sc_ref[0], sh_ref[0])


def normmod(x, g, sc, sh):
    return pl.pallas_call(
        _normmod_kernel,
        grid=(N_TOK // TM,),
        in_specs=[pl.BlockSpec((TM, D), lambda i: (i, 0)),
                  pl.BlockSpec((1, D), lambda i: (0, 0)),
                  pl.BlockSpec((1, 1, D), lambda i: (_group_of_tile(i), 0, 0)),
                  pl.BlockSpec((1, 1, D), lambda i: (_group_of_tile(i), 0, 0))],
        out_specs=pl.BlockSpec((TM, D), lambda i: (i, 0)),
        out_shape=jax.ShapeDtypeStruct((N_TOK, D), F32),
        compiler_params=_cparams("parallel"),
        name="normmod",
    )(x, g.reshape(1, D), sc, sh)


def _nm_matmul_kernel(x_ref, g_ref, sc_ref, sh_ref, w_ref, b_ref, o_ref, h_scr):
    @pl.when(pl.program_id(1) == 0)
    def _():
        h_scr[...] = _normmod(x_ref[...], g_ref[...], sc_ref[0], sh_ref[0]).astype(BF16)

    o_ref[...] = jnp.dot(h_scr[...], w_ref[...].astype(BF16),
                         preferred_element_type=F32) + b_ref[...]


def nm_matmul(x, g, sc, sh, w, b, n_out, tn=1024, name="nm_matmul"):
    return pl.pallas_call(
        _nm_matmul_kernel,
        grid=(N_TOK // TM, n_out // tn),
        in_specs=[pl.BlockSpec((TM, D), lambda i, j: (i, 0)),
                  pl.BlockSpec((1, D), lambda i, j: (0, 0)),
                  pl.BlockSpec((1, 1, D), lambda i, j: (_group_of_tile(i), 0, 0)),
                  pl.BlockSpec((1, 1, D), lambda i, j: (_group_of_tile(i), 0, 0)),
                  pl.BlockSpec((D, tn), lambda i, j: (0, j)),
                  pl.BlockSpec((1, tn), lambda i, j: (0, j))],
        out_specs=pl.BlockSpec((TM, tn), lambda i, j: (i, j)),
        out_shape=jax.ShapeDtypeStruct((N_TOK, n_out), F32),
        scratch_shapes=[pltpu.VMEM((TM, D), BF16)],
        compiler_params=_cparams("parallel", "arbitrary"),
        name=name,
    )(x, g.reshape(1, D), sc, sh, w, b.reshape(1, -1))


S5_G = 64
S5_H = 16
S5_P = 64
S5_L = 16
S5_GB = 8
S5_NB = S5_G // S5_GB
S5_CH = GROUP_ROWS // S5_L
S5_CTX_CH = CTX_LEN // S5_L
S5_NPOW = 8
S5_XW = S5_L * 128
S5_SW = 2 * S5_GB * S5_P


S5_LH = S5_L * S5_H
S5_NPWR = 24


def _s5_prep_kernel(lr_ref, li_ref, ldt_ref, btr_ref, bti_ref, cxr_ref, cxi_ref,
                    sel0_ref, sel1_ref, exp_ref,
                    kk_ref, wst_ref, wout_ref, apr_ref, api_ref):
    dt = jnp.exp(ldt_ref[0])
    lam_r, lam_i = lr_ref[0], li_ref[0]
    ar, ai = lam_r * dt, lam_i * dt
    pw = lax.broadcasted_iota(jnp.int32, (S5_GB, S5_NPWR, 2 * S5_P), 1).astype(F32)
    ep = jnp.exp(pw * ar)
    pwr, pwi = ep * jnp.cos(pw * ai), ep * jnp.sin(pw * ai)
    den = lam_r * lam_r + lam_i * lam_i
    nr, ni = pwr[:, 1:2] - 1.0, pwi[:, 1:2]
    beta_r = (nr * lam_r + ni * lam_i) / den
    beta_i = (ni * lam_r - nr * lam_i) / den
    bbr, bbi = _cmul(beta_r, beta_i, btr_ref[0], bti_ref[0])

    lane = lax.broadcasted_iota(jnp.int32, (S5_H, 2 * S5_P), 1)
    wst_ref[0, 0] = jnp.zeros((S5_XW, S5_SW), BF16)
    half = S5_SW // 2
    for g in range(S5_GB):
        mine = (lane >= S5_P) if g % 2 else (lane < S5_P)
        col = 128 * (g // 2)
        for s in range(S5_L):
            k = S5_L - 1 - s
            wr, wi = _cmul(pwr[g, k:k + 1], pwi[g, k:k + 1], bbr[g], bbi[g])
            rows = slice(128 * s + S5_H * g, 128 * s + S5_H * (g + 1))
            wst_ref[0, 0, rows, col:col + 128] = jnp.where(mine, wr, 0.0).astype(BF16)
            wst_ref[0, 0, rows, half + col:half + col + 128] = jnp.where(mine, wi, 0.0).astype(BF16)

    lane1 = lax.broadcasted_iota(jnp.int32, (1, 2 * S5_P), 1)

    def group_lanes(a):
        return jnp.concatenate([jnp.where(lane1 < S5_P, a[2 * q], a[2 * q + 1])
                                for q in range(S5_GB // 2)], axis=1)

    pr, pi_ = pwr[:, S5_L:S5_L + 1], pwi[:, S5_L:S5_L + 1]
    for k in range(S5_NPOW):
        apr_ref[0, 0, k:k + 1, :] = group_lanes(pr)
        api_ref[0, 0, k:k + 1, :] = group_lanes(pi_)
        pr, pi_ = _cmul(pr, pi_, pr, pi_)

    tn = (((0,), (0,)), ((), ()))
    kks, wre, wim = [], [], []
    for g in range(S5_GB):
        pg_r, pg_i = pwr[g, :, :S5_P], pwi[g, :, :S5_P]
        cr, ci = cxr_ref[0, g], cxi_ref[0, g]

        def c_times_pow(sel):
            er = lax.dot_general(pg_r, sel, tn, precision=HIGHEST, preferred_element_type=F32)
            ei = lax.dot_general(pg_i, sel, tn, precision=HIGHEST, preferred_element_type=F32)
            return _cmul(cr, ci, er, ei)

        k_r, k_i = c_times_pow(sel0_ref[...])
        kks.append(jnp.dot(bbr[g, :, :S5_P], k_r, precision=HIGHEST, preferred_element_type=F32)
                   - jnp.dot(bbi[g, :, :S5_P], k_i, precision=HIGHEST, preferred_element_type=F32))
        o_r, o_i = c_times_pow(sel1_ref[...])
        wre.append(o_r)
        wim.append(-o_i)
    glane = (lax.broadcasted_iota(jnp.int32, (1, S5_XW), 1) % 128) // S5_H

    def spread(parts, rows_per_group):
        a = jnp.concatenate(parts, axis=0).astype(BF16)
        a = jnp.dot(a, exp_ref[...], preferred_element_type=F32)
        grow = lax.broadcasted_iota(jnp.int32, (a.shape[0], 1), 0) // rows_per_group % S5_GB
        return jnp.where(grow == glane, a, 0.0).astype(BF16)

    wout_ref[0, 0] = spread(wre + wim, S5_P)
    kk_ref[0, 0] = spread(kks, S5_H)


def s5_prepare(lam_re, lam_im, log_dt, b_re, b_im, c_re, c_im):
    half = S5_SW // 2
    lr = jnp.tile(lam_re.reshape(2, S5_G, 1, S5_P), (1, 1, 1, 2))
    li = jnp.tile(lam_im.reshape(2, S5_G, 1, S5_P), (1, 1, 1, 2))
    ldt = log_dt.reshape(2, S5_G, 1, 1)
    btr = jnp.tile(jnp.swapaxes(b_re, 2, 3), (1, 1, 1, 2))
    bti = jnp.tile(jnp.swapaxes(b_im, 2, 3), (1, 1, 1, 2))
    cxr = jnp.tile(jnp.swapaxes(c_re, 2, 3), (1, 1, 1, S5_L))
    cxi = jnp.tile(jnp.swapaxes(c_im, 2, 3), (1, 1, 1, S5_L))
    k = jnp.arange(S5_NPWR)[:, None]
    t = (jnp.arange(S5_LH) // S5_H)[None, :]
    sel0 = (k == t).astype(F32)
    sel1 = (k == t + 1).astype(F32)
    src = jnp.arange(S5_LH)[:, None]
    dst = jnp.arange(S5_XW)[None, :]
    expand = ((src // S5_H == dst // 128) & (src % S5_H == dst % S5_H)).astype(BF16)

    def spec(*tail):
        return pl.BlockSpec((1, S5_GB) + tail, lambda d, j: (d, j) + (0,) * len(tail))

    def const(shape):
        return pl.BlockSpec(shape, lambda d, j: (0,) * len(shape))

    def blk(*tail):
        return pl.BlockSpec((1, 1) + tail, lambda d, j: (d, j) + (0,) * len(tail))

    return pl.pallas_call(
        _s5_prep_kernel,
        grid=(2, S5_NB),
        in_specs=[spec(1, 2 * S5_P), spec(1, 2 * S5_P), spec(1, 1),
                  spec(S5_H, 2 * S5_P), spec(S5_H, 2 * S5_P), spec(S5_P, S5_LH), spec(S5_P, S5_LH),
                  const((S5_NPWR, S5_LH)), const((S5_NPWR, S5_LH)), const((S5_LH, S5_XW))],
        out_specs=[blk(128, S5_XW), blk(S5_XW, S5_SW), blk(S5_SW, S5_XW),
                   blk(S5_NPOW, half), blk(S5_NPOW, half)],
        out_shape=[jax.ShapeDtypeStruct((2, S5_NB, 128, S5_XW), BF16),
                   jax.ShapeDtypeStruct((2, S5_NB, S5_XW, S5_SW), BF16),
                   jax.ShapeDtypeStruct((2, S5_NB, S5_SW, S5_XW), BF16),
                   jax.ShapeDtypeStruct((2, S5_NB, S5_NPOW, half), F32),
                   jax.ShapeDtypeStruct((2, S5_NB, S5_NPOW, half), F32)],
        compiler_params=_cparams("parallel", "parallel"),
        name="s5_prepare",
    )(lr, li, ldt, btr, bti, cxr, cxi, sel0, sel1, expand)


def _s5_scan_body(d, r, h_ref, kk_ref, wst_ref, wout_ref, apr_ref, api_ref, h0r_ref, h0i_ref,
                  y_ref, fr_ref, fi_ref, m8, zr_s, zi_s):
    half = S5_SW // 2

    @pl.when(r == 0)
    def _build():
        for s in range(S5_L):
            if s:
                m8[128 * s:128 * (s + 1), 0:128 * s] = jnp.zeros((128, 128 * s), BF16)
            m8[128 * s:128 * (s + 1), 128 * s:] = kk_ref[0, 0, :, :S5_XW - 128 * s]

    def tloc(s):
        return s if d == 0 else S5_L - 1 - s

    slabs = [h_ref[pl.ds(tloc(s), S5_CH, stride=S5_L), :] for s in range(S5_L)]
    x8 = jnp.concatenate(slabs, axis=1).astype(BF16)
    delta = jnp.dot(x8, wst_ref[0, 0], preferred_element_type=F32)
    zr, zi = delta[:, :half], delta[:, half:]

    row = lax.broadcasted_iota(jnp.int32, (S5_CH, 1), 0)
    is_ctx = r == 0
    pos = jnp.where(is_ctx, row & (S5_CTX_CH - 1), row)
    last = jnp.where(is_ctx, S5_CTX_CH - 1, S5_CH - 1)
    a = jnp.maximum(r - 1, 0)
    lat = (r > 0).astype(F32)
    h0r = h0r_ref[0, 0, pl.ds(a, 1), :] * lat
    h0i = h0i_ref[0, 0, pl.ds(a, 1), :] * lat
    first = (pos == 0) if d == 0 else (pos == last)
    ir, ii = _cmul(apr_ref[0, 0, 0:1, :], api_ref[0, 0, 0:1, :], h0r, h0i)
    zr = zr + jnp.where(first, ir, 0.0)
    zi = zi + jnp.where(first, ii, 0.0)
    for k in range(S5_NPOW):
        m = 1 << k
        akr, aki = apr_ref[0, 0, k:k + 1, :], api_ref[0, 0, k:k + 1, :]
        if d == 0:
            sr, si = pltpu.roll(zr, m, 0), pltpu.roll(zi, m, 0)
            valid = pos >= m
        else:
            sr, si = pltpu.roll(zr, S5_CH - m, 0), pltpu.roll(zi, S5_CH - m, 0)
            valid = pos <= last - m
        pr, pi_ = _cmul(akr, aki, sr, si)
        zr = zr + jnp.where(valid, pr, 0.0)
        zi = zi + jnp.where(valid, pi_, 0.0)
    if d == 0:
        sr, si = pltpu.roll(zr, 1, 0), pltpu.roll(zi, 1, 0)
    else:
        sr, si = pltpu.roll(zr, S5_CH - 1, 0), pltpu.roll(zi, S5_CH - 1, 0)
    sr = jnp.where(first, h0r, sr)
    si = jnp.where(first, h0i, si)
    s_in = jnp.concatenate([sr, si], axis=1).astype(BF16)
    y8 = (jnp.dot(x8, m8[...], preferred_element_type=F32)
          + jnp.dot(s_in, wout_ref[0, 0], preferred_element_type=F32))
    for t in range(S5_L):
        y_ref[0, pl.ds(tloc(t), S5_CH, stride=S5_L), :] = y8[:, 128 * t:128 * (t + 1)]

    @pl.when(r == 0)
    def _fin():
        off = S5_CTX_CH - 1 if d == 0 else 0
        for q in range(half // 128):
            zr_s[q] = zr[:, 128 * q:128 * (q + 1)]
            zi_s[q] = zi[:, 128 * q:128 * (q + 1)]
            fr_ref[0, 0, :, 128 * q:128 * (q + 1)] = zr_s[q, pl.ds(off, N_CTX_SEQ, stride=S5_CTX_CH), :]
            fi_ref[0, 0, :, 128 * q:128 * (q + 1)] = zi_s[q, pl.ds(off, N_CTX_SEQ, stride=S5_CTX_CH), :]


def _s5_scan_kernel(*refs):
    d = pl.program_id(1)
    r = pl.program_id(2)

    @pl.when(d == 0)
    def _():
        _s5_scan_body(0, r, *refs)

    @pl.when(d == 1)
    def _():
        _s5_scan_body(1, r, *refs)


def s5_scan(h, kk8, wst8, wout8, apr, api, h0r, h0i):
    half = S5_SW // 2
    return pl.pallas_call(
        _s5_scan_kernel,
        grid=(S5_NB, 2, N_GROUPS),
        in_specs=[pl.BlockSpec((GROUP_ROWS, 128), lambda j, d, r: (r, j)),
                  pl.BlockSpec((1, 1, 128, S5_XW), lambda j, d, r: (d, j, 0, 0)),
                  pl.BlockSpec((1, 1, S5_XW, S5_SW), lambda j, d, r: (d, j, 0, 0)),
                  pl.BlockSpec((1, 1, S5_SW, S5_XW), lambda j, d, r: (d, j, 0, 0)),
                  pl.BlockSpec((1, 1, S5_NPOW, half), lambda j, d, r: (d, j, 0, 0)),
                  pl.BlockSpec((1, 1, S5_NPOW, half), lambda j, d, r: (d, j, 0, 0)),
                  pl.BlockSpec((1, 1, N_LAT_SEQ, half), lambda j, d, r: (d, j, 0, 0)),
                  pl.BlockSpec((1, 1, N_LAT_SEQ, half), lambda j, d, r: (d, j, 0, 0))],
        out_specs=[pl.BlockSpec((1, GROUP_ROWS, 128), lambda j, d, r: (d, r, j)),
                   pl.BlockSpec((1, 1, N_CTX_SEQ, half), lambda j, d, r: (d, j, 0, 0)),
                   pl.BlockSpec((1, 1, N_CTX_SEQ, half), lambda j, d, r: (d, j, 0, 0))],
        out_shape=[jax.ShapeDtypeStruct((2, N_TOK, D), F32),
                   jax.ShapeDtypeStruct((2, S5_NB, N_CTX_SEQ, half), F32),
                   jax.ShapeDtypeStruct((2, S5_NB, N_CTX_SEQ, half), F32)],
        scratch_shapes=[pltpu.VMEM((S5_XW, S5_XW), BF16),
                        pltpu.VMEM((half // 128, S5_CH, 128), F32),
                        pltpu.VMEM((half // 128, S5_CH, 128), F32)],
        compiler_params=_cparams("arbitrary", "arbitrary", "arbitrary"),
        name="s5_scan",
    )(h, kk8, wst8, wout8, apr, api, h0r, h0i)


def _s5_glu_kernel(h_ref, y0_ref, y1_ref, dsk_ref, wa_ref, wb_ref, ba_ref, bb_ref, x_ref, g1_ref,
                   o_ref, yg_scr):
    @pl.when(pl.program_id(1) == 0)
    def _():
        y = dsk_ref[...] * h_ref[...] + y0_ref[0] + y1_ref[0]
        yg_scr[...] = jax.nn.gelu(y).astype(BF16)

    yg = yg_scr[...]
    a = jnp.dot(yg, wa_ref[...].astype(BF16), preferred_element_type=F32) + ba_ref[...]
    b = jnp.dot(yg, wb_ref[...].astype(BF16), preferred_element_type=F32) + bb_ref[...]
    o_ref[...] = x_ref[...] + g1_ref[0] * (a * jax.nn.sigmoid(b))


def s5_glu(h, y, d_skip, w_glu, b_glu, x, g1, tn=512):
    nj = D // tn
    b2 = b_glu.reshape(1, 2 * D)
    return pl.pallas_call(
        _s5_glu_kernel,
        grid=(N_TOK // TM, nj),
        in_specs=[pl.BlockSpec((TM, D), lambda i, j: (i, 0)),
                  pl.BlockSpec((1, TM, D), lambda i, j: (0, i, 0)),
                  pl.BlockSpec((1, TM, D), lambda i, j: (1, i, 0)),
                  pl.BlockSpec((1, D), lambda i, j: (0, 0)),
                  pl.BlockSpec((D, tn), lambda i, j: (0, j)),
                  pl.BlockSpec((D, tn), lambda i, j: (0, nj + j)),
                  pl.BlockSpec((1, tn), lambda i, j: (0, j)),
                  pl.BlockSpec((1, tn), lambda i, j: (0, nj + j)),
                  pl.BlockSpec((TM, tn), lambda i, j: (i, j)),
                  pl.BlockSpec((1, 1, tn), lambda i, j: (_group_of_tile(i), 0, j))],
        out_specs=pl.BlockSpec((TM, tn), lambda i, j: (i, j)),
        out_shape=jax.ShapeDtypeStruct((N_TOK, D), F32),
        scratch_shapes=[pltpu.VMEM((TM, D), BF16)],
        compiler_params=_cparams("parallel", "arbitrary"),
        name="s5_glu",
    )(h, y, y, d_skip.reshape(1, D), w_glu, w_glu, b2, b2, x, g1)


def s5_layer(x, mods, g_norm, st_re, st_im, lam_re, lam_im, log_dt, b_re, b_im, c_re, c_im,
             d_skip, w_glu, b_glu):
    sh1, sc1, g1 = mods
    h = normmod(x, g_norm, sc1, sh1)
    kk8, wst8, wout8, apr, api = s5_prepare(lam_re, lam_im, log_dt, b_re, b_im, c_re, c_im)

    def h0(st):
        return jnp.transpose(st.reshape(N_LAT_SEQ, 2, S5_NB, S5_GB * S5_P), (1, 2, 0, 3))

    y, fr, fi = s5_scan(h, kk8, wst8, wout8, apr, api, h0(st_re), h0(st_im))

    def fin(f):
        return jnp.transpose(f.reshape(2, S5_NB, N_CTX_SEQ, S5_GB, S5_P), (2, 0, 1, 3, 4)
                             ).reshape(N_CTX_SEQ, 2, S5_G, S5_P)

    x = s5_glu(h, y, d_skip, w_glu, b_glu, x, g1)
    return x, fin(fr), fin(fi)


CHUNK = 64
N_CHUNKS = N_TOK // CHUNK
CTX_CHUNKS = N_CTX_SEQ * CTX_LEN // CHUNK
CH_PER_CTX = CTX_LEN // CHUNK
CH_PER_LAT = LAT_LEN // CHUNK


def _chunk_of_step(d, c):
    return jnp.where(d == 0, c, N_CHUNKS - 1 - c)


def _seq_of_chunk(ce):
    return jnp.where(ce < CTX_CHUNKS, ce // CH_PER_CTX, N_CTX_SEQ + (ce - CTX_CHUNKS) // CH_PER_LAT)


def _chunk_flags(d, ce):
    is_ctx = ce < CTX_CHUNKS
    pos = jnp.where(is_ctx, ce % CH_PER_CTX, (ce - CTX_CHUNKS) % CH_PER_LAT)
    n = jnp.where(is_ctx, CH_PER_CTX, CH_PER_LAT)
    t_first, t_last = pos == 0, pos == n - 1
    if d == 0:
        return is_ctx, t_first, t_last
    return is_ctx, t_last, t_first


def _tri(d, shape, row_axis=0, col_axis=1):
    r = lax.broadcasted_iota(jnp.int32, shape, row_axis)
    c = lax.broadcasted_iota(jnp.int32, shape, col_axis)
    return (r >= c) if d == 0 else (r <= c)


HG_H = 8
HG_K = 128
HG_SAFE_SPAN = 60.0


def _hg_decay(d, layer, z_ref, lbl_ref):
    lg = lbl_ref[d]
    e = jnp.exp(lg - jnp.max(lg, axis=0, keepdims=True))
    sm = e / jnp.sum(e, axis=0, keepdims=True)
    lb = jnp.sum(sm[1:layer + 1], axis=0, keepdims=True)
    f = lb + (1.0 - lb) * jax.nn.sigmoid(z_ref[...])
    g = jnp.log(f)
    g_hi = g.astype(BF16)
    g_lo = (g - g_hi.astype(F32)).astype(BF16)
    tri = _tri(d, (CHUNK, CHUNK)).astype(BF16)
    cum = (jnp.dot(tri, g_hi, preferred_element_type=F32)
           + jnp.dot(tri, g_lo, preferred_element_type=F32))
    return 1.0 - f, cum


def _hg_main(d, layer, q_ref, v_ref, z_ref, lbl_ref, o_ref, st_scr, inter_scr):
    kk, cum = _hg_decay(d, layer, z_ref, lbl_ref)
    tri = _tri(d, (CHUNK, CHUNK))
    head, tot = (cum[0:1], cum[CHUNK - 1:CHUNK]) if d == 0 else (cum[CHUNK - 1:CHUNK], cum[0:1])
    mid = cum[CHUNK // 2 - 1:CHUNK // 2]
    q = q_ref[...]
    v = v_ref[...]
    qa = (q * jnp.exp(cum - mid)).astype(BF16)
    ka = (kk * jnp.exp(mid - cum)).astype(BF16)
    qs = (q * jnp.exp(cum)).astype(BF16)
    kd = (kk * jnp.exp(tot - cum)).astype(BF16)
    vb = v.astype(BF16)
    etot = jnp.exp(tot)
    nt = (((1,), (1,)), ((), ()))
    tn = (((0,), (0,)), ((), ()))
    for hd in range(HG_H):
        sl = slice(HG_K * hd, HG_K * (hd + 1))
        a = lax.dot_general(qa[:, sl], ka[:, sl], nt, preferred_element_type=F32)
        a = jnp.where(tri, a, 0.0).astype(BF16)
        st = st_scr[hd]
        inter = lax.dot_general(qs[:, sl], st.astype(BF16), nt, preferred_element_type=F32)
        inter_scr[:, sl] = inter
        o_ref[:, sl] = jnp.dot(a, vb[:, sl], preferred_element_type=F32) + inter
        st_scr[hd] = st * etot[:, sl] + lax.dot_general(vb[:, sl], kd[:, sl], tn,
                                                        preferred_element_type=F32)

    return jnp.max(jnp.maximum(head - mid, mid - tot))


def _hg_exact_intra(d, layer, q_ref, v_ref, z_ref, lbl_ref, o_ref, inter_scr, cum_scr, k_scr):
    kk, cum = _hg_decay(d, layer, z_ref, lbl_ref)
    cum_scr[...] = cum
    k_scr[...] = kk
    q = q_ref[...]
    c_idx = lax.broadcasted_iota(jnp.int32, (D, 128), 0) // HG_K
    h_idx = lax.broadcasted_iota(jnp.int32, (D, 128), 1)
    head_sum = (c_idx == h_idx).astype(F32)
    c_idx_t = lax.broadcasted_iota(jnp.int32, (128, D), 1) // HG_K
    h_idx_t = lax.broadcasted_iota(jnp.int32, (128, D), 0)
    head_bcast = (c_idx_t == h_idx_t).astype(F32)
    row = lax.broadcasted_iota(jnp.int32, (CHUNK, 1), 0)

    def source_row(s, acc):
        seen = (row >= s) if d == 0 else (row <= s)
        w = jnp.exp(jnp.where(seen, cum - cum_scr[pl.ds(s, 1), :], -jnp.inf))
        p = q * k_scr[pl.ds(s, 1), :] * w
        a_s = jnp.dot(p, head_sum, precision=HIGHEST, preferred_element_type=F32)
        a_s = jnp.dot(a_s, head_bcast, precision=HIGHEST, preferred_element_type=F32)
        return acc + a_s * v_ref[pl.ds(s, 1), :]

    intra = lax.fori_loop(0, CHUNK, source_row, jnp.zeros((CHUNK, D), F32))
    o_ref[...] = inter_scr[...] + intra


def _hg_scan_kernel(q0, v0, z0, q1, v1, z1, lbl_ref, s00, s01, o0, o1, fin0, fin1,
                    st0, st1, inter0, inter1, cum_scr, k_scr, *, layer):
    c = pl.program_id(0)
    dirs = ((0, q0, v0, z0, s00, o0, fin0, st0, inter0), (1, q1, v1, z1, s01, o1, fin1, st1, inter1))
    flags = [_chunk_flags(d, _chunk_of_step(d, c)) for d in (0, 1)]

    def set_state(st, value_fn):
        st[...] = value_fn()

    for (d, q, v, z, s0, o, fin, st, inter), (is_ctx, starts, ends) in zip(dirs, flags):
        pl.when(starts & is_ctx)(functools.partial(set_state, st, lambda: jnp.zeros((HG_H, HG_K, HG_K), F32)))
        pl.when(starts & jnp.logical_not(is_ctx))(functools.partial(set_state, st, lambda s0=s0: s0[0, 0]))
    spans = [_hg_main(d, layer, q, v, z, lbl_ref, o, st, inter)
             for (d, q, v, z, s0, o, fin, st, inter) in dirs]
    for (d, q, v, z, s0, o, fin, st, inter), span in zip(dirs, spans):
        pl.when(span > HG_SAFE_SPAN)(functools.partial(
            _hg_exact_intra, d, layer, q, v, z, lbl_ref, o, inter, cum_scr, k_scr))

    def write_final(fin, st):
        for hd in range(HG_H):
            fin[0, hd] = st[hd].T

    for (d, q, v, z, s0, o, fin, st, inter), (is_ctx, starts, ends) in zip(dirs, flags):
        pl.when(ends)(functools.partial(write_final, fin, st))


def hg_scan(proj, lb_logits, s0t, layer):
    def tok(d, col):
        return lambda c: (_chunk_of_step(d, c), col)

    def lat_idx(d):
        return lambda c: (d, jnp.maximum(_seq_of_chunk(_chunk_of_step(d, c)) - N_CTX_SEQ, 0), 0, 0, 0)

    def fin_idx(d):
        return lambda c: (jnp.minimum(_seq_of_chunk(_chunk_of_step(d, c)), N_CTX_SEQ), 0, 0, 0)

    state = (HG_H, HG_K, HG_K)
    return pl.pallas_call(
        functools.partial(_hg_scan_kernel, layer=layer),
        grid=(N_CHUNKS,),
        in_specs=[pl.BlockSpec((CHUNK, D), tok(0, 0)), pl.BlockSpec((CHUNK, D), tok(0, 1)),
                  pl.BlockSpec((CHUNK, D), tok(0, 3)),
                  pl.BlockSpec((CHUNK, D), tok(1, 0)), pl.BlockSpec((CHUNK, D), tok(1, 1)),
                  pl.BlockSpec((CHUNK, D), tok(1, 4)),
                  pl.BlockSpec((2, DEPTH, D), lambda c: (0, 0, 0)),
                  pl.BlockSpec((1, 1) + state, lat_idx(0)), pl.BlockSpec((1, 1) + state, lat_idx(1))],
        out_specs=[pl.BlockSpec((CHUNK, D), tok(0, 0)), pl.BlockSpec((CHUNK, D), tok(1, 0)),
                   pl.BlockSpec((1,) + state, fin_idx(0)), pl.BlockSpec((1,) + state, fin_idx(1))],
        out_shape=[jax.ShapeDtypeStruct((N_TOK, D), F32), jax.ShapeDtypeStruct((N_TOK, D), F32),
                   jax.ShapeDtypeStruct((N_CTX_SEQ + 1,) + state, F32),
                   jax.ShapeDtypeStruct((N_CTX_SEQ + 1,) + state, F32)],
        scratch_shapes=[pltpu.VMEM(state, F32), pltpu.VMEM(state, F32),
                        pltpu.VMEM((CHUNK, D), F32), pltpu.VMEM((CHUNK, D), F32),
                        pltpu.VMEM((CHUNK, D), F32), pltpu.VMEM((CHUNK, D), F32)],
        compiler_params=_cparams("arbitrary"),
        name="hg_scan",
    )(proj, proj, proj, proj, proj, proj, lb_logits, s0t, s0t)


def _hg_out_kernel(o0_ref, o1_ref, gate_ref, gn_ref, w_ref, x_ref, g1_ref, out_ref, on_scr):
    @pl.when(pl.program_id(1) == 0)
    def _():
        for hd in range(HG_H):
            sl = slice(HG_K * hd, HG_K * (hd + 1))
            o = o0_ref[:, sl] + o1_ref[:, sl]
            o = o * lax.rsqrt(jnp.mean(o * o, axis=-1, keepdims=True) + NORM_EPS) * gn_ref[...]
            on_scr[:, sl] = (o * _silu(gate_ref[:, sl])).astype(BF16)

    out_ref[...] = x_ref[...] + g1_ref[0] * jnp.dot(on_scr[...], w_ref[...].astype(BF16),
                                                    preferred_element_type=F32)


def hg_out(o_fwd, o_bwd, proj, g_norm, w_o, x, g1, tn=512):
    return pl.pallas_call(
        _hg_out_kernel,
        grid=(N_TOK // TM, D // tn),
        in_specs=[pl.BlockSpec((TM, D), lambda i, j: (i, 0)),
                  pl.BlockSpec((TM, D), lambda i, j: (i, 0)),
                  pl.BlockSpec((TM, D), lambda i, j: (i, 2)),
                  pl.BlockSpec((1, HG_K), lambda i, j: (0, 0)),
                  pl.BlockSpec((D, tn), lambda i, j: (0, j)),
                  pl.BlockSpec((TM, tn), lambda i, j: (i, j)),
                  pl.BlockSpec((1, 1, tn), lambda i, j: (_group_of_tile(i), 0, j))],
        out_specs=pl.BlockSpec((TM, tn), lambda i, j: (i, j)),
        out_shape=jax.ShapeDtypeStruct((N_TOK, D), F32),
        scratch_shapes=[pltpu.VMEM((TM, D), BF16)],
        compiler_params=_cparams("parallel", "arbitrary"),
        name="hg_out",
    )(o_fwd, o_bwd, proj, g_norm.reshape(1, HG_K), w_o, x, g1)


def hgrn_layer(x, mods, g_mix, state, layer, lb_logits, w_qig, w_f, b_f, g_norm, w_o):
    sh1, sc1, g1 = mods
    w5 = jnp.concatenate([w_qig, w_f[0], w_f[1]], axis=1)
    b5 = jnp.concatenate([jnp.zeros((3 * D,), F32), b_f[0], b_f[1]])
    proj = nm_matmul(x, g_mix, sc1, sh1, w5, b5, 5 * D, name="hg_proj")
    s0t = jnp.transpose(state, (1, 0, 2, 4, 3))
    o_fwd, o_bwd, fin_fwd, fin_bwd = hg_scan(proj, lb_logits, s0t, layer)
    x = hg_out(o_fwd, o_bwd, proj, g_norm, w_o, x, g1)
    return x, jnp.stack([fin_fwd[:N_CTX_SEQ], fin_bwd[:N_CTX_SEQ]], axis=1)


SSD_INNER = 2 * D
SSD_HEADS = 32
SSD_P = 64
SSD_NG = 4
SSD_N = 128
SSD_XBC = SSD_INNER + 2 * SSD_NG * SSD_N
SSD_ZX = SSD_INNER + SSD_XBC
SSD_CONV = 5
CONV_TM = 256
CONV_HALO = 8


def _ssd_conv_kernel(cur_ref, prev_ref, next_ref, w_ref, b_ref, o_ref, ext):
    i = pl.program_id(0)
    n_ctx_tiles = N_CTX_SEQ * CTX_LEN // CONV_TM
    per_lat = LAT_LEN // CONV_TM
    is_ctx = i < n_ctx_tiles
    k = (i - n_ctx_tiles) % per_lat
    seq_start = is_ctx | (k == 0)
    seq_end = is_ctx | (k == per_lat - 1)
    ext[0:CONV_HALO] = jnp.where(seq_start, 0.0, prev_ref[...])
    ext[CONV_HALO:CONV_HALO + CONV_TM] = cur_ref[...]
    ext[CONV_HALO + CONV_TM:] = jnp.where(seq_end, 0.0, next_ref[...])
    acc = jnp.broadcast_to(b_ref[...], (CONV_TM, D))
    for t in range(SSD_CONV):
        acc = acc + w_ref[t:t + 1, :] * ext[pl.ds(CONV_HALO - SSD_CONV // 2 + t, CONV_TM), :]
    o_ref[...] = _silu(acc)


def ssd_conv(zx, conv_w, conv_b):
    nrb = N_TOK // CONV_HALO
    rpt = CONV_TM // CONV_HALO
    c0 = SSD_INNER // D
    return pl.pallas_call(
        _ssd_conv_kernel,
        grid=(N_TOK // CONV_TM, SSD_XBC // D),
        in_specs=[pl.BlockSpec((CONV_TM, D), lambda i, j: (i, c0 + j)),
                  pl.BlockSpec((CONV_HALO, D), lambda i, j: (jnp.maximum(i * rpt - 1, 0), c0 + j)),
                  pl.BlockSpec((CONV_HALO, D), lambda i, j: (jnp.minimum((i + 1) * rpt, nrb - 1), c0 + j)),
                  pl.BlockSpec((SSD_CONV, D), lambda i, j: (0, j)),
                  pl.BlockSpec((1, D), lambda i, j: (0, j))],
        out_specs=pl.BlockSpec((CONV_TM, D), lambda i, j: (i, j)),
        out_shape=jax.ShapeDtypeStruct((N_TOK, SSD_XBC), F32),
        scratch_shapes=[pltpu.VMEM((CONV_TM + 2 * CONV_HALO, D), F32)],
        compiler_params=_cparams("parallel", "parallel"),
        name="ssd_conv",
    )(zx, zx, zx, conv_w, conv_b.reshape(1, SSD_XBC))


def _ssd_main(d, xlo_ref, xhi_ref, bc_ref, dtr_ref, dtb_ref, alog_ref, y_ref, ht_scr):
    xr = dtr_ref[...] + dtb_ref[...]
    dt = jnp.maximum(xr, 0.0) + jnp.log(1.0 + jnp.exp(-jnp.abs(xr)))
    dta = dt * (-jnp.exp(alog_ref[...]))
    tri = _tri(d, (CHUNK, CHUNK))
    cum = jnp.dot(tri.astype(F32), dta, precision=HIGHEST, preferred_element_type=F32)
    r = lax.broadcasted_iota(jnp.int32, (CHUNK, 2 * CHUNK), 0)
    cc = lax.broadcasted_iota(jnp.int32, (CHUNK, 2 * CHUNK), 1)
    lo_half = cc < CHUNK
    ccm = jnp.where(lo_half, cc, cc - CHUNK)
    trit = (ccm >= r) if d == 0 else (ccm <= r)
    tn = (((0,), (0,)), ((), ()))
    nt = (((1,), (1,)), ((), ()))
    cumt_lo = lax.dot_general(dta, (trit & lo_half).astype(F32), tn, precision=HIGHEST,
                              preferred_element_type=F32)
    cumt_hi = lax.dot_general(dta, (trit & jnp.logical_not(lo_half)).astype(F32), tn,
                              precision=HIGHEST, preferred_element_type=F32)
    lane = lax.broadcasted_iota(jnp.int32, (CHUNK, 2 * SSD_P), 1)
    first_head = lane < SSD_P
    tri2 = (r >= ccm) if d == 0 else (r <= ccm)
    bc = bc_ref[...]
    for gq in range(SSD_NG):
        bg = bc[:, SSD_N * gq:SSD_N * (gq + 1)].astype(BF16)
        cg = bc[:, SSD_NG * SSD_N + SSD_N * gq:SSD_NG * SSD_N + SSD_N * (gq + 1)].astype(BF16)
        cb2 = lax.dot_general(cg, jnp.concatenate([bg, bg], axis=0), nt,
                              preferred_element_type=F32)
        for pp in range(4 * gq, 4 * gq + 4):
            h1 = SSD_HEADS * d + 2 * pp
            colp = jnp.where(first_head, cum[:, h1:h1 + 1], cum[:, h1 + 1:h1 + 2])
            rowp = cumt_lo[h1:h1 + 1, :] + cumt_hi[h1 + 1:h1 + 2, :]
            lmat = jnp.exp(jnp.where(tri2, colp - rowp, -jnp.inf))
            dtp = jnp.where(first_head, dt[:, h1:h1 + 1], dt[:, h1 + 1:h1 + 2])
            xref = xlo_ref if pp < 8 else xhi_ref
            c0 = 128 * (pp % 8)
            xdt = xref[:, c0:c0 + 128] * dtp
            rhs = jnp.concatenate([jnp.where(first_head, xdt, 0.0),
                                   jnp.where(first_head, 0.0, xdt)], axis=0).astype(BF16)
            y = jnp.dot((cb2 * lmat).astype(BF16), rhs, preferred_element_type=F32)
            ht = ht_scr[:, 128 * pp:128 * (pp + 1)]
            y = y + jnp.dot(cg, ht.astype(BF16), preferred_element_type=F32) * jnp.exp(colp)
            y_ref[:, 128 * pp:128 * (pp + 1)] = y
            totp = colp[CHUNK - 1:CHUNK] if d == 0 else colp[0:1]
            xw = (xdt * jnp.exp(totp - colp)).astype(BF16)
            ht_scr[:, 128 * pp:128 * (pp + 1)] = (
                ht * jnp.exp(totp) + lax.dot_general(bg, xw, tn, preferred_element_type=F32))


def _ssd_scan_kernel(xlo0, xhi0, bc0, dtr0, xlo1, xhi1, bc1, dtr1, dtb_ref, alog_ref, h00, h01,
                     y0, y1, fin0, fin1, ht0, ht1):
    c = pl.program_id(0)
    dirs = ((0, xlo0, xhi0, bc0, dtr0, h00, y0, fin0, ht0), (1, xlo1, xhi1, bc1, dtr1, h01, y1, fin1, ht1))
    flags = [_chunk_flags(d, _chunk_of_step(d, c)) for d in (0, 1)]

    def set_state(ht, value_fn):
        ht[...] = value_fn()

    for (d, xlo, xhi, bc, dtr, h0, y, fin, ht), (is_ctx, starts, ends) in zip(dirs, flags):
        pl.when(starts & is_ctx)(functools.partial(
            set_state, ht, lambda: jnp.zeros((SSD_N, SSD_HEADS * SSD_P), F32)))
        pl.when(starts & jnp.logical_not(is_ctx))(functools.partial(set_state, ht, lambda h0=h0: h0[0, 0]))
    for (d, xlo, xhi, bc, dtr, h0, y, fin, ht) in dirs:
        _ssd_main(d, xlo, xhi, bc, dtr, dtb_ref, alog_ref, y, ht)
    def write_final(fin, ht):
        fin[0] = ht[...]

    for (d, xlo, xhi, bc, dtr, h0, y, fin, ht), (is_ctx, starts, ends) in zip(dirs, flags):
        pl.when(ends)(functools.partial(write_final, fin, ht))


def ssd_scan(xbc, dtr, dt_bias, a_log, h0t):
    nh2 = 2 * SSD_HEADS
    hp = SSD_HEADS * SSD_P

    def tok(d, col):
        return lambda c: (_chunk_of_step(d, c), col)

    def lat_idx(d):
        return lambda c: (d, jnp.maximum(_seq_of_chunk(_chunk_of_step(d, c)) - N_CTX_SEQ, 0), 0, 0)

    def fin_idx(d):
        return lambda c: (jnp.minimum(_seq_of_chunk(_chunk_of_step(d, c)), N_CTX_SEQ), 0, 0)

    def tok_specs(d):
        return [pl.BlockSpec((CHUNK, D), tok(d, 0)), pl.BlockSpec((CHUNK, D), tok(d, 1)),
                pl.BlockSpec((CHUNK, D), tok(d, 2)), pl.BlockSpec((CHUNK, nh2), tok(d, 0))]

    return pl.pallas_call(
        _ssd_scan_kernel,
        grid=(N_CHUNKS,),
        in_specs=tok_specs(0) + tok_specs(1) + [
            pl.BlockSpec((1, nh2), lambda c: (0, 0)), pl.BlockSpec((1, nh2), lambda c: (0, 0)),
            pl.BlockSpec((1, 1, SSD_N, hp), lat_idx(0)), pl.BlockSpec((1, 1, SSD_N, hp), lat_idx(1))],
        out_specs=[pl.BlockSpec((CHUNK, hp), tok(0, 0)), pl.BlockSpec((CHUNK, hp), tok(1, 0)),
                   pl.BlockSpec((1, SSD_N, hp), fin_idx(0)), pl.BlockSpec((1, SSD_N, hp), fin_idx(1))],
        out_shape=[jax.ShapeDtypeStruct((N_TOK, hp), F32), jax.ShapeDtypeStruct((N_TOK, hp), F32),
                   jax.ShapeDtypeStruct((N_CTX_SEQ + 1, SSD_N, hp), F32),
                   jax.ShapeDtypeStruct((N_CTX_SEQ + 1, SSD_N, hp), F32)],
        scratch_shapes=[pltpu.VMEM((SSD_N, hp), F32), pltpu.VMEM((SSD_N, hp), F32)],
        compiler_params=_cparams("arbitrary"),
        name="ssd_scan",
    )(xbc, xbc, xbc, dtr, xbc, xbc, xbc, dtr, dt_bias.reshape(1, nh2), a_log.reshape(1, nh2), h0t, h0t)


SSD_OUT_TM = 512


def _ssd_out_kernel(xlo_ref, xhi_ref, zlo_ref, zhi_ref, y0_ref, y1_ref, dsk_ref, gn_ref, w_ref,
                    x_ref, g1_ref, out_ref, yn_scr):
    @pl.when(pl.program_id(1) == 0)
    def _():
        halves = []
        ss = jnp.zeros((SSD_OUT_TM, 1), F32)
        for k, (xr, zr) in enumerate(((xlo_ref, zlo_ref), (xhi_ref, zhi_ref))):
            sl = slice(D * k, D * (k + 1))
            y = dsk_ref[:, sl] * xr[...] + y0_ref[:, sl] + y1_ref[:, sl]
            y = y * _silu(zr[...])
            ss = ss + jnp.sum(y * y, axis=-1, keepdims=True)
            halves.append(y)
        scale = lax.rsqrt(ss / SSD_INNER + NORM_EPS)
        for k, y in enumerate(halves):
            sl = slice(D * k, D * (k + 1))
            yn_scr[:, sl] = (y * scale * gn_ref[:, sl]).astype(BF16)

    out_ref[...] = x_ref[...] + g1_ref[0] * jnp.dot(yn_scr[...], w_ref[...].astype(BF16),
                                                    preferred_element_type=F32)


def ssd_out(xbc, zx, y_fwd, y_bwd, d_skip_cols, g_norm, w_out, x, g1, tn=512):
    tm = SSD_OUT_TM
    tpg = GROUP_ROWS // tm
    return pl.pallas_call(
        _ssd_out_kernel,
        grid=(N_TOK // tm, D // tn),
        in_specs=[pl.BlockSpec((tm, D), lambda i, j: (i, 0)),
                  pl.BlockSpec((tm, D), lambda i, j: (i, 1)),
                  pl.BlockSpec((tm, D), lambda i, j: (i, 0)),
                  pl.BlockSpec((tm, D), lambda i, j: (i, 1)),
                  pl.BlockSpec((tm, SSD_INNER), lambda i, j: (i, 0)),
                  pl.BlockSpec((tm, SSD_INNER), lambda i, j: (i, 0)),
                  pl.BlockSpec((1, SSD_INNER), lambda i, j: (0, 0)),
                  pl.BlockSpec((1, SSD_INNER), lambda i, j: (0, 0)),
                  pl.BlockSpec((SSD_INNER, tn), lambda i, j: (0, j)),
                  pl.BlockSpec((tm, tn), lambda i, j: (i, j)),
                  pl.BlockSpec((1, 1, tn), lambda i, j: (i // tpg, 0, j))],
        out_specs=pl.BlockSpec((tm, tn), lambda i, j: (i, j)),
        out_shape=jax.ShapeDtypeStruct((N_TOK, D), F32),
        scratch_shapes=[pltpu.VMEM((tm, SSD_INNER), BF16)],
        compiler_params=_cparams("parallel", "arbitrary"),
        name="ssd_out",
    )(xbc, xbc, zx, zx, y_fwd, y_bwd, d_skip_cols, g_norm.reshape(1, SSD_INNER), w_out, x, g1)


def ssd_layer(x, mods, g_mix, state, w_in, conv_w, conv_b, dt_bias, a_log, d_skip, g_norm, w_out):
    sh1, sc1, g1 = mods
    zx = nm_matmul(x, g_mix, sc1, sh1, w_in, jnp.zeros((SSD_ZX,), F32), SSD_ZX, name="ssd_proj")
    nh2 = 2 * SSD_HEADS
    dtr = nm_matmul(x, g_mix, sc1, sh1, w_in[:, SSD_ZX:], jnp.zeros((nh2,), F32), nh2, tn=nh2,
                    name="ssd_proj_dt")
    xbc = ssd_conv(zx, conv_w, conv_b)
    h0t = jnp.transpose(state, (1, 0, 4, 2, 3)).reshape(2, N_LAT_SEQ, SSD_N, SSD_HEADS * SSD_P)
    y_fwd, y_bwd, fin_fwd, fin_bwd = ssd_scan(xbc, dtr, dt_bias, a_log, h0t)
    dcols = jnp.repeat(d_skip, SSD_P).reshape(1, SSD_INNER)
    x = ssd_out(xbc, zx, y_fwd, y_bwd, dcols, g_norm, w_out, x, g1)
    fin = jnp.stack([fin_fwd[:N_CTX_SEQ], fin_bwd[:N_CTX_SEQ]], axis=1)
    fin = jnp.transpose(fin.reshape(N_CTX_SEQ, 2, SSD_N, SSD_HEADS, SSD_P), (0, 1, 3, 4, 2))
    return x, fin


N_EXP = 16
FF = 2 * D
CAP_CTX = 2 * CTX_LEN // N_EXP
CAP_LAT = 2 * LAT_LEN // N_EXP
SLOTS_PER_GROUP = 512
SLOTS = N_GROUPS * SLOTS_PER_GROUP


def _router_kernel(x_ref, g_ref, sc_ref, sh_ref, wt_ref, h_ref, aff_ref):
    h = _normmod(x_ref[...], g_ref[...], sc_ref[0], sh_ref[0])
    h_ref[...] = h.astype(BF16)
    logits = lax.dot_general(wt_ref[...], h, (((1,), (1,)), ((), ())), precision=HIGHEST,
                             preferred_element_type=F32)
    e = jnp.exp(logits - jnp.max(logits, axis=0, keepdims=True))
    aff_ref[...] = e / jnp.sum(e, axis=0, keepdims=True)


def moe_route(x, g, sc, sh, w_router_t):
    return pl.pallas_call(
        _router_kernel,
        grid=(N_TOK // TM,),
        in_specs=[pl.BlockSpec((TM, D), lambda i: (i, 0)),
                  pl.BlockSpec((1, D), lambda i: (0, 0)),
                  pl.BlockSpec((1, 1, D), lambda i: (_group_of_tile(i), 0, 0)),
                  pl.BlockSpec((1, 1, D), lambda i: (_group_of_tile(i), 0, 0)),
                  pl.BlockSpec((N_EXP, D), lambda i: (0, 0))],
        out_specs=[pl.BlockSpec((TM, D), lambda i: (i, 0)),
                   pl.BlockSpec((N_EXP, TM), lambda i: (0, i))],
        out_shape=[jax.ShapeDtypeStruct((N_TOK, D), BF16),
                   jax.ShapeDtypeStruct((N_EXP, N_TOK), F32)],
        compiler_params=_cparams("parallel"),
        name="moe_router",
    )(x, g.reshape(1, D), sc, sh, w_router_t)


def _lane_prefix_excl(m):
    s, t = m.shape
    r = lax.broadcasted_iota(jnp.int32, (128, 128), 0)
    c = lax.broadcasted_iota(jnp.int32, (128, 128), 1)
    upper = (r <= c).astype(BF16)
    run = jnp.zeros((s, 1), F32)
    out = []
    for k in range(t // 128):
        blk = m[:, 128 * k:128 * (k + 1)]
        inc = jnp.dot(blk.astype(BF16), upper, preferred_element_type=F32) + run
        out.append(inc - blk)
        run = inc[:, 127:128]
    return jnp.concatenate(out, axis=1)


def _select_kernel(a_ref, off_ref, slot_ref, base_ref, *, cap):
    bits = pltpu.bitcast(a_ref[...], jnp.int32)
    s = bits.shape[0]
    capf = float(cap)

    def body(_, lohi):
        lo, hi = lohi
        mid = lo + ((hi - lo + 1) >> 1)
        cnt = jnp.sum((bits >= mid).astype(F32), axis=1, keepdims=True)
        ok = cnt >= capf
        return jnp.where(ok, mid, lo), jnp.where(ok, hi, mid - 1)

    lo0 = jnp.zeros((s, 1), jnp.int32)
    hi0 = jnp.full((s, 1), 0x7F800000, jnp.int32)
    thr, _ = lax.fori_loop(0, 31, body, (lo0, hi0))
    gt = (bits > thr).astype(F32)
    eq = (bits == thr).astype(F32)
    need = capf - jnp.sum(gt, axis=1, keepdims=True)
    sel = gt + eq * (_lane_prefix_excl(eq) < need).astype(F32)
    slot = _lane_prefix_excl(sel) + off_ref[...]
    slot_ref[...] = jnp.where(sel > 0.0, slot, -1.0).astype(jnp.int32)
    t = bits.shape[1]
    tok = lax.broadcasted_iota(jnp.int32, (t, 128), 0)
    tile = lax.broadcasted_iota(jnp.int32, (t, 128), 1)
    ahead = (tok < tile * TOK_TILE).astype(BF16)
    base_ref[...] = jnp.dot(sel.astype(BF16), ahead, preferred_element_type=F32).astype(jnp.int32)


def moe_select(aff, off, cap):
    s, t = aff.shape
    return pl.pallas_call(
        functools.partial(_select_kernel, cap=cap),
        grid=(1,),
        in_specs=[pl.BlockSpec((s, t), lambda i: (0, 0)),
                  pl.BlockSpec((s, 1), lambda i: (0, 0))],
        out_specs=[pl.BlockSpec((s, t), lambda i: (0, 0)),
                   pl.BlockSpec((s, 128), lambda i: (0, 0))],
        out_shape=[jax.ShapeDtypeStruct((s, t), jnp.int32),
                   jax.ShapeDtypeStruct((s, 128), jnp.int32)],
        compiler_params=_cparams("arbitrary"),
        name="moe_select",
    )(aff, off)


TOK_TILE = 256
TILES = GROUP_ROWS // TOK_TILE
CNT_STRIDE = TILES + 1
SLOT_BLK = 128
TILES_PER_BLK = TILES * SLOT_BLK // SLOTS_PER_GROUP
GATHER_MARGIN = 2
BAND_MARGIN = 48


def _gather_kernel(cnt_ref, slot_ref, aff_ref, h_ref, xs_ref, gs_ref):
    base = (pl.program_id(0) * N_EXP + pl.program_id(1)) * CNT_STRIDE

    def gather(k, toks):
        n = toks.stop - toks.start
        s = lax.broadcasted_iota(jnp.int32, (SLOT_BLK, n), 0) + SLOT_BLK * k
        oh = s == slot_ref[0, 0, :, toks]
        rows = slice(SLOT_BLK * k, SLOT_BLK * (k + 1))
        xs_ref[0, rows, :] = jnp.dot(oh.astype(BF16), h_ref[toks, :],
                                     preferred_element_type=F32).astype(BF16)
        gs_ref[0, rows, :] = jnp.sum(jnp.where(oh, aff_ref[0, :, toks], 0.0), axis=1, keepdims=True)

    for k in range(SLOTS_PER_GROUP // SLOT_BLK):
        lo = max(TILES_PER_BLK * k - GATHER_MARGIN, 0)
        hi = min(TILES_PER_BLK * (k + 1) + GATHER_MARGIN, TILES)
        in_band = (cnt_ref[base + lo] <= SLOT_BLK * k) & (cnt_ref[base + hi] >= SLOT_BLK * (k + 1))

        @pl.when(in_band)
        def _():
            gather(k, slice(TOK_TILE * lo, TOK_TILE * hi))

        @pl.when(jnp.logical_not(in_band))
        def _():
            gather(k, slice(0, GROUP_ROWS))


def moe_gather(cnt, slot, aff3, h2):
    return pl.pallas_call(
        _gather_kernel,
        grid_spec=pltpu.PrefetchScalarGridSpec(
            num_scalar_prefetch=1,
            grid=(N_GROUPS, N_EXP),
            in_specs=[pl.BlockSpec((1, 1, 1, GROUP_ROWS), lambda r, e, c: (r, e, 0, 0)),
                      pl.BlockSpec((1, 1, GROUP_ROWS), lambda r, e, c: (e, 0, r)),
                      pl.BlockSpec((GROUP_ROWS, D), lambda r, e, c: (r, 0))],
            out_specs=[pl.BlockSpec((1, SLOTS_PER_GROUP, D), lambda r, e, c: (e, r, 0)),
                       pl.BlockSpec((1, SLOTS_PER_GROUP, 1), lambda r, e, c: (e, r, 0))]),
        out_shape=[jax.ShapeDtypeStruct((N_EXP, SLOTS, D), BF16),
                   jax.ShapeDtypeStruct((N_EXP, SLOTS, 1), F32)],
        compiler_params=_cparams("parallel", "parallel"),
        name="moe_gather",
    )(cnt, slot, aff3, h2)


FF_TILE = 512
FFN_ROWS = 512


def _ffn_kernel(xs_ref, gs_ref, wg_ref, wu_ref, wd_ref, ys_ref, acc):
    f = pl.program_id(1)

    @pl.when(f == 0)
    def _():
        acc[...] = jnp.zeros_like(acc)

    wg = wg_ref[0, 0].astype(BF16)
    wu = wu_ref[0, 0].astype(BF16)
    wd = wd_ref[0, 0].astype(BF16)
    for rb in range(SLOTS // FFN_ROWS):
        rows = slice(FFN_ROWS * rb, FFN_ROWS * (rb + 1))
        x = xs_ref[0, rows, :]
        g = jnp.dot(x, wg, preferred_element_type=F32)
        u = jnp.dot(x, wu, preferred_element_type=F32)
        hid = (_silu(g) * u).astype(BF16)
        acc[rows, :] += jnp.dot(hid, wd, preferred_element_type=F32)

    @pl.when(f == FF // FF_TILE - 1)
    def _():
        ys_ref[0] = (acc[...] * gs_ref[0]).astype(BF16)


def moe_ffn(xs, gs, w_gate, w_up, w_down, layer):
    return pl.pallas_call(
        _ffn_kernel,
        grid=(N_EXP, FF // FF_TILE),
        in_specs=[pl.BlockSpec((1, SLOTS, D), lambda e, f: (e, 0, 0)),
                  pl.BlockSpec((1, SLOTS, 1), lambda e, f: (e, 0, 0)),
                  pl.BlockSpec((1, 1, D, FF_TILE), lambda e, f: (layer, e, 0, f)),
                  pl.BlockSpec((1, 1, D, FF_TILE), lambda e, f: (layer, e, 0, f)),
                  pl.BlockSpec((1, 1, FF_TILE, D), lambda e, f: (layer, e, f, 0))],
        out_specs=pl.BlockSpec((1, SLOTS, D), lambda e, f: (e, 0, 0)),
        out_shape=jax.ShapeDtypeStruct((N_EXP, SLOTS, D), BF16),
        scratch_shapes=[pltpu.VMEM((SLOTS, D), F32)],
        compiler_params=_cparams("parallel", "arbitrary"),
        name="moe_ffn",
    )(xs, gs, w_gate, w_up, w_down)


SCAT_TN = 512


def _scatter_kernel(cnt_ref, slot_ref, ys_ref, x_ref, g2_ref, o_ref):
    r, j = pl.program_id(0), pl.program_id(2)
    cap_tile = SLOTS_PER_GROUP // TILES
    start = pl.multiple_of(jnp.clip(cap_tile * j - BAND_MARGIN, 0, SLOTS_PER_GROUP - SLOT_BLK), 16)
    in_band = True
    for e in range(N_EXP):
        base = (r * N_EXP + e) * CNT_STRIDE + j
        in_band = in_band & (cnt_ref[base] >= start) & (cnt_ref[base + 1] <= start + SLOT_BLK)

    def scatter(first, n_blocks):
        lane = lax.broadcasted_iota(jnp.int32, (TOK_TILE, SLOT_BLK), 1)
        acc = jnp.zeros((TOK_TILE, SCAT_TN), F32)
        for b in range(n_blocks):
            lo = first + SLOT_BLK * b
            for e in range(N_EXP):
                oh = (slot_ref[0, :, e:e + 1] == lane + lo).astype(BF16)
                acc = acc + jnp.dot(oh, ys_ref[e, pl.ds(lo, SLOT_BLK), :], preferred_element_type=F32)
        o_ref[...] = x_ref[...] + g2_ref[0] * acc

    @pl.when(in_band)
    def _():
        scatter(start, 1)

    @pl.when(jnp.logical_not(in_band))
    def _():
        scatter(0, SLOTS_PER_GROUP // SLOT_BLK)


def moe_scatter(cnt, slot_t, ys, x, g2):
    return pl.pallas_call(
        _scatter_kernel,
        grid_spec=pltpu.PrefetchScalarGridSpec(
            num_scalar_prefetch=1,
            grid=(N_GROUPS, D // SCAT_TN, TILES),
            in_specs=[pl.BlockSpec((1, TOK_TILE, 128), lambda r, c, j, n: (r, j, 0)),
                      pl.BlockSpec((N_EXP, SLOTS_PER_GROUP, SCAT_TN), lambda r, c, j, n: (0, r, c)),
                      pl.BlockSpec((TOK_TILE, SCAT_TN), lambda r, c, j, n: (r * TILES + j, c)),
                      pl.BlockSpec((1, 1, SCAT_TN), lambda r, c, j, n: (r, 0, c))],
            out_specs=pl.BlockSpec((TOK_TILE, SCAT_TN), lambda r, c, j, n: (r * TILES + j, c))),
        out_shape=jax.ShapeDtypeStruct((N_TOK, D), F32),
        compiler_params=_cparams("parallel", "parallel", "arbitrary"),
        name="moe_scatter",
    )(cnt, slot_t, ys, x, g2)


def moe_layer(x, mods, g_ffn, layer, w_router, w_gate, w_up, w_down):
    sh2, sc2, g2 = mods
    h2, aff = moe_route(x, g_ffn, sc2, sh2, w_router.T)
    aff_ctx = aff[:, :GROUP_ROWS].reshape(N_EXP * N_CTX_SEQ, CTX_LEN)
    off_ctx = jnp.tile(jnp.arange(N_CTX_SEQ, dtype=F32) * CAP_CTX, N_EXP).reshape(-1, 1)
    slot_ctx, _ = moe_select(aff_ctx, off_ctx, CAP_CTX)
    aff_lat = jnp.transpose(aff[:, GROUP_ROWS:].reshape(N_EXP, N_LAT_SEQ, LAT_LEN), (1, 0, 2)
                            ).reshape(N_LAT_SEQ * N_EXP, LAT_LEN)
    slot_lat, base_lat = moe_select(aff_lat, jnp.zeros((N_LAT_SEQ * N_EXP, 1), F32), CAP_LAT)
    slot = jnp.concatenate([slot_ctx.reshape(1, N_EXP, GROUP_ROWS),
                            slot_lat.reshape(N_LAT_SEQ, N_EXP, LAT_LEN)], axis=0)
    slot_t = jnp.pad(jnp.transpose(slot, (0, 2, 1)), ((0, 0), (0, 0), (0, 128 - N_EXP)),
                     constant_values=-1)
    slot = slot.reshape(N_GROUPS, N_EXP, 1, GROUP_ROWS)
    cnt_ctx = jnp.broadcast_to(jnp.arange(CNT_STRIDE, dtype=jnp.int32) * CAP_CTX, (1, N_EXP, CNT_STRIDE))
    cnt = jnp.concatenate([cnt_ctx, base_lat[:, :CNT_STRIDE].reshape(N_LAT_SEQ, N_EXP, CNT_STRIDE)],
                          axis=0).reshape(-1)
    xs, gs = moe_gather(cnt, slot, aff.reshape(N_EXP, 1, N_TOK), h2)
    ys = moe_ffn(xs, gs, w_gate, w_up, w_down, layer)
    return moe_scatter(cnt, slot_t, ys, x, g2)


def _final_norm_kernel(x_ref, g_ref, ctx_ref, lat_ref):
    x = x_ref[...]
    y = x * lax.rsqrt(jnp.mean(x * x, axis=-1, keepdims=True) + NORM_EPS) * g_ref[...]
    is_ctx = pl.program_id(0) < TILES_PER_GROUP

    @pl.when(is_ctx)
    def _():
        ctx_ref[...] = y

    @pl.when(jnp.logical_not(is_ctx))
    def _():
        lat_ref[...] = y


def final_norm(x, g):
    t = TILES_PER_GROUP
    return pl.pallas_call(
        _final_norm_kernel,
        grid=(N_TOK // TM,),
        in_specs=[pl.BlockSpec((TM, D), lambda i: (i, 0)),
                  pl.BlockSpec((1, D), lambda i: (0, 0))],
        out_specs=[pl.BlockSpec((TM, D), lambda i: (jnp.minimum(i, t - 1), 0)),
                   pl.BlockSpec((TM, D), lambda i: (jnp.maximum(i - t, 0), 0))],
        out_shape=[jax.ShapeDtypeStruct((GROUP_ROWS, D), F32),
                   jax.ShapeDtypeStruct((N_TOK - GROUP_ROWS, D), F32)],
        compiler_params=_cparams("arbitrary"),
        name="final_norm",
    )(x, g.reshape(1, D))


def _grid_pos_embed():
    rows = LAT_LEN // GRID_W
    quarter = D // 4
    omega = 1.0 / (10000.0 ** (jnp.arange(quarter, dtype=F32) / quarter))
    r = jnp.arange(rows, dtype=F32)[:, None] * omega
    cl = jnp.arange(GRID_W, dtype=F32)[:, None] * omega
    emb_r = jnp.concatenate([jnp.sin(r), jnp.cos(r)], axis=-1)
    emb_c = jnp.concatenate([jnp.sin(cl), jnp.cos(cl)], axis=-1)
    emb = jnp.concatenate([jnp.broadcast_to(emb_r[:, None], (rows, GRID_W, D // 2)),
                           jnp.broadcast_to(emb_c[None], (rows, GRID_W, D // 2))], axis=-1)
    return emb.reshape(LAT_LEN, D)


def kernel(x_prompt, x_sample, state_s5_re, state_s5_im, state_hgrn, state_ssd, c, c_ctx, w_ada, b_ada, norm_mix, norm_ffn, norm_final, s5_lam_re, s5_lam_im, s5_log_dt, s5_b_re, s5_b_im, s5_c_re, s5_c_im, s5_d, s5_w_glu, s5_b_glu, hg_w_qig, hg_w_f, hg_b_f, hg_lb_logits, hg_norm, hg_w_o, ssd_w_in, ssd_conv_w, ssd_conv_b, ssd_dt_bias, ssd_a_log, ssd_d, ssd_norm, ssd_w_out, moe_router, moe_w_gate, moe_w_up, moe_w_down):
    cond8 = jnp.concatenate([c_ctx[None], c, jnp.zeros((5, D), F32)], axis=0)
    mod = ada_mod(cond8, w_ada, b_ada)
    mods = jnp.transpose(mod.reshape(DEPTH, 8, 6, D)[:, :3], (0, 2, 1, 3)).reshape(DEPTH, 6, 3, 1, D)
    x = embed_tokens(x_prompt.reshape(-1, D), x_sample.reshape(-1, D), _grid_pos_embed())
    s5_re, s5_im, hg_fin, ssd_fin = [], [], [], []
    for i in range(DEPTH):
        mix_mods = (mods[i, 0], mods[i, 1], mods[i, 2])
        kind, j = i % 3, i // 3
        if kind == 0:
            x, fr, fi = s5_layer(x, mix_mods, norm_mix[i], state_s5_re[:, j], state_s5_im[:, j],
                                 s5_lam_re[j], s5_lam_im[j], s5_log_dt[j], s5_b_re[j], s5_b_im[j],
                                 s5_c_re[j], s5_c_im[j], s5_d[j], s5_w_glu[j], s5_b_glu[j])
            s5_re.append(fr)
            s5_im.append(fi)
        elif kind == 1:
            x, fh = hgrn_layer(x, mix_mods, norm_mix[i], state_hgrn[:, j], i, hg_lb_logits,
                               hg_w_qig[j], hg_w_f[j], hg_b_f[j], hg_norm[j], hg_w_o[j])
            hg_fin.append(fh)
        else:
            x, fs = ssd_layer(x, mix_mods, norm_mix[i], state_ssd[:, j], ssd_w_in[j], ssd_conv_w[j],
                              ssd_conv_b[j], ssd_dt_bias[j], ssd_a_log[j], ssd_d[j], ssd_norm[j],
                              ssd_w_out[j])
            ssd_fin.append(fs)
        x = moe_layer(x, (mods[i, 3], mods[i, 4], mods[i, 5]), norm_ffn[i], i, moe_router[i],
                      moe_w_gate, moe_w_up, moe_w_down)
    y_ctx, y_lat = final_norm(x, norm_final)
    return (y_ctx.reshape(N_CTX_SEQ, CTX_LEN, D), y_lat.reshape(N_LAT_SEQ, LAT_LEN, D),
            jnp.stack(s5_re, axis=1), jnp.stack(s5_im, axis=1),
            jnp.stack(hg_fin, axis=1), jnp.stack(ssd_fin, axis=1))
```

```python
import functools
import math

import jax
import jax.numpy as jnp
from jax import lax
from jax.experimental import pallas as pl
from jax.experimental.pallas import tpu as pltpu

F32 = jnp.float32
BF16 = jnp.bfloat16
HIGHEST = lax.Precision.HIGHEST

D = 1024
DEPTH = 4
N_CTX_SEQ = 16
CTX_LEN = 256
N_LAT_SEQ = 2
LAT_LEN = 4096
GROUP_ROWS = 4096
N_GROUPS = 3
N_TOK = N_GROUPS * GROUP_ROWS
N_SEQ = N_CTX_SEQ + N_LAT_SEQ
NORM_EPS = 1e-6
GRID_W = 64

VMEM_LIMIT_BYTES = 56 * 1024 * 1024


def _cparams(*sem):
    return pltpu.CompilerParams(dimension_semantics=sem, vmem_limit_bytes=VMEM_LIMIT_BYTES)


def _silu(x):
    return x * jax.nn.sigmoid(x)


def _normmod(x, g, sc, sh):
    ms = jnp.mean(x * x, axis=-1, keepdims=True)
    return x * lax.rsqrt(ms + NORM_EPS) * g * (1.0 + sc) + sh


def _cmul(ar, ai, br, bi):
    return ar * br - ai * bi, ar * bi + ai * br


def _mod_kernel(c_ref, w_ref, b_ref, o_ref):
    o_ref[0] = jnp.dot(_silu(c_ref[...]), w_ref[0], precision=HIGHEST,
                       preferred_element_type=F32) + b_ref[0]


def ada_mod(cond8, w_ada, b_ada):
    tn = 1536
    return pl.pallas_call(
        _mod_kernel,
        grid=(DEPTH, 6 * D // tn),
        in_specs=[pl.BlockSpec((8, D), lambda i, j: (0, 0)),
                  pl.BlockSpec((1, D, tn), lambda i, j: (i, 0, j)),
                  pl.BlockSpec((1, 1, tn), lambda i, j: (i, 0, j))],
        out_specs=pl.BlockSpec((1, 8, tn), lambda i, j: (i, 0, j)),
        out_shape=jax.ShapeDtypeStruct((DEPTH, 8, 6 * D), F32),
        compiler_params=_cparams("parallel", "parallel"),
        name="ada_mod",
    )(cond8, w_ada, b_ada.reshape(DEPTH, 1, 6 * D))


def _embed_kernel(xp_ref, xs_ref, pos_ref, o_ref):
    r = pl.program_id(0)

    @pl.when(r == 0)
    def _():
        o_ref[...] = xp_ref[...]

    @pl.when(r > 0)
    def _():
        o_ref[...] = xs_ref[...] + pos_ref[...]


def embed_tokens(xp, xs, pos):
    tm = 1024
    nt = GROUP_ROWS // tm
    return pl.pallas_call(
        _embed_kernel,
        grid=(N_GROUPS, nt),
        in_specs=[pl.BlockSpec((tm, D), lambda r, i: (jnp.where(r == 0, i, 0), 0)),
                  pl.BlockSpec((tm, D), lambda r, i: (jnp.where(r == 0, 0, (r - 1) * nt + i), 0)),
                  pl.BlockSpec((tm, D), lambda r, i: (i, 0))],
        out_specs=pl.BlockSpec((tm, D), lambda r, i: (r * nt + i, 0)),
        out_shape=jax.ShapeDtypeStruct((N_TOK, D), F32),
        compiler_params=_cparams("parallel", "parallel"),
        name="embed_tokens",
    )(xp, xs, pos)


TM = 1024
TILES_PER_GROUP = GROUP_ROWS // TM


def _group_of_tile(i):
    return i // TILES_PER_GROUP


def _normmod_kernel(x_ref, g_ref, sc_ref, sh_ref, o_ref):
    o_ref[...] = _normmod(x_ref[...], g_ref[...], sc_ref[0], sh_ref[0])


def normmod(x, g, sc, sh):
    return pl.pallas_call(
        _normmod_kernel,
        grid=(N_TOK // TM,),
        in_specs=[pl.BlockSpec((TM, D), lambda i: (i, 0)),
                  pl.BlockSpec((1, D), lambda i: (0, 0)),
                  pl.BlockSpec((1, 1, D), lambda i: (_group_of_tile(i), 0, 0)),
                  pl.BlockSpec((1, 1, D), lambda i: (_group_of_tile(i), 0, 0))],
        out_specs=pl.BlockSpec((TM, D), lambda i: (i, 0)),
        out_shape=jax.ShapeDtypeStruct((N_TOK, D), F32),
        compiler_params=_cparams("parallel"),
        name="normmod",
    )(x, g.reshape(1, D), sc, sh)


def _nm_matmul_kernel(x_ref, g_ref, sc_ref, sh_ref, w_ref, b_ref, o_ref, h_scr):
    @pl.when(pl.program_id(1) == 0)
    def _():
        h_scr[...] = _normmod(x_ref[...], g_ref[...], sc_ref[0], sh_ref[0]).astype(BF16)

    o_ref[...] = jnp.dot(h_scr[...], w_ref[...].astype(BF16),
                         preferred_element_type=F32) + b_ref[...]


def nm_matmul(x, g, sc, sh, w, b, n_out, tn=1024, name="nm_matmul"):
    return pl.pallas_call(
        _nm_matmul_kernel,
        grid=(N_TOK // TM, n_out // tn),
        in_specs=[pl.BlockSpec((TM, D), lambda i, j: (i, 0)),
                  pl.BlockSpec((1, D), lambda i, j: (0, 0)),
                  pl.BlockSpec((1, 1, D), lambda i, j: (_group_of_tile(i), 0, 0)),
                  pl.BlockSpec((1, 1, D), lambda i, j: (_group_of_tile(i), 0, 0)),
                  pl.BlockSpec((D, tn), lambda i, j: (0, j)),
                  pl.BlockSpec((1, tn), lambda i, j: (0, j))],
        out_specs=pl.BlockSpec((TM, tn), lambda i, j: (i, j)),
        out_shape=jax.ShapeDtypeStruct((N_TOK, n_out), F32),
        scratch_shapes=[pltpu.VMEM((TM, D), BF16)],
        compiler_params=_cparams("parallel", "arbitrary"),
        name=name,
    )(x, g.reshape(1, D), sc, sh, w, b.reshape(1, -1))


S5_G = 64
S5_H = 16
S5_P = 64
S5_L = 16
S5_GB = 8
S5_NB = S5_G // S5_GB
S5_CH = GROUP_ROWS // S5_L
S5_CTX_CH = CTX_LEN // S5_L
S5_NPOW = 8
S5_XW = S5_L * 128
S5_SW = 2 * S5_GB * S5_P


S5_LH = S5_L * S5_H
S5_NPWR = 24


def _s5_prep_kernel(lr_ref, li_ref, ldt_ref, btr_ref, bti_ref, cxr_ref, cxi_ref,
                    sel0_ref, sel1_ref, exp_ref,
                    kk_ref, wst_ref, wout_ref, apr_ref, api_ref):
    dt = jnp.exp(ldt_ref[0])
    lam_r, lam_i = lr_ref[0], li_ref[0]
    ar, ai = lam_r * dt, lam_i * dt
    pw = lax.broadcasted_iota(jnp.int32, (S5_GB, S5_NPWR, 2 * S5_P), 1).astype(F32)
    ep = jnp.exp(pw * ar)
    pwr, pwi = ep * jnp.cos(pw * ai), ep * jnp.sin(pw * ai)
    den = lam_r * lam_r + lam_i * lam_i
    nr, ni = pwr[:, 1:2] - 1.0, pwi[:, 1:2]
    beta_r = (nr * lam_r + ni * lam_i) / den
    beta_i = (ni * lam_r - nr * lam_i) / den
    bbr, bbi = _cmul(beta_r, beta_i, btr_ref[0], bti_ref[0])

    lane = lax.broadcasted_iota(jnp.int32, (S5_H, 2 * S5_P), 1)
    wst_ref[0, 0] = jnp.zeros((S5_XW, S5_SW), BF16)
    half = S5_SW // 2
    for g in range(S5_GB):
        mine = (lane >= S5_P) if g % 2 else (lane < S5_P)
        col = 128 * (g // 2)
        for s in range(S5_L):
            k = S5_L - 1 - s
            wr, wi = _cmul(pwr[g, k:k + 1], pwi[g, k:k + 1], bbr[g], bbi[g])
            rows = slice(128 * s + S5_H * g, 128 * s + S5_H * (g + 1))
            wst_ref[0, 0, rows, col:col + 128] = jnp.where(mine, wr, 0.0).astype(BF16)
            wst_ref[0, 0, rows, half + col:half + col + 128] = jnp.where(mine, wi, 0.0).astype(BF16)

    lane1 = lax.broadcasted_iota(jnp.int32, (1, 2 * S5_P), 1)

    def group_lanes(a):
        return jnp.concatenate([jnp.where(lane1 < S5_P, a[2 * q], a[2 * q + 1])
                                for q in range(S5_GB // 2)], axis=1)

    pr, pi_ = pwr[:, S5_L:S5_L + 1], pwi[:, S5_L:S5_L + 1]
    for k in range(S5_NPOW):
        apr_ref[0, 0, k:k + 1, :] = group_lanes(pr)
        api_ref[0, 0, k:k + 1, :] = group_lanes(pi_)
        pr, pi_ = _cmul(pr, pi_, pr, pi_)

    tn = (((0,), (0,)), ((), ()))
    kks, wre, wim = [], [], []
    for g in range(S5_GB):
        pg_r, pg_i = pwr[g, :, :S5_P], pwi[g, :, :S5_P]
        cr, ci = cxr_ref[0, g], cxi_ref[0, g]

        def c_times_pow(sel):
            er = lax.dot_general(pg_r, sel, tn, precision=HIGHEST, preferred_element_type=F32)
            ei = lax.dot_general(pg_i, sel, tn, precision=HIGHEST, preferred_element_type=F32)
            return _cmul(cr, ci, er, ei)

        k_r, k_i = c_times_pow(sel0_ref[...])
        kks.append(jnp.dot(bbr[g, :, :S5_P], k_r, precision=HIGHEST, preferred_element_type=F32)
                   - jnp.dot(bbi[g, :, :S5_P], k_i, precision=HIGHEST, preferred_element_type=F32))
        o_r, o_i = c_times_pow(sel1_ref[...])
        wre.append(o_r)
        wim.append(-o_i)
    glane = (lax.broadcasted_iota(jnp.int32, (1, S5_XW), 1) % 128) // S5_H

    def spread(parts, rows_per_group):
        a = jnp.concatenate(parts, axis=0).astype(BF16)
        a = jnp.dot(a, exp_ref[...], preferred_element_type=F32)
        grow = lax.broadcasted_iota(jnp.int32, (a.shape[0], 1), 0) // rows_per_group % S5_GB
        return jnp.where(grow == glane, a, 0.0).astype(BF16)

    wout_ref[0, 0] = spread(wre + wim, S5_P)
    kk_ref[0, 0] = spread(kks, S5_H)


def s5_prepare(lam_re, lam_im, log_dt, b_re, b_im, c_re, c_im):
    half = S5_SW // 2
    lr = jnp.tile(lam_re.reshape(2, S5_G, 1, S5_P), (1, 1, 1, 2))
    li = jnp.tile(lam_im.reshape(2, S5_G, 1, S5_P), (1, 1, 1, 2))
    ldt = log_dt.reshape(2, S5_G, 1, 1)
    btr = jnp.tile(jnp.swapaxes(b_re, 2, 3), (1, 1, 1, 2))
    bti = jnp.tile(jnp.swapaxes(b_im, 2, 3), (1, 1, 1, 2))
    cxr = jnp.tile(jnp.swapaxes(c_re, 2, 3), (1, 1, 1, S5_L))
    cxi = jnp.tile(jnp.swapaxes(c_im, 2, 3), (1, 1, 1, S5_L))
    k = jnp.arange(S5_NPWR)[:, None]
    t = (jnp.arange(S5_LH) // S5_H)[None, :]
    sel0 = (k == t).astype(F32)
    sel1 = (k == t + 1).astype(F32)
    src = jnp.arange(S5_LH)[:, None]
    dst = jnp.arange(S5_XW)[None, :]
    expand = ((src // S5_H == dst // 128) & (src % S5_H == dst % S5_H)).astype(BF16)

    def spec(*tail):
        return pl.BlockSpec((1, S5_GB) + tail, lambda d, j: (d, j) + (0,) * len(tail))

    def const(shape):
        return pl.BlockSpec(shape, lambda d, j: (0,) * len(shape))

    def blk(*tail):
        return pl.BlockSpec((1, 1) + tail, lambda d, j: (d, j) + (0,) * len(tail))

    return pl.pallas_call(
        _s5_prep_kernel,
        grid=(2, S5_NB),
        in_specs=[spec(1, 2 * S5_P), spec(1, 2 * S5_P), spec(1, 1),
                  spec(S5_H, 2 * S5_P), spec(S5_H, 2 * S5_P), spec(S5_P, S5_LH), spec(S5_P, S5_LH),
                  const((S5_NPWR, S5_LH)), const((S5_NPWR, S5_LH)), const((S5_LH, S5_XW))],
        out_specs=[blk(128, S5_XW), blk(S5_XW, S5_SW), blk(S5_SW, S5_XW),
                   blk(S5_NPOW, half), blk(S5_NPOW, half)],
        out_shape=[jax.ShapeDtypeStruct((2, S5_NB, 128, S5_XW), BF16),
                   jax.ShapeDtypeStruct((2, S5_NB, S5_XW, S5_SW), BF16),
                   jax.ShapeDtypeStruct((2, S5_NB, S5_SW, S5_XW), BF16),
                   jax.ShapeDtypeStruct((2, S5_NB, S5_NPOW, half), F32),
                   jax.ShapeDtypeStruct((2, S5_NB, S5_NPOW, half), F32)],
        compiler_params=_cparams("parallel", "parallel"),
        name="s5_prepare",
    )(lr, li, ldt, btr, bti, cxr, cxi, sel0, sel1, expand)


def _s5_scan_body(d, r, h_ref, kk_ref, wst_ref, wout_ref, apr_ref, api_ref, h0r_ref, h0i_ref,
                  y_ref, fr_ref, fi_ref, m8, zr_s, zi_s):
    half = S5_SW // 2

    @pl.when(r == 0)
    def _build():
        for s in range(S5_L):
            if s:
                m8[128 * s:128 * (s + 1), 0:128 * s] = jnp.zeros((128, 128 * s), BF16)
            m8[128 * s:128 * (s + 1), 128 * s:] = kk_ref[0, 0, :, :S5_XW - 128 * s]

    def tloc(s):
        return s if d == 0 else S5_L - 1 - s

    slabs = [h_ref[pl.ds(tloc(s), S5_CH, stride=S5_L), :] for s in range(S5_L)]
    x8 = jnp.concatenate(slabs, axis=1).astype(BF16)
    delta = jnp.dot(x8, wst_ref[0, 0], preferred_element_type=F32)
    zr, zi = delta[:, :half], delta[:, half:]

    row = lax.broadcasted_iota(jnp.int32, (S5_CH, 1), 0)
    is_ctx = r == 0
    pos = jnp.where(is_ctx, row & (S5_CTX_CH - 1), row)
    last = jnp.where(is_ctx, S5_CTX_CH - 1, S5_CH - 1)
    a = jnp.maximum(r - 1, 0)
    lat = (r > 0).astype(F32)
    h0r = h0r_ref[0, 0, pl.ds(a, 1), :] * lat
    h0i = h0i_ref[0, 0, pl.ds(a, 1), :] * lat
    first = (pos == 0) if d == 0 else (pos == last)
    ir, ii = _cmul(apr_ref[0, 0, 0:1, :], api_ref[0, 0, 0:1, :], h0r, h0i)
    zr = zr + jnp.where(first, ir, 0.0)
    zi = zi + jnp.where(first, ii, 0.0)
    for k in range(S5_NPOW):
        m = 1 << k
        akr, aki = apr_ref[0, 0, k:k + 1, :], api_ref[0, 0, k:k + 1, :]
        if d == 0:
            sr, si = pltpu.roll(zr, m, 0), pltpu.roll(zi, m, 0)
            valid = pos >= m
        else:
            sr, si = pltpu.roll(zr, S5_CH - m, 0), pltpu.roll(zi, S5_CH - m, 0)
            valid = pos <= last - m
        pr, pi_ = _cmul(akr, aki, sr, si)
        zr = zr + jnp.where(valid, pr, 0.0)
        zi = zi + jnp.where(valid, pi_, 0.0)
    if d == 0:
        sr, si = pltpu.roll(zr, 1, 0), pltpu.roll(zi, 1, 0)
    else:
        sr, si = pltpu.roll(zr, S5_CH - 1, 0), pltpu.roll(zi, S5_CH - 1, 0)
    sr = jnp.where(first, h0r, sr)
    si = jnp.where(first, h0i, si)
    s_in = jnp.concatenate([sr, si], axis=1).astype(BF16)
    y8 = (jnp.dot(x8, m8[...], preferred_element_type=F32)
          + jnp.dot(s_in, wout_ref[0, 0], preferred_element_type=F32))
    for t in range(S5_L):
        y_ref[0, pl.ds(tloc(t), S5_CH, stride=S5_L), :] = y8[:, 128 * t:128 * (t + 1)]

    @pl.when(r == 0)
    def _fin():
        off = S5_CTX_CH - 1 if d == 0 else 0
        for q in range(half // 128):
            zr_s[q] = zr[:, 128 * q:128 * (q + 1)]
            zi_s[q] = zi[:, 128 * q:128 * (q + 1)]
            fr_ref[0, 0, :, 128 * q:128 * (q + 1)] = zr_s[q, pl.ds(off, N_CTX_SEQ, stride=S5_CTX_CH), :]
            fi_ref[0, 0, :, 128 * q:128 * (q + 1)] = zi_s[q, pl.ds(off, N_CTX_SEQ, stride=S5_CTX_CH), :]


def _s5_scan_kernel(*refs):
    d = pl.program_id(1)
    r = pl.program_id(2)

    @pl.when(d == 0)
    def _():
        _s5_scan_body(0, r, *refs)

    @pl.when(d == 1)
    def _():
        _s5_scan_body(1, r, *refs)


def s5_scan(h, kk8, wst8, wout8, apr, api, h0r, h0i):
    half = S5_SW // 2
    return pl.pallas_call(
        _s5_scan_kernel,
        grid=(S5_NB, 2, N_GROUPS),
        in_specs=[pl.BlockSpec((GROUP_ROWS, 128), lambda j, d, r: (r, j)),
                  pl.BlockSpec((1, 1, 128, S5_XW), lambda j, d, r: (d, j, 0, 0)),
                  pl.BlockSpec((1, 1, S5_XW, S5_SW), lambda j, d, r: (d, j, 0, 0)),
                  pl.BlockSpec((1, 1, S5_SW, S5_XW), lambda j, d, r: (d, j, 0, 0)),
                  pl.BlockSpec((1, 1, S5_NPOW, half), lambda j, d, r: (d, j, 0, 0)),
                  pl.BlockSpec((1, 1, S5_NPOW, half), lambda j, d, r: (d, j, 0, 0)),
                  pl.BlockSpec((1, 1, N_LAT_SEQ, half), lambda j, d, r: (d, j, 0, 0)),
                  pl.BlockSpec((1, 1, N_LAT_SEQ, half), lambda j, d, r: (d, j, 0, 0))],
        out_specs=[pl.BlockSpec((1, GROUP_ROWS, 128), lambda j, d, r: (d, r, j)),
                   pl.BlockSpec((1, 1, N_CTX_SEQ, half), lambda j, d, r: (d, j, 0, 0)),
                   pl.BlockSpec((1, 1, N_CTX_SEQ, half), lambda j, d, r: (d, j, 0, 0))],
        out_shape=[jax.ShapeDtypeStruct((2, N_TOK, D), F32),
                   jax.ShapeDtypeStruct((2, S5_NB, N_CTX_SEQ, half), F32),
                   jax.ShapeDtypeStruct((2, S5_NB, N_CTX_SEQ, half), F32)],
        scratch_shapes=[pltpu.VMEM((S5_XW, S5_XW), BF16),
                        pltpu.VMEM((half // 128, S5_CH, 128), F32),
                        pltpu.VMEM((half // 128, S5_CH, 128), F32)],
        compiler_params=_cparams("arbitrary", "arbitrary", "arbitrary"),
        name="s5_scan",
    )(h, kk8, wst8, wout8, apr, api, h0r, h0i)


def _s5_glu_kernel(h_ref, y0_ref, y1_ref, dsk_ref, wa_ref, wb_ref, ba_ref, bb_ref, x_ref, g1_ref,
                   o_ref, yg_scr):
    @pl.when(pl.program_id(1) == 0)
    def _():
        y = dsk_ref[...] * h_ref[...] + y0_ref[0] + y1_ref[0]
        yg_scr[...] = jax.nn.gelu(y).astype(BF16)

    yg = yg_scr[...]
    a = jnp.dot(yg, wa_ref[...].astype(BF16), preferred_element_type=F32) + ba_ref[...]
    b = jnp.dot(yg, wb_ref[...].astype(BF16), preferred_element_type=F32) + bb_ref[...]
    o_ref[...] = x_ref[...] + g1_ref[0] * (a * jax.nn.sigmoid(b))


def s5_glu(h, y, d_skip, w_glu, b_glu, x, g1, tn=512):
    nj = D // tn
    b2 = b_glu.reshape(1, 2 * D)
    return pl.pallas_call(
        _s5_glu_kernel,
        grid=(N_TOK // TM, nj),
        in_specs=[pl.BlockSpec((TM, D), lambda i, j: (i, 0)),
                  pl.BlockSpec((1, TM, D), lambda i, j: (0, i, 0)),
                  pl.BlockSpec((1, TM, D), lambda i, j: (1, i, 0)),
                  pl.BlockSpec((1, D), lambda i, j: (0, 0)),
                  pl.BlockSpec((D, tn), lambda i, j: (0, j)),
                  pl.BlockSpec((D, tn), lambda i, j: (0, nj + j)),
                  pl.BlockSpec((1, tn), lambda i, j: (0, j)),
                  pl.BlockSpec((1, tn), lambda i, j: (0, nj + j)),
                  pl.BlockSpec((TM, tn), lambda i, j: (i, j)),
                  pl.BlockSpec((1, 1, tn), lambda i, j: (_group_of_tile(i), 0, j))],
        out_specs=pl.BlockSpec((TM, tn), lambda i, j: (i, j)),
        out_shape=jax.ShapeDtypeStruct((N_TOK, D), F32),
        scratch_shapes=[pltpu.VMEM((TM, D), BF16)],
        compiler_params=_cparams("parallel", "arbitrary"),
        name="s5_glu",
    )(h, y, y, d_skip.reshape(1, D), w_glu, w_glu, b2, b2, x, g1)


def s5_layer(x, mods, g_norm, st_re, st_im, lam_re, lam_im, log_dt, b_re, b_im, c_re, c_im,
             d_skip, w_glu, b_glu):
    sh1, sc1, g1 = mods
    h = normmod(x, g_norm, sc1, sh1)
    kk8, wst8, wout8, apr, api = s5_prepare(lam_re, lam_im, log_dt, b_re, b_im, c_re, c_im)

    def h0(st):
        return jnp.transpose(st.reshape(N_LAT_SEQ, 2, S5_NB, S5_GB * S5_P), (1, 2, 0, 3))

    y, fr, fi = s5_scan(h, kk8, wst8, wout8, apr, api, h0(st_re), h0(st_im))

    def fin(f):
        return jnp.transpose(f.reshape(2, S5_NB, N_CTX_SEQ, S5_GB, S5_P), (2, 0, 1, 3, 4)
                             ).reshape(N_CTX_SEQ, 2, S5_G, S5_P)

    x = s5_glu(h, y, d_skip, w_glu, b_glu, x, g1)
    return x, fin(fr), fin(fi)


CHUNK = 64
N_CHUNKS = N_TOK // CHUNK
CTX_CHUNKS = N_CTX_SEQ * CTX_LEN // CHUNK
CH_PER_CTX = CTX_LEN // CHUNK
CH_PER_LAT = LAT_LEN // CHUNK


def _chunk_of_step(d, c):
    return jnp.where(d == 0, c, N_CHUNKS - 1 - c)


def _seq_of_chunk(ce):
    return jnp.where(ce < CTX_CHUNKS, ce // CH_PER_CTX, N_CTX_SEQ + (ce - CTX_CHUNKS) // CH_PER_LAT)


def _chunk_flags(d, ce):
    is_ctx = ce < CTX_CHUNKS
    pos = jnp.where(is_ctx, ce % CH_PER_CTX, (ce - CTX_CHUNKS) % CH_PER_LAT)
    n = jnp.where(is_ctx, CH_PER_CTX, CH_PER_LAT)
    t_first, t_last = pos == 0, pos == n - 1
    if d == 0:
        return is_ctx, t_first, t_last
    return is_ctx, t_last, t_first


def _tri(d, shape, row_axis=0, col_axis=1):
    r = lax.broadcasted_iota(jnp.int32, shape, row_axis)
    c = lax.broadcasted_iota(jnp.int32, shape, col_axis)
    return (r >= c) if d == 0 else (r <= c)


HG_H = 8
HG_K = 128
HG_SAFE_SPAN = 60.0


def _hg_decay(d, layer, z_ref, lbl_ref):
    lg = lbl_ref[d]
    e = jnp.exp(lg - jnp.max(lg, axis=0, keepdims=True))
    sm = e / jnp.sum(e, axis=0, keepdims=True)
    lb = jnp.sum(sm[1:layer + 1], axis=0, keepdims=True)
    f = lb + (1.0 - lb) * jax.nn.sigmoid(z_ref[...])
    g = jnp.log(f)
    g_hi = g.astype(BF16)
    g_lo = (g - g_hi.astype(F32)).astype(BF16)
    tri = _tri(d, (CHUNK, CHUNK)).astype(BF16)
    cum = (jnp.dot(tri, g_hi, preferred_element_type=F32)
           + jnp.dot(tri, g_lo, preferred_element_type=F32))
    return 1.0 - f, cum


def _hg_main(d, layer, q_ref, v_ref, z_ref, lbl_ref, o_ref, st_scr, inter_scr):
    kk, cum = _hg_decay(d, layer, z_ref, lbl_ref)
    tri = _tri(d, (CHUNK, CHUNK))
    head, tot = (cum[0:1], cum[CHUNK - 1:CHUNK]) if d == 0 else (cum[CHUNK - 1:CHUNK], cum[0:1])
    mid = cum[CHUNK // 2 - 1:CHUNK // 2]
    q = q_ref[...]
    v = v_ref[...]
    qa = (q * jnp.exp(cum - mid)).astype(BF16)
    ka = (kk * jnp.exp(mid - cum)).astype(BF16)
    qs = (q * jnp.exp(cum)).astype(BF16)
    kd = (kk * jnp.exp(tot - cum)).astype(BF16)
    vb = v.astype(BF16)
    etot = jnp.exp(tot)
    nt = (((1,), (1,)), ((), ()))
    tn = (((0,), (0,)), ((), ()))
    for hd in range(HG_H):
        sl = slice(HG_K * hd, HG_K * (hd + 1))
        a = lax.dot_general(qa[:, sl], ka[:, sl], nt, preferred_element_type=F32)
        a = jnp.where(tri, a, 0.0).astype(BF16)
        st = st_scr[hd]
        inter = lax.dot_general(qs[:, sl], st.astype(BF16), nt, preferred_element_type=F32)
        inter_scr[:, sl] = inter
        o_ref[:, sl] = jnp.dot(a, vb[:, sl], preferred_element_type=F32) + inter
        st_scr[hd] = st * etot[:, sl] + lax.dot_general(vb[:, sl], kd[:, sl], tn,
                                                        preferred_element_type=F32)

    return jnp.max(jnp.maximum(head - mid, mid - tot))


def _hg_exact_intra(d, layer, q_ref, v_ref, z_ref, lbl_ref, o_ref, inter_scr, cum_scr, k_scr):
    kk, cum = _hg_decay(d, layer, z_ref, lbl_ref)
    cum_scr[...] = cum
    k_scr[...] = kk
    q = q_ref[...]
    c_idx = lax.broadcasted_iota(jnp.int32, (D, 128), 0) // HG_K
    h_idx = lax.broadcasted_iota(jnp.int32, (D, 128), 1)
    head_sum = (c_idx == h_idx).astype(F32)
    c_idx_t = lax.broadcasted_iota(jnp.int32, (128, D), 1) // HG_K
    h_idx_t = lax.broadcasted_iota(jnp.int32, (128, D), 0)
    head_bcast = (c_idx_t == h_idx_t).astype(F32)
    row = lax.broadcasted_iota(jnp.int32, (CHUNK, 1), 0)

    def source_row(s, acc):
        seen = (row >= s) if d == 0 else (row <= s)
        w = jnp.exp(jnp.where(seen, cum - cum_scr[pl.ds(s, 1), :], -jnp.inf))
        p = q * k_scr[pl.ds(s, 1), :] * w
        a_s = jnp.dot(p, head_sum, precision=HIGHEST, preferred_element_type=F32)
        a_s = jnp.dot(a_s, head_bcast, precision=HIGHEST, preferred_element_type=F32)
        return acc + a_s * v_ref[pl.ds(s, 1), :]

    intra = lax.fori_loop(0, CHUNK, source_row, jnp.zeros((CHUNK, D), F32))
    o_ref[...] = inter_scr[...] + intra


def _hg_scan_kernel(q0, v0, z0, q1, v1, z1, lbl_ref, s00, s01, o0, o1, fin0, fin1,
                    st0, st1, inter0, inter1, cum_scr, k_scr, *, layer):
    c = pl.program_id(0)
    dirs = ((0, q0, v0, z0, s00, o0, fin0, st0, inter0), (1, q1, v1, z1, s01, o1, fin1, st1, inter1))
    flags = [_chunk_flags(d, _chunk_of_step(d, c)) for d in (0, 1)]

    def set_state(st, value_fn):
        st[...] = value_fn()

    for (d, q, v, z, s0, o, fin, st, inter), (is_ctx, starts, ends) in zip(dirs, flags):
        pl.when(starts & is_ctx)(functools.partial(set_state, st, lambda: jnp.zeros((HG_H, HG_K, HG_K), F32)))
        pl.when(starts & jnp.logical_not(is_ctx))(functools.partial(set_state, st, lambda s0=s0: s0[0, 0]))
    spans = [_hg_main(d, layer, q, v, z, lbl_ref, o, st, inter)
             for (d, q, v, z, s0, o, fin, st, inter) in dirs]
    for (d, q, v, z, s0, o, fin, st, inter), span in zip(dirs, spans):
        pl.when(span > HG_SAFE_SPAN)(functools.partial(
            _hg_exact_intra, d, layer, q, v, z, lbl_ref, o, inter, cum_scr, k_scr))

    def write_final(fin, st):
        for hd in range(HG_H):
            fin[0, hd] = st[hd].T

    for (d, q, v, z, s0, o, fin, st, inter), (is_ctx, starts, ends) in zip(dirs, flags):
        pl.when(ends)(functools.partial(write_final, fin, st))


def hg_scan(proj, lb_logits, s0t, layer):
    def tok(d, col):
        return lambda c: (_chunk_of_step(d, c), col)

    def lat_idx(d):
        return lambda c: (d, jnp.maximum(_seq_of_chunk(_chunk_of_step(d, c)) - N_CTX_SEQ, 0), 0, 0, 0)

    def fin_idx(d):
        return lambda c: (jnp.minimum(_seq_of_chunk(_chunk_of_step(d, c)), N_CTX_SEQ), 0, 0, 0)

    state = (HG_H, HG_K, HG_K)
    return pl.pallas_call(
        functools.partial(_hg_scan_kernel, layer=layer),
        grid=(N_CHUNKS,),
        in_specs=[pl.BlockSpec((CHUNK, D), tok(0, 0)), pl.BlockSpec((CHUNK, D), tok(0, 1)),
                  pl.BlockSpec((CHUNK, D), tok(0, 3)),
                  pl.BlockSpec((CHUNK, D), tok(1, 0)), pl.BlockSpec((CHUNK, D), tok(1, 1)),
                  pl.BlockSpec((CHUNK, D), tok(1, 4)),
                  pl.BlockSpec((2, DEPTH, D), lambda c: (0, 0, 0)),
                  pl.BlockSpec((1, 1) + state, lat_idx(0)), pl.BlockSpec((1, 1) + state, lat_idx(1))],
        out_specs=[pl.BlockSpec((CHUNK, D), tok(0, 0)), pl.BlockSpec((CHUNK, D), tok(1, 0)),
                   pl.BlockSpec((1,) + state, fin_idx(0)), pl.BlockSpec((1,) + state, fin_idx(1))],
        out_shape=[jax.ShapeDtypeStruct((N_TOK, D), F32), jax.ShapeDtypeStruct((N_TOK, D), F32),
                   jax.ShapeDtypeStruct((N_CTX_SEQ + 1,) + state, F32),
                   jax.ShapeDtypeStruct((N_CTX_SEQ + 1,) + state, F32)],
        scratch_shapes=[pltpu.VMEM(state, F32), pltpu.VMEM(state, F32),
                        pltpu.VMEM((CHUNK, D), F32), pltpu.VMEM((CHUNK, D), F32),
                        pltpu.VMEM((CHUNK, D), F32), pltpu.VMEM((CHUNK, D), F32)],
        compiler_params=_cparams("arbitrary"),
        name="hg_scan",
    )(proj, proj, proj, proj, proj, proj, lb_logits, s0t, s0t)


def _hg_out_kernel(o0_ref, o1_ref, gate_ref, gn_ref, w_ref, x_ref, g1_ref, out_ref, on_scr):
    @pl.when(pl.program_id(1) == 0)
    def _():
        for hd in range(HG_H):
            sl = slice(HG_K * hd, HG_K * (hd + 1))
            o = o0_ref[:, sl] + o1_ref[:, sl]
            o = o * lax.rsqrt(jnp.mean(o * o, axis=-1, keepdims=True) + NORM_EPS) * gn_ref[...]
            on_scr[:, sl] = (o * _silu(gate_ref[:, sl])).astype(BF16)

    out_ref[...] = x_ref[...] + g1_ref[0] * jnp.dot(on_scr[...], w_ref[...].astype(BF16),
                                                    preferred_element_type=F32)


def hg_out(o_fwd, o_bwd, proj, g_norm, w_o, x, g1, tn=512):
    return pl.pallas_call(
        _hg_out_kernel,
        grid=(N_TOK // TM, D // tn),
        in_specs=[pl.BlockSpec((TM, D), lambda i, j: (i, 0)),
                  pl.BlockSpec((TM, D), lambda i, j: (i, 0)),
                  pl.BlockSpec((TM, D), lambda i, j: (i, 2)),
                  pl.BlockSpec((1, HG_K), lambda i, j: (0, 0)),
                  pl.BlockSpec((D, tn), lambda i, j: (0, j)),
                  pl.BlockSpec((TM, tn), lambda i, j: (i, j)),
                  pl.BlockSpec((1, 1, tn), lambda i, j: (_group_of_tile(i), 0, j))],
        out_specs=pl.BlockSpec((TM, tn), lambda i, j: (i, j)),
        out_shape=jax.ShapeDtypeStruct((N_TOK, D), F32),
        scratch_shapes=[pltpu.VMEM((TM, D), BF16)],
        compiler_params=_cparams("parallel", "arbitrary"),
        name="hg_out",
    )(o_fwd, o_bwd, proj, g_norm.reshape(1, HG_K), w_o, x, g1)


def hgrn_layer(x, mods, g_mix, state, layer, lb_logits, w_qig, w_f, b_f, g_norm, w_o):
    sh1, sc1, g1 = mods
    w5 = jnp.concatenate([w_qig, w_f[0], w_f[1]], axis=1)
    b5 = jnp.concatenate([jnp.zeros((3 * D,), F32), b_f[0], b_f[1]])
    proj = nm_matmul(x, g_mix, sc1, sh1, w5, b5, 5 * D, name="hg_proj")
    s0t = jnp.transpose(state, (1, 0, 2, 4, 3))
    o_fwd, o_bwd, fin_fwd, fin_bwd = hg_scan(proj, lb_logits, s0t, layer)
    x = hg_out(o_fwd, o_bwd, proj, g_norm, w_o, x, g1)
    return x, jnp.stack([fin_fwd[:N_CTX_SEQ], fin_bwd[:N_CTX_SEQ]], axis=1)


SSD_INNER = 2 * D
SSD_HEADS = 32
SSD_P = 64
SSD_NG = 4
SSD_N = 128
SSD_XBC = SSD_INNER + 2 * SSD_NG * SSD_N
SSD_ZX = SSD_INNER + SSD_XBC
SSD_CONV = 5
CONV_TM = 256
CONV_HALO = 8


def _ssd_conv_kernel(cur_ref, prev_ref, next_ref, w_ref, b_ref, o_ref, ext):
    i = pl.program_id(0)
    n_ctx_tiles = N_CTX_SEQ * CTX_LEN // CONV_TM
    per_lat = LAT_LEN // CONV_TM
    is_ctx = i < n_ctx_tiles
    k = (i - n_ctx_tiles) % per_lat
    seq_start = is_ctx | (k == 0)
    seq_end = is_ctx | (k == per_lat - 1)
    ext[0:CONV_HALO] = jnp.where(seq_start, 0.0, prev_ref[...])
    ext[CONV_HALO:CONV_HALO + CONV_TM] = cur_ref[...]
    ext[CONV_HALO + CONV_TM:] = jnp.where(seq_end, 0.0, next_ref[...])
    acc = jnp.broadcast_to(b_ref[...], (CONV_TM, D))
    for t in range(SSD_CONV):
        acc = acc + w_ref[t:t + 1, :] * ext[pl.ds(CONV_HALO - SSD_CONV // 2 + t, CONV_TM), :]
    o_ref[...] = _silu(acc)


def ssd_conv(zx, conv_w, conv_b):
    nrb = N_TOK // CONV_HALO
    rpt = CONV_TM // CONV_HALO
    c0 = SSD_INNER // D
    return pl.pallas_call(
        _ssd_conv_kernel,
        grid=(N_TOK // CONV_TM, SSD_XBC // D),
        in_specs=[pl.BlockSpec((CONV_TM, D), lambda i, j: (i, c0 + j)),
                  pl.BlockSpec((CONV_HALO, D), lambda i, j: (jnp.maximum(i * rpt - 1, 0), c0 + j)),
                  pl.BlockSpec((CONV_HALO, D), lambda i, j: (jnp.minimum((i + 1) * rpt, nrb - 1), c0 + j)),
                  pl.BlockSpec((SSD_CONV, D), lambda i, j: (0, j)),
                  pl.BlockSpec((1, D), lambda i, j: (0, j))],
        out_specs=pl.BlockSpec((CONV_TM, D), lambda i, j: (i, j)),
        out_shape=jax.ShapeDtypeStruct((N_TOK, SSD_XBC), F32),
        scratch_shapes=[pltpu.VMEM((CONV_TM + 2 * CONV_HALO, D), F32)],
        compiler_params=_cparams("parallel", "parallel"),
        name="ssd_conv",
    )(zx, zx, zx, conv_w, conv_b.reshape(1, SSD_XBC))


def _ssd_main(d, xlo_ref, xhi_ref, bc_ref, dtr_ref, dtb_ref, alog_ref, y_ref, ht_scr):
    xr = dtr_ref[...] + dtb_ref[...]
    dt = jnp.maximum(xr, 0.0) + jnp.log(1.0 + jnp.exp(-jnp.abs(xr)))
    dta = dt * (-jnp.exp(alog_ref[...]))
    tri = _tri(d, (CHUNK, CHUNK))
    cum = jnp.dot(tri.astype(F32), dta, precision=HIGHEST, preferred_element_type=F32)
    r = lax.broadcasted_iota(jnp.int32, (CHUNK, 2 * CHUNK), 0)
    cc = lax.broadcasted_iota(jnp.int32, (CHUNK, 2 * CHUNK), 1)
    lo_half = cc < CHUNK
    ccm = jnp.where(lo_half, cc, cc - CHUNK)
    trit = (ccm >= r) if d == 0 else (ccm <= r)
    tn = (((0,), (0,)), ((), ()))
    nt = (((1,), (1,)), ((), ()))
    cumt_lo = lax.dot_general(dta, (trit & lo_half).astype(F32), tn, precision=HIGHEST,
                              preferred_element_type=F32)
    cumt_hi = lax.dot_general(dta, (trit & jnp.logical_not(lo_half)).astype(F32), tn,
                              precision=HIGHEST, preferred_element_type=F32)
    lane = lax.broadcasted_iota(jnp.int32, (CHUNK, 2 * SSD_P), 1)
    first_head = lane < SSD_P
    tri2 = (r >= ccm) if d == 0 else (r <= ccm)
    bc = bc_ref[...]
    for gq in range(SSD_NG):
        bg = bc[:, SSD_N * gq:SSD_N * (gq + 1)].astype(BF16)
        cg = bc[:, SSD_NG * SSD_N + SSD_N * gq:SSD_NG * SSD_N + SSD_N * (gq + 1)].astype(BF16)
        cb2 = lax.dot_general(cg, jnp.concatenate([bg, bg], axis=0), nt,
                              preferred_element_type=F32)
        for pp in range(4 * gq, 4 * gq + 4):
            h1 = SSD_HEADS * d + 2 * pp
            colp = jnp.where(first_head, cum[:, h1:h1 + 1], cum[:, h1 + 1:h1 + 2])
            rowp = cumt_lo[h1:h1 + 1, :] + cumt_hi[h1 + 1:h1 + 2, :]
            lmat = jnp.exp(jnp.where(tri2, colp - rowp, -jnp.inf))
            dtp = jnp.where(first_head, dt[:, h1:h1 + 1], dt[:, h1 + 1:h1 + 2])
            xref = xlo_ref if pp < 8 else xhi_ref
            c0 = 128 * (pp % 8)
            xdt = xref[:, c0:c0 + 128] * dtp
            rhs = jnp.concatenate([jnp.where(first_head, xdt, 0.0),
                                   jnp.where(first_head, 0.0, xdt)], axis=0).astype(BF16)
            y = jnp.dot((cb2 * lmat).astype(BF16), rhs, preferred_element_type=F32)
            ht = ht_scr[:, 128 * pp:128 * (pp + 1)]
            y = y + jnp.dot(cg, ht.astype(BF16), preferred_element_type=F32) * jnp.exp(colp)
            y_ref[:, 128 * pp:128 * (pp + 1)] = y
            totp = colp[CHUNK - 1:CHUNK] if d == 0 else colp[0:1]
            xw = (xdt * jnp.exp(totp - colp)).astype(BF16)
            ht_scr[:, 128 * pp:128 * (pp + 1)] = (
                ht * jnp.exp(totp) + lax.dot_general(bg, xw, tn, preferred_element_type=F32))


def _ssd_scan_kernel(xlo0, xhi0, bc0, dtr0, xlo1, xhi1, bc1, dtr1, dtb_ref, alog_ref, h00, h01,
                     y0, y1, fin0, fin1, ht0, ht1):
    c = pl.program_id(0)
    dirs = ((0, xlo0, xhi0, bc0, dtr0, h00, y0, fin0, ht0), (1, xlo1, xhi1, bc1, dtr1, h01, y1, fin1, ht1))
    flags = [_chunk_flags(d, _chunk_of_step(d, c)) for d in (0, 1)]

    def set_state(ht, value_fn):
        ht[...] = value_fn()

    for (d, xlo, xhi, bc, dtr, h0, y, fin, ht), (is_ctx, starts, ends) in zip(dirs, flags):
        pl.when(starts & is_ctx)(functools.partial(
            set_state, ht, lambda: jnp.zeros((SSD_N, SSD_HEADS * SSD_P), F32)))
        pl.when(starts & jnp.logical_not(is_ctx))(functools.partial(set_state, ht, lambda h0=h0: h0[0, 0]))
    for (d, xlo, xhi, bc, dtr, h0, y, fin, ht) in dirs:
        _ssd_main(d, xlo, xhi, bc, dtr, dtb_ref, alog_ref, y, ht)
    def write_final(fin, ht):
        fin[0] = ht[...]

    for (d, xlo, xhi, bc, dtr, h0, y, fin, ht), (is_ctx, starts, ends) in zip(dirs, flags):
        pl.when(ends)(functools.partial(write_final, fin, ht))


def ssd_scan(xbc, dtr, dt_bias, a_log, h0t):
    nh2 = 2 * SSD_HEADS
    hp = SSD_HEADS * SSD_P

    def tok(d, col):
        return lambda c: (_chunk_of_step(d, c), col)

    def lat_idx(d):
        return lambda c: (d, jnp.maximum(_seq_of_chunk(_chunk_of_step(d, c)) - N_CTX_SEQ, 0), 0, 0)

    def fin_idx(d):
        return lambda c: (jnp.minimum(_seq_of_chunk(_chunk_of_step(d, c)), N_CTX_SEQ), 0, 0)

    def tok_specs(d):
        return [pl.BlockSpec((CHUNK, D), tok(d, 0)), pl.BlockSpec((CHUNK, D), tok(d, 1)),
                pl.BlockSpec((CHUNK, D), tok(d, 2)), pl.BlockSpec((CHUNK, nh2), tok(d, 0))]

    return pl.pallas_call(
        _ssd_scan_kernel,
        grid=(N_CHUNKS,),
        in_specs=tok_specs(0) + tok_specs(1) + [
            pl.BlockSpec((1, nh2), lambda c: (0, 0)), pl.BlockSpec((1, nh2), lambda c: (0, 0)),
            pl.BlockSpec((1, 1, SSD_N, hp), lat_idx(0)), pl.BlockSpec((1, 1, SSD_N, hp), lat_idx(1))],
        out_specs=[pl.BlockSpec((CHUNK, hp), tok(0, 0)), pl.BlockSpec((CHUNK, hp), tok(1, 0)),
                   pl.BlockSpec((1, SSD_N, hp), fin_idx(0)), pl.BlockSpec((1, SSD_N, hp), fin_idx(1))],
        out_shape=[jax.ShapeDtypeStruct((N_TOK, hp), F32), jax.ShapeDtypeStruct((N_TOK, hp), F32),
                   jax.ShapeDtypeStruct((N_CTX_SEQ + 1, SSD_N, hp), F32),
                   jax.ShapeDtypeStruct((N_CTX_SEQ + 1, SSD_N, hp), F32)],
        scratch_shapes=[pltpu.VMEM((SSD_N, hp), F32), pltpu.VMEM((SSD_N, hp), F32)],
        compiler_params=_cparams("arbitrary"),
        name="ssd_scan",
    )(xbc, xbc, xbc, dtr, xbc, xbc, xbc, dtr, dt_bias.reshape(1, nh2), a_log.reshape(1, nh2), h0t, h0t)


SSD_OUT_TM = 512


def _ssd_out_kernel(xlo_ref, xhi_ref, zlo_ref, zhi_ref, y0_ref, y1_ref, dsk_ref, gn_ref, w_ref,
                    x_ref, g1_ref, out_ref, yn_scr):
    @pl.when(pl.program_id(1) == 0)
    def _():
        halves = []
        ss = jnp.zeros((SSD_OUT_TM, 1), F32)
        for k, (xr, zr) in enumerate(((xlo_ref, zlo_ref), (xhi_ref, zhi_ref))):
            sl = slice(D * k, D * (k + 1))
            y = dsk_ref[:, sl] * xr[...] + y0_ref[:, sl] + y1_ref[:, sl]
            y = y * _silu(zr[...])
            ss = ss + jnp.sum(y * y, axis=-1, keepdims=True)
            halves.append(y)
        scale = lax.rsqrt(ss / SSD_INNER + NORM_EPS)
        for k, y in enumerate(halves):
            sl = slice(D * k, D * (k + 1))
            yn_scr[:, sl] = (y * scale * gn_ref[:, sl]).astype(BF16)

    out_ref[...] = x_ref[...] + g1_ref[0] * jnp.dot(yn_scr[...], w_ref[...].astype(BF16),
                                                    preferred_element_type=F32)


def ssd_out(xbc, zx, y_fwd, y_bwd, d_skip_cols, g_norm, w_out, x, g1, tn=512):
    tm = SSD_OUT_TM
    tpg = GROUP_ROWS // tm
    return pl.pallas_call(
        _ssd_out_kernel,
        grid=(N_TOK // tm, D // tn),
        in_specs=[pl.BlockSpec((tm, D), lambda i, j: (i, 0)),
                  pl.BlockSpec((tm, D), lambda i, j: (i, 1)),
                  pl.BlockSpec((tm, D), lambda i, j: (i, 0)),
                  pl.BlockSpec((tm, D), lambda i, j: (i, 1)),
                  pl.BlockSpec((tm, SSD_INNER), lambda i, j: (i, 0)),
                  pl.BlockSpec((tm, SSD_INNER), lambda i, j: (i, 0)),
                  pl.BlockSpec((1, SSD_INNER), lambda i, j: (0, 0)),
                  pl.BlockSpec((1, SSD_INNER), lambda i, j: (0, 0)),
                  pl.BlockSpec((SSD_INNER, tn), lambda i, j: (0, j)),
                  pl.BlockSpec((tm, tn), lambda i, j: (i, j)),
                  pl.BlockSpec((1, 1, tn), lambda i, j: (i // tpg, 0, j))],
        out_specs=pl.BlockSpec((tm, tn), lambda i, j: (i, j)),
        out_shape=jax.ShapeDtypeStruct((N_TOK, D), F32),
        scratch_shapes=[pltpu.VMEM((tm, SSD_INNER), BF16)],
        compiler_params=_cparams("parallel", "arbitrary"),
        name="ssd_out",
    )(xbc, xbc, zx, zx, y_fwd, y_bwd, d_skip_cols, g_norm.reshape(1, SSD_INNER), w_out, x, g1)


def ssd_layer(x, mods, g_mix, state, w_in, conv_w, conv_b, dt_bias, a_log, d_skip, g_norm, w_out):
    sh1, sc1, g1 = mods
    zx = nm_matmul(x, g_mix, sc1, sh1, w_in, jnp.zeros((SSD_ZX,), F32), SSD_ZX, name="ssd_proj")
    nh2 = 2 * SSD_HEADS
    dtr = nm_matmul(x, g_mix, sc1, sh1, w_in[:, SSD_ZX:], jnp.zeros((nh2,), F32), nh2, tn=nh2,
                    name="ssd_proj_dt")
    xbc = ssd_conv(zx, conv_w, conv_b)
    h0t = jnp.transpose(state, (1, 0, 4, 2, 3)).reshape(2, N_LAT_SEQ, SSD_N, SSD_HEADS * SSD_P)
    y_fwd, y_bwd, fin_fwd, fin_bwd = ssd_scan(xbc, dtr, dt_bias, a_log, h0t)
    dcols = jnp.repeat(d_skip, SSD_P).reshape(1, SSD_INNER)
    x = ssd_out(xbc, zx, y_fwd, y_bwd, dcols, g_norm, w_out, x, g1)
    fin = jnp.stack([fin_fwd[:N_CTX_SEQ], fin_bwd[:N_CTX_SEQ]], axis=1)
    fin = jnp.transpose(fin.reshape(N_CTX_SEQ, 2, SSD_N, SSD_HEADS, SSD_P), (0, 1, 3, 4, 2))
    return x, fin


N_EXP = 16
FF = 2 * D
CAP_CTX = 2 * CTX_LEN // N_EXP
CAP_LAT = 2 * LAT_LEN // N_EXP
SLOTS_PER_GROUP = 512
SLOTS = N_GROUPS * SLOTS_PER_GROUP


def _router_kernel(x_ref, g_ref, sc_ref, sh_ref, wt_ref, h_ref, aff_ref):
    h = _normmod(x_ref[...], g_ref[...], sc_ref[0], sh_ref[0])
    h_ref[...] = h.astype(BF16)
    logits = lax.dot_general(wt_ref[...], h, (((1,), (1,)), ((), ())), precision=HIGHEST,
                             preferred_element_type=F32)
    e = jnp.exp(logits - jnp.max(logits, axis=0, keepdims=True))
    aff_ref[...] = e / jnp.sum(e, axis=0, keepdims=True)


def moe_route(x, g, sc, sh, w_router_t):
    return pl.pallas_call(
        _router_kernel,
        grid=(N_TOK // TM,),
        in_specs=[pl.BlockSpec((TM, D), lambda i: (i, 0)),
                  pl.BlockSpec((1, D), lambda i: (0, 0)),
                  pl.BlockSpec((1, 1, D), lambda i: (_group_of_tile(i), 0, 0)),
                  pl.BlockSpec((1, 1, D), lambda i: (_group_of_tile(i), 0, 0)),
                  pl.BlockSpec((N_EXP, D), lambda i: (0, 0))],
        out_specs=[pl.BlockSpec((TM, D), lambda i: (i, 0)),
                   pl.BlockSpec((N_EXP, TM), lambda i: (0, i))],
        out_shape=[jax.ShapeDtypeStruct((N_TOK, D), BF16),
                   jax.ShapeDtypeStruct((N_EXP, N_TOK), F32)],
        compiler_params=_cparams("parallel"),
        name="moe_router",
    )(x, g.reshape(1, D), sc, sh, w_router_t)


def _lane_prefix_excl(m):
    s, t = m.shape
    r = lax.broadcasted_iota(jnp.int32, (128, 128), 0)
    c = lax.broadcasted_iota(jnp.int32, (128, 128), 1)
    upper = (r <= c).astype(BF16)
    run = jnp.zeros((s, 1), F32)
    out = []
    for k in range(t // 128):
        blk = m[:, 128 * k:128 * (k + 1)]
        inc = jnp.dot(blk.astype(BF16), upper, preferred_element_type=F32) + run
        out.append(inc - blk)
        run = inc[:, 127:128]
    return jnp.concatenate(out, axis=1)


def _select_kernel(a_ref, off_ref, slot_ref, base_ref, *, cap):
    bits = pltpu.bitcast(a_ref[...], jnp.int32)
    s = bits.shape[0]
    capf = float(cap)

    def body(_, lohi):
        lo, hi = lohi
        mid = lo + ((hi - lo + 1) >> 1)
        cnt = jnp.sum((bits >= mid).astype(F32), axis=1, keepdims=True)
        ok = cnt >= capf
        return jnp.where(ok, mid, lo), jnp.where(ok, hi, mid - 1)

    lo0 = jnp.zeros((s, 1), jnp.int32)
    hi0 = jnp.full((s, 1), 0x7F800000, jnp.int32)
    thr, _ = lax.fori_loop(0, 31, body, (lo0, hi0))
    gt = (bits > thr).astype(F32)
    eq = (bits == thr).astype(F32)
    need = capf - jnp.sum(gt, axis=1, keepdims=True)
    sel = gt + eq * (_lane_prefix_excl(eq) < need).astype(F32)
    slot = _lane_prefix_excl(sel) + off_ref[...]
    slot_ref[...] = jnp.where(sel > 0.0, slot, -1.0).astype(jnp.int32)
    t = bits.shape[1]
    tok = lax.broadcasted_iota(jnp.int32, (t, 128), 0)
    tile = lax.broadcasted_iota(jnp.int32, (t, 128), 1)
    ahead = (tok < tile * TOK_TILE).astype(BF16)
    base_ref[...] = jnp.dot(sel.astype(BF16), ahead, preferred_element_type=F32).astype(jnp.int32)


def moe_select(aff, off, cap):
    s, t = aff.shape
    return pl.pallas_call(
        functools.partial(_select_kernel, cap=cap),
        grid=(1,),
        in_specs=[pl.BlockSpec((s, t), lambda i: (0, 0)),
                  pl.BlockSpec((s, 1), lambda i: (0, 0))],
        out_specs=[pl.BlockSpec((s, t), lambda i: (0, 0)),
                   pl.BlockSpec((s, 128), lambda i: (0, 0))],
        out_shape=[jax.ShapeDtypeStruct((s, t), jnp.int32),
                   jax.ShapeDtypeStruct((s, 128), jnp.int32)],
        compiler_params=_cparams("arbitrary"),
        name="moe_select",
    )(aff, off)


TOK_TILE = 256
TILES = GROUP_ROWS // TOK_TILE
CNT_STRIDE = TILES + 1
SLOT_BLK = 128
SLOT_ALIGN = 16
GATHER_SPAN = 8


def _gather_kernel(cnt_ref, slot_ref, aff_ref, h_ref, xs_ref, gs_ref):
    base = (pl.program_id(0) * N_EXP + pl.program_id(1)) * CNT_STRIDE

    def gather(k, tiles):
        s = lax.broadcasted_iota(jnp.int32, (SLOT_BLK, TOK_TILE), 0) + SLOT_BLK * k
        acc = jnp.zeros((SLOT_BLK, D), F32)
        gate = jnp.zeros((SLOT_BLK, 1), F32)
        for t in tiles:
            oh = s == slot_ref[0, 0, pl.ds(t, 1), :]
            first = t * TOK_TILE
            rows = pl.ds(first if isinstance(t, int) else pl.multiple_of(first, TOK_TILE), TOK_TILE)
            acc = acc + jnp.dot(oh.astype(BF16), h_ref[rows, :], preferred_element_type=F32)
            gate = gate + jnp.sum(jnp.where(oh, aff_ref[0, pl.ds(t, 1), :], 0.0), axis=1, keepdims=True)
        out = slice(SLOT_BLK * k, SLOT_BLK * (k + 1))
        xs_ref[0, out, :] = acc.astype(BF16)
        gs_ref[0, out, :] = gate

    for k in range(SLOTS_PER_GROUP // SLOT_BLK):
        before = 0
        for j in range(1, TILES + 1):
            before = before + (cnt_ref[base + j] <= SLOT_BLK * k).astype(jnp.int32)
        t0 = jnp.minimum(before, TILES - GATHER_SPAN)
        covered = cnt_ref[base + t0 + GATHER_SPAN] >= SLOT_BLK * (k + 1)

        @pl.when(covered)
        def _():
            gather(k, [t0 + i for i in range(GATHER_SPAN)])

        @pl.when(jnp.logical_not(covered))
        def _():
            gather(k, range(TILES))


def moe_gather(cnt, slot, aff3, h2):
    return pl.pallas_call(
        _gather_kernel,
        grid_spec=pltpu.PrefetchScalarGridSpec(
            num_scalar_prefetch=1,
            grid=(N_GROUPS, N_EXP),
            in_specs=[pl.BlockSpec((1, 1, TILES, TOK_TILE), lambda r, e, c: (r, e, 0, 0)),
                      pl.BlockSpec((1, TILES, TOK_TILE), lambda r, e, c: (e, r, 0)),
                      pl.BlockSpec((GROUP_ROWS, D), lambda r, e, c: (r, 0))],
            out_specs=[pl.BlockSpec((1, SLOTS_PER_GROUP, D), lambda r, e, c: (e, r, 0)),
                       pl.BlockSpec((1, SLOTS_PER_GROUP, 1), lambda r, e, c: (e, r, 0))]),
        out_shape=[jax.ShapeDtypeStruct((N_EXP, SLOTS, D), BF16),
                   jax.ShapeDtypeStruct((N_EXP, SLOTS, 1), F32)],
        compiler_params=_cparams("parallel", "parallel"),
        name="moe_gather",
    )(cnt, slot, aff3, h2)


FF_TILE = 512
FFN_ROWS = 512


def _ffn_kernel(xs_ref, gs_ref, wg_ref, wu_ref, wd_ref, ys_ref, acc):
    f = pl.program_id(1)

    @pl.when(f == 0)
    def _():
        acc[...] = jnp.zeros_like(acc)

    wg = wg_ref[0, 0].astype(BF16)
    wu = wu_ref[0, 0].astype(BF16)
    wd = wd_ref[0, 0].astype(BF16)
    for rb in range(SLOTS // FFN_ROWS):
        rows = slice(FFN_ROWS * rb, FFN_ROWS * (rb + 1))
        x = xs_ref[0, rows, :]
        g = jnp.dot(x, wg, preferred_element_type=F32)
        u = jnp.dot(x, wu, preferred_element_type=F32)
        hid = (_silu(g) * u).astype(BF16)
        acc[rows, :] += jnp.dot(hid, wd, preferred_element_type=F32)

    @pl.when(f == FF // FF_TILE - 1)
    def _():
        ys_ref[0] = (acc[...] * gs_ref[0]).astype(BF16)


def moe_ffn(xs, gs, w_gate, w_up, w_down, layer):
    return pl.pallas_call(
        _ffn_kernel,
        grid=(N_EXP, FF // FF_TILE),
        in_specs=[pl.BlockSpec((1, SLOTS, D), lambda e, f: (e, 0, 0)),
                  pl.BlockSpec((1, SLOTS, 1), lambda e, f: (e, 0, 0)),
                  pl.BlockSpec((1, 1, D, FF_TILE), lambda e, f: (layer, e, 0, f)),
                  pl.BlockSpec((1, 1, D, FF_TILE), lambda e, f: (layer, e, 0, f)),
                  pl.BlockSpec((1, 1, FF_TILE, D), lambda e, f: (layer, e, f, 0))],
        out_specs=pl.BlockSpec((1, SLOTS, D), lambda e, f: (e, 0, 0)),
        out_shape=jax.ShapeDtypeStruct((N_EXP, SLOTS, D), BF16),
        scratch_shapes=[pltpu.VMEM((SLOTS, D), F32)],
        compiler_params=_cparams("parallel", "arbitrary"),
        name="moe_ffn",
    )(xs, gs, w_gate, w_up, w_down)


SCAT_TN = 512


def _scatter_kernel(cnt_ref, slot_ref, ys_ref, x_ref, g2_ref, o_ref):
    r, j = pl.program_id(0), pl.program_id(2)
    starts, covered = [], True
    for e in range(N_EXP):
        base = (r * N_EXP + e) * CNT_STRIDE + j
        s = jnp.minimum(cnt_ref[base] // SLOT_ALIGN * SLOT_ALIGN, SLOTS_PER_GROUP - SLOT_BLK)
        starts.append(pl.multiple_of(s, SLOT_ALIGN))
        covered = covered & (cnt_ref[base + 1] <= s + SLOT_BLK)

    def scatter(windows, n):
        lane = lax.broadcasted_iota(jnp.int32, (TOK_TILE, n), 1)
        acc = jnp.zeros((TOK_TILE, SCAT_TN), F32)
        for e in range(N_EXP):
            oh = (slot_ref[0, :, e:e + 1] == lane + windows[e]).astype(BF16)
            acc = acc + jnp.dot(oh, ys_ref[e, pl.ds(windows[e], n), :], preferred_element_type=F32)
        o_ref[...] = x_ref[...] + g2_ref[0] * acc

    @pl.when(covered)
    def _():
        scatter(starts, SLOT_BLK)

    @pl.when(jnp.logical_not(covered))
    def _():
        scatter([0] * N_EXP, SLOTS_PER_GROUP)


def moe_scatter(cnt, slot_t, ys, x, g2):
    return pl.pallas_call(
        _scatter_kernel,
        grid_spec=pltpu.PrefetchScalarGridSpec(
            num_scalar_prefetch=1,
            grid=(N_GROUPS, D // SCAT_TN, TILES),
            in_specs=[pl.BlockSpec((1, TOK_TILE, 128), lambda r, c, j, n: (r, j, 0)),
                      pl.BlockSpec((N_EXP, SLOTS_PER_GROUP, SCAT_TN), lambda r, c, j, n: (0, r, c)),
                      pl.BlockSpec((TOK_TILE, SCAT_TN), lambda r, c, j, n: (r * TILES + j, c)),
                      pl.BlockSpec((1, 1, SCAT_TN), lambda r, c, j, n: (r, 0, c))],
            out_specs=pl.BlockSpec((TOK_TILE, SCAT_TN), lambda r, c, j, n: (r * TILES + j, c))),
        out_shape=jax.ShapeDtypeStruct((N_TOK, D), F32),
        compiler_params=_cparams("parallel", "parallel", "arbitrary"),
        name="moe_scatter",
    )(cnt, slot_t, ys, x, g2)


def moe_layer(x, mods, g_ffn, layer, w_router, w_gate, w_up, w_down):
    sh2, sc2, g2 = mods
    h2, aff = moe_route(x, g_ffn, sc2, sh2, w_router.T)
    aff_ctx = aff[:, :GROUP_ROWS].reshape(N_EXP * N_CTX_SEQ, CTX_LEN)
    off_ctx = jnp.tile(jnp.arange(N_CTX_SEQ, dtype=F32) * CAP_CTX, N_EXP).reshape(-1, 1)
    slot_ctx, _ = moe_select(aff_ctx, off_ctx, CAP_CTX)
    aff_lat = jnp.transpose(aff[:, GROUP_ROWS:].reshape(N_EXP, N_LAT_SEQ, LAT_LEN), (1, 0, 2)
                            ).reshape(N_LAT_SEQ * N_EXP, LAT_LEN)
    slot_lat, base_lat = moe_select(aff_lat, jnp.zeros((N_LAT_SEQ * N_EXP, 1), F32), CAP_LAT)
    slot = jnp.concatenate([slot_ctx.reshape(1, N_EXP, GROUP_ROWS),
                            slot_lat.reshape(N_LAT_SEQ, N_EXP, LAT_LEN)], axis=0)
    slot_t = jnp.pad(jnp.transpose(slot, (0, 2, 1)), ((0, 0), (0, 0), (0, 128 - N_EXP)),
                     constant_values=-1)
    slot = slot.reshape(N_GROUPS, N_EXP, TILES, TOK_TILE)
    cnt_ctx = jnp.broadcast_to(jnp.arange(CNT_STRIDE, dtype=jnp.int32) * CAP_CTX, (1, N_EXP, CNT_STRIDE))
    cnt = jnp.concatenate([cnt_ctx, base_lat[:, :CNT_STRIDE].reshape(N_LAT_SEQ, N_EXP, CNT_STRIDE)],
                          axis=0).reshape(-1)
    xs, gs = moe_gather(cnt, slot, aff.reshape(N_EXP, N_TOK // TOK_TILE, TOK_TILE), h2)
    ys = moe_ffn(xs, gs, w_gate, w_up, w_down, layer)
    return moe_scatter(cnt, slot_t, ys, x, g2)


def _final_norm_kernel(x_ref, g_ref, ctx_ref, lat_ref):
    x = x_ref[...]
    y = x * lax.rsqrt(jnp.mean(x * x, axis=-1, keepdims=True) + NORM_EPS) * g_ref[...]
    is_ctx = pl.program_id(0) < TILES_PER_GROUP

    @pl.when(is_ctx)
    def _():
        ctx_ref[...] = y

    @pl.when(jnp.logical_not(is_ctx))
    def _():
        lat_ref[...] = y


def final_norm(x, g):
    t = TILES_PER_GROUP
    return pl.pallas_call(
        _final_norm_kernel,
        grid=(N_TOK // TM,),
        in_specs=[pl.BlockSpec((TM, D), lambda i: (i, 0)),
                  pl.BlockSpec((1, D), lambda i: (0, 0))],
        out_specs=[pl.BlockSpec((TM, D), lambda i: (jnp.minimum(i, t - 1), 0)),
                   pl.BlockSpec((TM, D), lambda i: (jnp.maximum(i - t, 0), 0))],
        out_shape=[jax.ShapeDtypeStruct((GROUP_ROWS, D), F32),
                   jax.ShapeDtypeStruct((N_TOK - GROUP_ROWS, D), F32)],
        compiler_params=_cparams("arbitrary"),
        name="final_norm",
    )(x, g.reshape(1, D))


def _grid_pos_embed():
    rows = LAT_LEN // GRID_W
    quarter = D // 4
    omega = 1.0 / (10000.0 ** (jnp.arange(quarter, dtype=F32) / quarter))
    r = jnp.arange(rows, dtype=F32)[:, None] * omega
    cl = jnp.arange(GRID_W, dtype=F32)[:, None] * omega
    emb_r = jnp.concatenate([jnp.sin(r), jnp.cos(r)], axis=-1)
    emb_c = jnp.concatenate([jnp.sin(cl), jnp.cos(cl)], axis=-1)
    emb = jnp.concatenate([jnp.broadcast_to(emb_r[:, None], (rows, GRID_W, D // 2)),
                           jnp.broadcast_to(emb_c[None], (rows, GRID_W, D // 2))], axis=-1)
    return emb.reshape(LAT_LEN, D)


def kernel(x_prompt, x_sample, state_s5_re, state_s5_im, state_hgrn, state_ssd, c, c_ctx, w_ada, b_ada, norm_mix, norm_ffn, norm_final, s5_lam_re, s5_lam_im, s5_log_dt, s5_b_re, s5_b_im, s5_c_re, s5_c_im, s5_d, s5_w_glu, s5_b_glu, hg_w_qig, hg_w_f, hg_b_f, hg_lb_logits, hg_norm, hg_w_o, ssd_w_in, ssd_conv_w, ssd_conv_b, ssd_dt_bias, ssd_a_log, ssd_d, ssd_norm, ssd_w_out, moe_router, moe_w_gate, moe_w_up, moe_w_down):
    cond8 = jnp.concatenate([c_ctx[None], c, jnp.zeros((5, D), F32)], axis=0)
    mod = ada_mod(cond8, w_ada, b_ada)
    mods = jnp.transpose(mod.reshape(DEPTH, 8, 6, D)[:, :3], (0, 2, 1, 3)).reshape(DEPTH, 6, 3, 1, D)
    x = embed_tokens(x_prompt.reshape(-1, D), x_sample.reshape(-1, D), _grid_pos_embed())
    s5_re, s5_im, hg_fin, ssd_fin = [], [], [], []
    for i in range(DEPTH):
        mix_mods = (mods[i, 0], mods[i, 1], mods[i, 2])
        kind, j = i % 3, i // 3
        if kind == 0:
            x, fr, fi = s5_layer(x, mix_mods, norm_mix[i], state_s5_re[:, j], state_s5_im[:, j],
                                 s5_lam_re[j], s5_lam_im[j], s5_log_dt[j], s5_b_re[j], s5_b_im[j],
                                 s5_c_re[j], s5_c_im[j], s5_d[j], s5_w_glu[j], s5_b_glu[j])
            s5_re.append(fr)
            s5_im.append(fi)
        elif kind == 1:
            x, fh = hgrn_layer(x, mix_mods, norm_mix[i], state_hgrn[:, j], i, hg_lb_logits,
                               hg_w_qig[j], hg_w_f[j], hg_b_f[j], hg_norm[j], hg_w_o[j])
            hg_fin.append(fh)
        else:
            x, fs = ssd_layer(x, mix_mods, norm_mix[i], state_ssd[:, j], ssd_w_in[j], ssd_conv_w[j],
                              ssd_conv_b[j], ssd_dt_bias[j], ssd_a_log[j], ssd_d[j], ssd_norm[j],
                              ssd_w_out[j])
            ssd_fin.append(fs)
        x = moe_layer(x, (mods[i, 3], mods[i, 4], mods[i, 5]), norm_ffn[i], i, moe_router[i],
                      moe_w_gate, moe_w_up, moe_w_down)
    y_ctx, y_lat = final_norm(x, norm_final)
    return (y_ctx.reshape(N_CTX_SEQ, CTX_LEN, D), y_lat.reshape(N_LAT_SEQ, LAT_LEN, D),
            jnp.stack(s5_re, axis=1), jnp.stack(s5_im, axis=1),
            jnp.stack(hg_fin, axis=1), jnp.stack(ssd_fin, axis=1))
```

```python
import functools
import math

import jax
import jax.numpy as jnp
from jax import lax
from jax.experimental import pallas as pl
from jax.experimental.pallas import tpu as pltpu

F32 = jnp.float32
BF16 = jnp.bfloat16
HIGHEST = lax.Precision.HIGHEST

D = 1024
DEPTH = 4
N_CTX_SEQ = 16
CTX_LEN = 256
N_LAT_SEQ = 2
LAT_LEN = 4096
GROUP_ROWS = 4096
N_GROUPS = 3
N_TOK = N_GROUPS * GROUP_ROWS
N_SEQ = N_CTX_SEQ + N_LAT_SEQ
NORM_EPS = 1e-6
GRID_W = 64

VMEM_LIMIT_BYTES = 56 * 1024 * 1024


def _cparams(*sem):
    return pltpu.CompilerParams(dimension_semantics=sem, vmem_limit_bytes=VMEM_LIMIT_BYTES)


def _silu(x):
    return x * jax.nn.sigmoid(x)


def _normmod(x, g, sc, sh):
    ms = jnp.mean(x * x, axis=-1, keepdims=True)
    return x * lax.rsqrt(ms + NORM_EPS) * g * (1.0 + sc) + sh


def _cmul(ar, ai, br, bi):
    return ar * br - ai * bi, ar * bi + ai * br


def _mod_kernel(c_ref, w_ref, b_ref, o_ref):
    o_ref[0] = jnp.dot(_silu(c_ref[...]), w_ref[0], precision=HIGHEST,
                       preferred_element_type=F32) + b_ref[0]


def ada_mod(cond8, w_ada, b_ada):
    tn = 1536
    return pl.pallas_call(
        _mod_kernel,
        grid=(DEPTH, 6 * D // tn),
        in_specs=[pl.BlockSpec((8, D), lambda i, j: (0, 0)),
                  pl.BlockSpec((1, D, tn), lambda i, j: (i, 0, j)),
                  pl.BlockSpec((1, 1, tn), lambda i, j: (i, 0, j))],
        out_specs=pl.BlockSpec((1, 8, tn), lambda i, j: (i, 0, j)),
        out_shape=jax.ShapeDtypeStruct((DEPTH, 8, 6 * D), F32),
        compiler_params=_cparams("parallel", "parallel"),
        name="ada_mod",
    )(cond8, w_ada, b_ada.reshape(DEPTH, 1, 6 * D))


def _embed_kernel(xp_ref, xs_ref, pos_ref, o_ref):
    r = pl.program_id(0)

    @pl.when(r == 0)
    def _():
        o_ref[...] = xp_ref[...]

    @pl.when(r > 0)
    def _():
        o_ref[...] = xs_ref[...] + pos_ref[...]


def embed_tokens(xp, xs, pos):
    tm = 1024
    nt = GROUP_ROWS // tm
    return pl.pallas_call(
        _embed_kernel,
        grid=(N_GROUPS, nt),
        in_specs=[pl.BlockSpec((tm, D), lambda r, i: (jnp.where(r == 0, i, 0), 0)),
                  pl.BlockSpec((tm, D), lambda r, i: (jnp.where(r == 0, 0, (r - 1) * nt + i), 0)),
                  pl.BlockSpec((tm, D), lambda r, i: (i, 0))],
        out_specs=pl.BlockSpec((tm, D), lambda r, i: (r * nt + i, 0)),
        out_shape=jax.ShapeDtypeStruct((N_TOK, D), F32),
        compiler_params=_cparams("parallel", "parallel"),
        name="embed_tokens",
    )(xp, xs, pos)


TM = 1024
TILES_PER_GROUP = GROUP_ROWS // TM


def _group_of_tile(i):
    return i // TILES_PER_GROUP


def _normmod_kernel(x_ref, g_ref, sc_ref, sh_ref, o_ref):
    o_ref[...] = _normmod(x_ref[...], g_ref[...], sc_ref[0], sh_ref[0])


def normmod(x, g, sc, sh):
    return pl.pallas_call(
        _normmod_kernel,
        grid=(N_TOK // TM,),
        in_specs=[pl.BlockSpec((TM, D), lambda i: (i, 0)),
                  pl.BlockSpec((1, D), lambda i: (0, 0)),
                  pl.BlockSpec((1, 1, D), lambda i: (_group_of_tile(i), 0, 0)),
                  pl.BlockSpec((1, 1, D), lambda i: (_group_of_tile(i), 0, 0))],
        out_specs=pl.BlockSpec((TM, D), lambda i: (i, 0)),
        out_shape=jax.ShapeDtypeStruct((N_TOK, D), F32),
        compiler_params=_cparams("parallel"),
        name="normmod",
    )(x, g.reshape(1, D), sc, sh)


def _nm_matmul_kernel(x_ref, g_ref, sc_ref, sh_ref, w_ref, b_ref, o_ref, h_scr):
    @pl.when(pl.program_id(1) == 0)
    def _():
        h_scr[...] = _normmod(x_ref[...], g_ref[...], sc_ref[0], sh_ref[0]).astype(BF16)

    o_ref[...] = jnp.dot(h_scr[...], w_ref[...].astype(BF16),
                         preferred_element_type=F32) + b_ref[...]


def nm_matmul(x, g, sc, sh, w, b, n_out, tn=1024, name="nm_matmul"):
    return pl.pallas_call(
        _nm_matmul_kernel,
        grid=(N_TOK // TM, n_out // tn),
        in_specs=[pl.BlockSpec((TM, D), lambda i, j: (i, 0)),
                  pl.BlockSpec((1, D), lambda i, j: (0, 0)),
                  pl.BlockSpec((1, 1, D), lambda i, j: (_group_of_tile(i), 0, 0)),
                  pl.BlockSpec((1, 1, D), lambda i, j: (_group_of_tile(i), 0, 0)),
                  pl.BlockSpec((D, tn), lambda i, j: (0, j)),
                  pl.BlockSpec((1, tn), lambda i, j: (0, j))],
        out_specs=pl.BlockSpec((TM, tn), lambda i, j: (i, j)),
        out_shape=jax.ShapeDtypeStruct((N_TOK, n_out), F32),
        scratch_shapes=[pltpu.VMEM((TM, D), BF16)],
        compiler_params=_cparams("parallel", "arbitrary"),
        name=name,
    )(x, g.reshape(1, D), sc, sh, w, b.reshape(1, -1))


S5_G = 64
S5_H = 16
S5_P = 64
S5_L = 16
S5_GB = 8
S5_NB = S5_G // S5_GB
S5_CH = GROUP_ROWS // S5_L
S5_CTX_CH = CTX_LEN // S5_L
S5_NPOW = 8
S5_XW = S5_L * 128
S5_SW = 2 * S5_GB * S5_P


S5_LH = S5_L * S5_H
S5_NPWR = 24


def _s5_prep_kernel(lr_ref, li_ref, ldt_ref, btr_ref, bti_ref, cxr_ref, cxi_ref,
                    sel0_ref, sel1_ref, exp_ref,
                    kk_ref, wst_ref, wout_ref, apr_ref, api_ref):
    dt = jnp.exp(ldt_ref[0])
    lam_r, lam_i = lr_ref[0], li_ref[0]
    ar, ai = lam_r * dt, lam_i * dt
    pw = lax.broadcasted_iota(jnp.int32, (S5_GB, S5_NPWR, 2 * S5_P), 1).astype(F32)
    ep = jnp.exp(pw * ar)
    pwr, pwi = ep * jnp.cos(pw * ai), ep * jnp.sin(pw * ai)
    den = lam_r * lam_r + lam_i * lam_i
    nr, ni = pwr[:, 1:2] - 1.0, pwi[:, 1:2]
    beta_r = (nr * lam_r + ni * lam_i) / den
    beta_i = (ni * lam_r - nr * lam_i) / den
    bbr, bbi = _cmul(beta_r, beta_i, btr_ref[0], bti_ref[0])

    lane = lax.broadcasted_iota(jnp.int32, (S5_H, 2 * S5_P), 1)
    wst_ref[0, 0] = jnp.zeros((S5_XW, S5_SW), BF16)
    half = S5_SW // 2
    for g in range(S5_GB):
        mine = (lane >= S5_P) if g % 2 else (lane < S5_P)
        col = 128 * (g // 2)
        for s in range(S5_L):
            k = S5_L - 1 - s
            wr, wi = _cmul(pwr[g, k:k + 1], pwi[g, k:k + 1], bbr[g], bbi[g])
            rows = slice(128 * s + S5_H * g, 128 * s + S5_H * (g + 1))
            wst_ref[0, 0, rows, col:col + 128] = jnp.where(mine, wr, 0.0).astype(BF16)
            wst_ref[0, 0, rows, half + col:half + col + 128] = jnp.where(mine, wi, 0.0).astype(BF16)

    lane1 = lax.broadcasted_iota(jnp.int32, (1, 2 * S5_P), 1)

    def group_lanes(a):
        return jnp.concatenate([jnp.where(lane1 < S5_P, a[2 * q], a[2 * q + 1])
                                for q in range(S5_GB // 2)], axis=1)

    pr, pi_ = pwr[:, S5_L:S5_L + 1], pwi[:, S5_L:S5_L + 1]
    for k in range(S5_NPOW):
        apr_ref[0, 0, k:k + 1, :] = group_lanes(pr)
        api_ref[0, 0, k:k + 1, :] = group_lanes(pi_)
        pr, pi_ = _cmul(pr, pi_, pr, pi_)

    tn = (((0,), (0,)), ((), ()))
    kks, wre, wim = [], [], []
    for g in range(S5_GB):
        pg_r, pg_i = pwr[g, :, :S5_P], pwi[g, :, :S5_P]
        cr, ci = cxr_ref[0, g], cxi_ref[0, g]

        def c_times_pow(sel):
            er = lax.dot_general(pg_r, sel, tn, precision=HIGHEST, preferred_element_type=F32)
            ei = lax.dot_general(pg_i, sel, tn, precision=HIGHEST, preferred_element_type=F32)
            return _cmul(cr, ci, er, ei)

        k_r, k_i = c_times_pow(sel0_ref[...])
        kks.append(jnp.dot(bbr[g, :, :S5_P], k_r, precision=HIGHEST, preferred_element_type=F32)
                   - jnp.dot(bbi[g, :, :S5_P], k_i, precision=HIGHEST, preferred_element_type=F32))
        o_r, o_i = c_times_pow(sel1_ref[...])
        wre.append(o_r)
        wim.append(-o_i)
    glane = (lax.broadcasted_iota(jnp.int32, (1, S5_XW), 1) % 128) // S5_H

    def spread(parts, rows_per_group):
        a = jnp.concatenate(parts, axis=0).astype(BF16)
        a = jnp.dot(a, exp_ref[...], preferred_element_type=F32)
        grow = lax.broadcasted_iota(jnp.int32, (a.shape[0], 1), 0) // rows_per_group % S5_GB
        return jnp.where(grow == glane, a, 0.0).astype(BF16)

    wout_ref[0, 0] = spread(wre + wim, S5_P)
    kk_ref[0, 0] = spread(kks, S5_H)


def s5_prepare(lam_re, lam_im, log_dt, b_re, b_im, c_re, c_im):
    half = S5_SW // 2
    lr = jnp.tile(lam_re.reshape(2, S5_G, 1, S5_P), (1, 1, 1, 2))
    li = jnp.tile(lam_im.reshape(2, S5_G, 1, S5_P), (1, 1, 1, 2))
    ldt = log_dt.reshape(2, S5_G, 1, 1)
    btr = jnp.tile(jnp.swapaxes(b_re, 2, 3), (1, 1, 1, 2))
    bti = jnp.tile(jnp.swapaxes(b_im, 2, 3), (1, 1, 1, 2))
    cxr = jnp.tile(jnp.swapaxes(c_re, 2, 3), (1, 1, 1, S5_L))
    cxi = jnp.tile(jnp.swapaxes(c_im, 2, 3), (1, 1, 1, S5_L))
    k = jnp.arange(S5_NPWR)[:, None]
    t = (jnp.arange(S5_LH) // S5_H)[None, :]
    sel0 = (k == t).astype(F32)
    sel1 = (k == t + 1).astype(F32)
    src = jnp.arange(S5_LH)[:, None]
    dst = jnp.arange(S5_XW)[None, :]
    expand = ((src // S5_H == dst // 128) & (src % S5_H == dst % S5_H)).astype(BF16)

    def spec(*tail):
        return pl.BlockSpec((1, S5_GB) + tail, lambda d, j: (d, j) + (0,) * len(tail))

    def const(shape):
        return pl.BlockSpec(shape, lambda d, j: (0,) * len(shape))

    def blk(*tail):
        return pl.BlockSpec((1, 1) + tail, lambda d, j: (d, j) + (0,) * len(tail))

    return pl.pallas_call(
        _s5_prep_kernel,
        grid=(2, S5_NB),
        in_specs=[spec(1, 2 * S5_P), spec(1, 2 * S5_P), spec(1, 1),
                  spec(S5_H, 2 * S5_P), spec(S5_H, 2 * S5_P), spec(S5_P, S5_LH), spec(S5_P, S5_LH),
                  const((S5_NPWR, S5_LH)), const((S5_NPWR, S5_LH)), const((S5_LH, S5_XW))],
        out_specs=[blk(128, S5_XW), blk(S5_XW, S5_SW), blk(S5_SW, S5_XW),
                   blk(S5_NPOW, half), blk(S5_NPOW, half)],
        out_shape=[jax.ShapeDtypeStruct((2, S5_NB, 128, S5_XW), BF16),
                   jax.ShapeDtypeStruct((2, S5_NB, S5_XW, S5_SW), BF16),
                   jax.ShapeDtypeStruct((2, S5_NB, S5_SW, S5_XW), BF16),
                   jax.ShapeDtypeStruct((2, S5_NB, S5_NPOW, half), F32),
                   jax.ShapeDtypeStruct((2, S5_NB, S5_NPOW, half), F32)],
        compiler_params=_cparams("parallel", "parallel"),
        name="s5_prepare",
    )(lr, li, ldt, btr, bti, cxr, cxi, sel0, sel1, expand)


def _s5_scan_body(d, r, h_ref, kk_ref, wst_ref, wout_ref, apr_ref, api_ref, h0r_ref, h0i_ref,
                  y_ref, fr_ref, fi_ref, m8, zr_s, zi_s):
    half = S5_SW // 2

    @pl.when(r == 0)
    def _build():
        for s in range(S5_L):
            if s:
                m8[128 * s:128 * (s + 1), 0:128 * s] = jnp.zeros((128, 128 * s), BF16)
            m8[128 * s:128 * (s + 1), 128 * s:] = kk_ref[0, 0, :, :S5_XW - 128 * s]

    def tloc(s):
        return s if d == 0 else S5_L - 1 - s

    slabs = [h_ref[pl.ds(tloc(s), S5_CH, stride=S5_L), :] for s in range(S5_L)]
    x8 = jnp.concatenate(slabs, axis=1).astype(BF16)
    delta = jnp.dot(x8, wst_ref[0, 0], preferred_element_type=F32)
    zr, zi = delta[:, :half], delta[:, half:]

    row = lax.broadcasted_iota(jnp.int32, (S5_CH, 1), 0)
    is_ctx = r == 0
    pos = jnp.where(is_ctx, row & (S5_CTX_CH - 1), row)
    last = jnp.where(is_ctx, S5_CTX_CH - 1, S5_CH - 1)
    a = jnp.maximum(r - 1, 0)
    lat = (r > 0).astype(F32)
    h0r = h0r_ref[0, 0, pl.ds(a, 1), :] * lat
    h0i = h0i_ref[0, 0, pl.ds(a, 1), :] * lat
    first = (pos == 0) if d == 0 else (pos == last)
    ir, ii = _cmul(apr_ref[0, 0, 0:1, :], api_ref[0, 0, 0:1, :], h0r, h0i)
    zr = zr + jnp.where(first, ir, 0.0)
    zi = zi + jnp.where(first, ii, 0.0)
    for k in range(S5_NPOW):
        m = 1 << k
        akr, aki = apr_ref[0, 0, k:k + 1, :], api_ref[0, 0, k:k + 1, :]
        if d == 0:
            sr, si = pltpu.roll(zr, m, 0), pltpu.roll(zi, m, 0)
            valid = pos >= m
        else:
            sr, si = pltpu.roll(zr, S5_CH - m, 0), pltpu.roll(zi, S5_CH - m, 0)
            valid = pos <= last - m
        pr, pi_ = _cmul(akr, aki, sr, si)
        zr = zr + jnp.where(valid, pr, 0.0)
        zi = zi + jnp.where(valid, pi_, 0.0)
    if d == 0:
        sr, si = pltpu.roll(zr, 1, 0), pltpu.roll(zi, 1, 0)
    else:
        sr, si = pltpu.roll(zr, S5_CH - 1, 0), pltpu.roll(zi, S5_CH - 1, 0)
    sr = jnp.where(first, h0r, sr)
    si = jnp.where(first, h0i, si)
    s_in = jnp.concatenate([sr, si], axis=1).astype(BF16)
    y8 = (jnp.dot(x8, m8[...], preferred_element_type=F32)
          + jnp.dot(s_in, wout_ref[0, 0], preferred_element_type=F32))
    for t in range(S5_L):
        y_ref[0, pl.ds(tloc(t), S5_CH, stride=S5_L), :] = y8[:, 128 * t:128 * (t + 1)]

    @pl.when(r == 0)
    def _fin():
        off = S5_CTX_CH - 1 if d == 0 else 0
        for q in range(half // 128):
            zr_s[q] = zr[:, 128 * q:128 * (q + 1)]
            zi_s[q] = zi[:, 128 * q:128 * (q + 1)]
            fr_ref[0, 0, :, 128 * q:128 * (q + 1)] = zr_s[q, pl.ds(off, N_CTX_SEQ, stride=S5_CTX_CH), :]
            fi_ref[0, 0, :, 128 * q:128 * (q + 1)] = zi_s[q, pl.ds(off, N_CTX_SEQ, stride=S5_CTX_CH), :]


def _s5_scan_kernel(*refs):
    d = pl.program_id(1)
    r = pl.program_id(2)

    @pl.when(d == 0)
    def _():
        _s5_scan_body(0, r, *refs)

    @pl.when(d == 1)
    def _():
        _s5_scan_body(1, r, *refs)


def s5_scan(h, kk8, wst8, wout8, apr, api, h0r, h0i):
    half = S5_SW // 2
    return pl.pallas_call(
        _s5_scan_kernel,
        grid=(S5_NB, 2, N_GROUPS),
        in_specs=[pl.BlockSpec((GROUP_ROWS, 128), lambda j, d, r: (r, j)),
                  pl.BlockSpec((1, 1, 128, S5_XW), lambda j, d, r: (d, j, 0, 0)),
                  pl.BlockSpec((1, 1, S5_XW, S5_SW), lambda j, d, r: (d, j, 0, 0)),
                  pl.BlockSpec((1, 1, S5_SW, S5_XW), lambda j, d, r: (d, j, 0, 0)),
                  pl.BlockSpec((1, 1, S5_NPOW, half), lambda j, d, r: (d, j, 0, 0)),
                  pl.BlockSpec((1, 1, S5_NPOW, half), lambda j, d, r: (d, j, 0, 0)),
                  pl.BlockSpec((1, 1, N_LAT_SEQ, half), lambda j, d, r: (d, j, 0, 0)),
                  pl.BlockSpec((1, 1, N_LAT_SEQ, half), lambda j, d, r: (d, j, 0, 0))],
        out_specs=[pl.BlockSpec((1, GROUP_ROWS, 128), lambda j, d, r: (d, r, j)),
                   pl.BlockSpec((1, 1, N_CTX_SEQ, half), lambda j, d, r: (d, j, 0, 0)),
                   pl.BlockSpec((1, 1, N_CTX_SEQ, half), lambda j, d, r: (d, j, 0, 0))],
        out_shape=[jax.ShapeDtypeStruct((2, N_TOK, D), F32),
                   jax.ShapeDtypeStruct((2, S5_NB, N_CTX_SEQ, half), F32),
                   jax.ShapeDtypeStruct((2, S5_NB, N_CTX_SEQ, half), F32)],
        scratch_shapes=[pltpu.VMEM((S5_XW, S5_XW), BF16),
                        pltpu.VMEM((half // 128, S5_CH, 128), F32),
                        pltpu.VMEM((half // 128, S5_CH, 128), F32)],
        compiler_params=_cparams("arbitrary", "arbitrary", "arbitrary"),
        name="s5_scan",
    )(h, kk8, wst8, wout8, apr, api, h0r, h0i)


def _s5_glu_kernel(h_ref, y0_ref, y1_ref, dsk_ref, wa_ref, wb_ref, ba_ref, bb_ref, x_ref, g1_ref,
                   o_ref, yg_scr):
    @pl.when(pl.program_id(1) == 0)
    def _():
        y = dsk_ref[...] * h_ref[...] + y0_ref[0] + y1_ref[0]
        yg_scr[...] = jax.nn.gelu(y).astype(BF16)

    yg = yg_scr[...]
    a = jnp.dot(yg, wa_ref[...].astype(BF16), preferred_element_type=F32) + ba_ref[...]
    b = jnp.dot(yg, wb_ref[...].astype(BF16), preferred_element_type=F32) + bb_ref[...]
    o_ref[...] = x_ref[...] + g1_ref[0] * (a * jax.nn.sigmoid(b))


def s5_glu(h, y, d_skip, w_glu, b_glu, x, g1, tn=512):
    nj = D // tn
    b2 = b_glu.reshape(1, 2 * D)
    return pl.pallas_call(
        _s5_glu_kernel,
        grid=(N_TOK // TM, nj),
        in_specs=[pl.BlockSpec((TM, D), lambda i, j: (i, 0)),
                  pl.BlockSpec((1, TM, D), lambda i, j: (0, i, 0)),
                  pl.BlockSpec((1, TM, D), lambda i, j: (1, i, 0)),
                  pl.BlockSpec((1, D), lambda i, j: (0, 0)),
                  pl.BlockSpec((D, tn), lambda i, j: (0, j)),
                  pl.BlockSpec((D, tn), lambda i, j: (0, nj + j)),
                  pl.BlockSpec((1, tn), lambda i, j: (0, j)),
                  pl.BlockSpec((1, tn), lambda i, j: (0, nj + j)),
                  pl.BlockSpec((TM, tn), lambda i, j: (i, j)),
                  pl.BlockSpec((1, 1, tn), lambda i, j: (_group_of_tile(i), 0, j))],
        out_specs=pl.BlockSpec((TM, tn), lambda i, j: (i, j)),
        out_shape=jax.ShapeDtypeStruct((N_TOK, D), F32),
        scratch_shapes=[pltpu.VMEM((TM, D), BF16)],
        compiler_params=_cparams("parallel", "arbitrary"),
        name="s5_glu",
    )(h, y, y, d_skip.reshape(1, D), w_glu, w_glu, b2, b2, x, g1)


def s5_layer(x, mods, g_norm, st_re, st_im, lam_re, lam_im, log_dt, b_re, b_im, c_re, c_im,
             d_skip, w_glu, b_glu):
    sh1, sc1, g1 = mods
    h = normmod(x, g_norm, sc1, sh1)
    kk8, wst8, wout8, apr, api = s5_prepare(lam_re, lam_im, log_dt, b_re, b_im, c_re, c_im)

    def h0(st):
        return jnp.transpose(st.reshape(N_LAT_SEQ, 2, S5_NB, S5_GB * S5_P), (1, 2, 0, 3))

    y, fr, fi = s5_scan(h, kk8, wst8, wout8, apr, api, h0(st_re), h0(st_im))

    def fin(f):
        return jnp.transpose(f.reshape(2, S5_NB, N_CTX_SEQ, S5_GB, S5_P), (2, 0, 1, 3, 4)
                             ).reshape(N_CTX_SEQ, 2, S5_G, S5_P)

    x = s5_glu(h, y, d_skip, w_glu, b_glu, x, g1)
    return x, fin(fr), fin(fi)


CHUNK = 64
N_CHUNKS = N_TOK // CHUNK
CTX_CHUNKS = N_CTX_SEQ * CTX_LEN // CHUNK
CH_PER_CTX = CTX_LEN // CHUNK
CH_PER_LAT = LAT_LEN // CHUNK


def _chunk_of_step(d, c):
    return jnp.where(d == 0, c, N_CHUNKS - 1 - c)


def _seq_of_chunk(ce):
    return jnp.where(ce < CTX_CHUNKS, ce // CH_PER_CTX, N_CTX_SEQ + (ce - CTX_CHUNKS) // CH_PER_LAT)


def _chunk_flags(d, ce):
    is_ctx = ce < CTX_CHUNKS
    pos = jnp.where(is_ctx, ce % CH_PER_CTX, (ce - CTX_CHUNKS) % CH_PER_LAT)
    n = jnp.where(is_ctx, CH_PER_CTX, CH_PER_LAT)
    t_first, t_last = pos == 0, pos == n - 1
    if d == 0:
        return is_ctx, t_first, t_last
    return is_ctx, t_last, t_first


def _tri(d, shape, row_axis=0, col_axis=1):
    r = lax.broadcasted_iota(jnp.int32, shape, row_axis)
    c = lax.broadcasted_iota(jnp.int32, shape, col_axis)
    return (r >= c) if d == 0 else (r <= c)


HG_H = 8
HG_K = 128
HG_SAFE_SPAN = 60.0


def _hg_decay(d, layer, z_ref, lbl_ref):
    lg = lbl_ref[d]
    e = jnp.exp(lg - jnp.max(lg, axis=0, keepdims=True))
    sm = e / jnp.sum(e, axis=0, keepdims=True)
    lb = jnp.sum(sm[1:layer + 1], axis=0, keepdims=True)
    f = lb + (1.0 - lb) * jax.nn.sigmoid(z_ref[...])
    g = jnp.log(f)
    g_hi = g.astype(BF16)
    g_lo = (g - g_hi.astype(F32)).astype(BF16)
    tri = _tri(d, (CHUNK, CHUNK)).astype(BF16)
    cum = (jnp.dot(tri, g_hi, preferred_element_type=F32)
           + jnp.dot(tri, g_lo, preferred_element_type=F32))
    return 1.0 - f, cum


def _hg_main(d, layer, q_ref, v_ref, z_ref, lbl_ref, o_ref, st_scr, inter_scr):
    kk, cum = _hg_decay(d, layer, z_ref, lbl_ref)
    tri = _tri(d, (CHUNK, CHUNK))
    head, tot = (cum[0:1], cum[CHUNK - 1:CHUNK]) if d == 0 else (cum[CHUNK - 1:CHUNK], cum[0:1])
    mid = cum[CHUNK // 2 - 1:CHUNK // 2]
    q = q_ref[...]
    v = v_ref[...]
    qa = (q * jnp.exp(cum - mid)).astype(BF16)
    ka = (kk * jnp.exp(mid - cum)).astype(BF16)
    qs = (q * jnp.exp(cum)).astype(BF16)
    kd = (kk * jnp.exp(tot - cum)).astype(BF16)
    vb = v.astype(BF16)
    etot = jnp.exp(tot)
    nt = (((1,), (1,)), ((), ()))
    tn = (((0,), (0,)), ((), ()))
    for hd in range(HG_H):
        sl = slice(HG_K * hd, HG_K * (hd + 1))
        a = lax.dot_general(qa[:, sl], ka[:, sl], nt, preferred_element_type=F32)
        a = jnp.where(tri, a, 0.0).astype(BF16)
        st = st_scr[hd]
        inter = lax.dot_general(qs[:, sl], st.astype(BF16), nt, preferred_element_type=F32)
        inter_scr[:, sl] = inter
        o_ref[:, sl] = jnp.dot(a, vb[:, sl], preferred_element_type=F32) + inter
        st_scr[hd] = st * etot[:, sl] + lax.dot_general(vb[:, sl], kd[:, sl], tn,
                                                        preferred_element_type=F32)

    return jnp.max(jnp.maximum(head - mid, mid - tot))


def _hg_exact_intra(d, layer, q_ref, v_ref, z_ref, lbl_ref, o_ref, inter_scr, cum_scr, k_scr):
    kk, cum = _hg_decay(d, layer, z_ref, lbl_ref)
    cum_scr[...] = cum
    k_scr[...] = kk
    q = q_ref[...]
    c_idx = lax.broadcasted_iota(jnp.int32, (D, 128), 0) // HG_K
    h_idx = lax.broadcasted_iota(jnp.int32, (D, 128), 1)
    head_sum = (c_idx == h_idx).astype(F32)
    c_idx_t = lax.broadcasted_iota(jnp.int32, (128, D), 1) // HG_K
    h_idx_t = lax.broadcasted_iota(jnp.int32, (128, D), 0)
    head_bcast = (c_idx_t == h_idx_t).astype(F32)
    row = lax.broadcasted_iota(jnp.int32, (CHUNK, 1), 0)

    def source_row(s, acc):
        seen = (row >= s) if d == 0 else (row <= s)
        w = jnp.exp(jnp.where(seen, cum - cum_scr[pl.ds(s, 1), :], -jnp.inf))
        p = q * k_scr[pl.ds(s, 1), :] * w
        a_s = jnp.dot(p, head_sum, precision=HIGHEST, preferred_element_type=F32)
        a_s = jnp.dot(a_s, head_bcast, precision=HIGHEST, preferred_element_type=F32)
        return acc + a_s * v_ref[pl.ds(s, 1), :]

    intra = lax.fori_loop(0, CHUNK, source_row, jnp.zeros((CHUNK, D), F32))
    o_ref[...] = inter_scr[...] + intra


def _hg_scan_kernel(q0, v0, z0, q1, v1, z1, lbl_ref, s00, s01, o0, o1, fin0, fin1,
                    st0, st1, inter0, inter1, cum_scr, k_scr, *, layer):
    c = pl.program_id(0)
    dirs = ((0, q0, v0, z0, s00, o0, fin0, st0, inter0), (1, q1, v1, z1, s01, o1, fin1, st1, inter1))
    flags = [_chunk_flags(d, _chunk_of_step(d, c)) for d in (0, 1)]

    def set_state(st, value_fn):
        st[...] = value_fn()

    for (d, q, v, z, s0, o, fin, st, inter), (is_ctx, starts, ends) in zip(dirs, flags):
        pl.when(starts & is_ctx)(functools.partial(set_state, st, lambda: jnp.zeros((HG_H, HG_K, HG_K), F32)))
        pl.when(starts & jnp.logical_not(is_ctx))(functools.partial(set_state, st, lambda s0=s0: s0[0, 0]))
    spans = [_hg_main(d, layer, q, v, z, lbl_ref, o, st, inter)
             for (d, q, v, z, s0, o, fin, st, inter) in dirs]
    for (d, q, v, z, s0, o, fin, st, inter), span in zip(dirs, spans):
        pl.when(span > HG_SAFE_SPAN)(functools.partial(
            _hg_exact_intra, d, layer, q, v, z, lbl_ref, o, inter, cum_scr, k_scr))

    def write_final(fin, st):
        for hd in range(HG_H):
            fin[0, hd] = st[hd].T

    for (d, q, v, z, s0, o, fin, st, inter), (is_ctx, starts, ends) in zip(dirs, flags):
        pl.when(ends)(functools.partial(write_final, fin, st))


def hg_scan(proj, lb_logits, s0t, layer):
    def tok(d, col):
        return lambda c: (_chunk_of_step(d, c), col)

    def lat_idx(d):
        return lambda c: (d, jnp.maximum(_seq_of_chunk(_chunk_of_step(d, c)) - N_CTX_SEQ, 0), 0, 0, 0)

    def fin_idx(d):
        return lambda c: (jnp.minimum(_seq_of_chunk(_chunk_of_step(d, c)), N_CTX_SEQ), 0, 0, 0)

    state = (HG_H, HG_K, HG_K)
    return pl.pallas_call(
        functools.partial(_hg_scan_kernel, layer=layer),
        grid=(N_CHUNKS,),
        in_specs=[pl.BlockSpec((CHUNK, D), tok(0, 0)), pl.BlockSpec((CHUNK, D), tok(0, 1)),
                  pl.BlockSpec((CHUNK, D), tok(0, 3)),
                  pl.BlockSpec((CHUNK, D), tok(1, 0)), pl.BlockSpec((CHUNK, D), tok(1, 1)),
                  pl.BlockSpec((CHUNK, D), tok(1, 4)),
                  pl.BlockSpec((2, DEPTH, D), lambda c: (0, 0, 0)),
                  pl.BlockSpec((1, 1) + state, lat_idx(0)), pl.BlockSpec((1, 1) + state, lat_idx(1))],
        out_specs=[pl.BlockSpec((CHUNK, D), tok(0, 0)), pl.BlockSpec((CHUNK, D), tok(1, 0)),
                   pl.BlockSpec((1,) + state, fin_idx(0)), pl.BlockSpec((1,) + state, fin_idx(1))],
        out_shape=[jax.ShapeDtypeStruct((N_TOK, D), F32), jax.ShapeDtypeStruct((N_TOK, D), F32),
                   jax.ShapeDtypeStruct((N_CTX_SEQ + 1,) + state, F32),
                   jax.ShapeDtypeStruct((N_CTX_SEQ + 1,) + state, F32)],
        scratch_shapes=[pltpu.VMEM(state, F32), pltpu.VMEM(state, F32),
                        pltpu.VMEM((CHUNK, D), F32), pltpu.VMEM((CHUNK, D), F32),
                        pltpu.VMEM((CHUNK, D), F32), pltpu.VMEM((CHUNK, D), F32)],
        compiler_params=_cparams("arbitrary"),
        name="hg_scan",
    )(proj, proj, proj, proj, proj, proj, lb_logits, s0t, s0t)


def _hg_out_kernel(o0_ref, o1_ref, gate_ref, gn_ref, w_ref, x_ref, g1_ref, out_ref, on_scr):
    @pl.when(pl.program_id(1) == 0)
    def _():
        for hd in range(HG_H):
            sl = slice(HG_K * hd, HG_K * (hd + 1))
            o = o0_ref[:, sl] + o1_ref[:, sl]
            o = o * lax.rsqrt(jnp.mean(o * o, axis=-1, keepdims=True) + NORM_EPS) * gn_ref[...]
            on_scr[:, sl] = (o * _silu(gate_ref[:, sl])).astype(BF16)

    out_ref[...] = x_ref[...] + g1_ref[0] * jnp.dot(on_scr[...], w_ref[...].astype(BF16),
                                                    preferred_element_type=F32)


def hg_out(o_fwd, o_bwd, proj, g_norm, w_o, x, g1, tn=512):
    return pl.pallas_call(
        _hg_out_kernel,
        grid=(N_TOK // TM, D // tn),
        in_specs=[pl.BlockSpec((TM, D), lambda i, j: (i, 0)),
                  pl.BlockSpec((TM, D), lambda i, j: (i, 0)),
                  pl.BlockSpec((TM, D), lambda i, j: (i, 2)),
                  pl.BlockSpec((1, HG_K), lambda i, j: (0, 0)),
                  pl.BlockSpec((D, tn), lambda i, j: (0, j)),
                  pl.BlockSpec((TM, tn), lambda i, j: (i, j)),
                  pl.BlockSpec((1, 1, tn), lambda i, j: (_group_of_tile(i), 0, j))],
        out_specs=pl.BlockSpec((TM, tn), lambda i, j: (i, j)),
        out_shape=jax.ShapeDtypeStruct((N_TOK, D), F32),
        scratch_shapes=[pltpu.VMEM((TM, D), BF16)],
        compiler_params=_cparams("parallel", "arbitrary"),
        name="hg_out",
    )(o_fwd, o_bwd, proj, g_norm.reshape(1, HG_K), w_o, x, g1)


def hgrn_layer(x, mods, g_mix, state, layer, lb_logits, w_qig, w_f, b_f, g_norm, w_o):
    sh1, sc1, g1 = mods
    w5 = jnp.concatenate([w_qig, w_f[0], w_f[1]], axis=1)
    b5 = jnp.concatenate([jnp.zeros((3 * D,), F32), b_f[0], b_f[1]])
    proj = nm_matmul(x, g_mix, sc1, sh1, w5, b5, 5 * D, name="hg_proj")
    s0t = jnp.transpose(state, (1, 0, 2, 4, 3))
    o_fwd, o_bwd, fin_fwd, fin_bwd = hg_scan(proj, lb_logits, s0t, layer)
    x = hg_out(o_fwd, o_bwd, proj, g_norm, w_o, x, g1)
    return x, jnp.stack([fin_fwd[:N_CTX_SEQ], fin_bwd[:N_CTX_SEQ]], axis=1)


SSD_INNER = 2 * D
SSD_HEADS = 32
SSD_P = 64
SSD_NG = 4
SSD_N = 128
SSD_XBC = SSD_INNER + 2 * SSD_NG * SSD_N
SSD_ZX = SSD_INNER + SSD_XBC
SSD_CONV = 5
CONV_TM = 256
CONV_HALO = 8


def _ssd_conv_kernel(cur_ref, prev_ref, next_ref, w_ref, b_ref, o_ref, ext):
    i = pl.program_id(0)
    n_ctx_tiles = N_CTX_SEQ * CTX_LEN // CONV_TM
    per_lat = LAT_LEN // CONV_TM
    is_ctx = i < n_ctx_tiles
    k = (i - n_ctx_tiles) % per_lat
    seq_start = is_ctx | (k == 0)
    seq_end = is_ctx | (k == per_lat - 1)
    ext[0:CONV_HALO] = jnp.where(seq_start, 0.0, prev_ref[...])
    ext[CONV_HALO:CONV_HALO + CONV_TM] = cur_ref[...]
    ext[CONV_HALO + CONV_TM:] = jnp.where(seq_end, 0.0, next_ref[...])
    acc = jnp.broadcast_to(b_ref[...], (CONV_TM, D))
    for t in range(SSD_CONV):
        acc = acc + w_ref[t:t + 1, :] * ext[pl.ds(CONV_HALO - SSD_CONV // 2 + t, CONV_TM), :]
    o_ref[...] = _silu(acc)


def ssd_conv(zx, conv_w, conv_b):
    nrb = N_TOK // CONV_HALO
    rpt = CONV_TM // CONV_HALO
    c0 = SSD_INNER // D
    return pl.pallas_call(
        _ssd_conv_kernel,
        grid=(N_TOK // CONV_TM, SSD_XBC // D),
        in_specs=[pl.BlockSpec((CONV_TM, D), lambda i, j: (i, c0 + j)),
                  pl.BlockSpec((CONV_HALO, D), lambda i, j: (jnp.maximum(i * rpt - 1, 0), c0 + j)),
                  pl.BlockSpec((CONV_HALO, D), lambda i, j: (jnp.minimum((i + 1) * rpt, nrb - 1), c0 + j)),
                  pl.BlockSpec((SSD_CONV, D), lambda i, j: (0, j)),
                  pl.BlockSpec((1, D), lambda i, j: (0, j))],
        out_specs=pl.BlockSpec((CONV_TM, D), lambda i, j: (i, j)),
        out_shape=jax.ShapeDtypeStruct((N_TOK, SSD_XBC), F32),
        scratch_shapes=[pltpu.VMEM((CONV_TM + 2 * CONV_HALO, D), F32)],
        compiler_params=_cparams("parallel", "parallel"),
        name="ssd_conv",
    )(zx, zx, zx, conv_w, conv_b.reshape(1, SSD_XBC))


def _ssd_main(d, xlo_ref, xhi_ref, bc_ref, dtr_ref, dtb_ref, alog_ref, y_ref, ht_scr):
    xr = dtr_ref[...] + dtb_ref[...]
    dt = jnp.maximum(xr, 0.0) + jnp.log(1.0 + jnp.exp(-jnp.abs(xr)))
    dta = dt * (-jnp.exp(alog_ref[...]))
    tri = _tri(d, (CHUNK, CHUNK))
    cum = jnp.dot(tri.astype(F32), dta, precision=HIGHEST, preferred_element_type=F32)
    r = lax.broadcasted_iota(jnp.int32, (CHUNK, 2 * CHUNK), 0)
    cc = lax.broadcasted_iota(jnp.int32, (CHUNK, 2 * CHUNK), 1)
    lo_half = cc < CHUNK
    ccm = jnp.where(lo_half, cc, cc - CHUNK)
    trit = (ccm >= r) if d == 0 else (ccm <= r)
    tn = (((0,), (0,)), ((), ()))
    nt = (((1,), (1,)), ((), ()))
    cumt_lo = lax.dot_general(dta, (trit & lo_half).astype(F32), tn, precision=HIGHEST,
                              preferred_element_type=F32)
    cumt_hi = lax.dot_general(dta, (trit & jnp.logical_not(lo_half)).astype(F32), tn,
                              precision=HIGHEST, preferred_element_type=F32)
    lane = lax.broadcasted_iota(jnp.int32, (CHUNK, 2 * SSD_P), 1)
    first_head = lane < SSD_P
    tri2 = (r >= ccm) if d == 0 else (r <= ccm)
    bc = bc_ref[...]
    for gq in range(SSD_NG):
        bg = bc[:, SSD_N * gq:SSD_N * (gq + 1)].astype(BF16)
        cg = bc[:, SSD_NG * SSD_N + SSD_N * gq:SSD_NG * SSD_N + SSD_N * (gq + 1)].astype(BF16)
        cb2 = lax.dot_general(cg, jnp.concatenate([bg, bg], axis=0), nt,
                              preferred_element_type=F32)
        for pp in range(4 * gq, 4 * gq + 4):
            h1 = SSD_HEADS * d + 2 * pp
            colp = jnp.where(first_head, cum[:, h1:h1 + 1], cum[:, h1 + 1:h1 + 2])
            rowp = cumt_lo[h1:h1 + 1, :] + cumt_hi[h1 + 1:h1 + 2, :]
            lmat = jnp.exp(jnp.where(tri2, colp - rowp, -jnp.inf))
            dtp = jnp.where(first_head, dt[:, h1:h1 + 1], dt[:, h1 + 1:h1 + 2])
            xref = xlo_ref if pp < 8 else xhi_ref
            c0 = 128 * (pp % 8)
            xdt = xref[:, c0:c0 + 128] * dtp
            rhs = jnp.concatenate([jnp.where(first_head, xdt, 0.0),
                                   jnp.where(first_head, 0.0, xdt)], axis=0).astype(BF16)
            y = jnp.dot((cb2 * lmat).astype(BF16), rhs, preferred_element_type=F32)
            ht = ht_scr[:, 128 * pp:128 * (pp + 1)]
            y = y + jnp.dot(cg, ht.astype(BF16), preferred_element_type=F32) * jnp.exp(colp)
            y_ref[:, 128 * pp:128 * (pp + 1)] = y
            totp = colp[CHUNK - 1:CHUNK] if d == 0 else colp[0:1]
            xw = (xdt * jnp.exp(totp - colp)).astype(BF16)
            ht_scr[:, 128 * pp:128 * (pp + 1)] = (
                ht * jnp.exp(totp) + lax.dot_general(bg, xw, tn, preferred_element_type=F32))


def _ssd_scan_kernel(xlo0, xhi0, bc0, dtr0, xlo1, xhi1, bc1, dtr1, dtb_ref, alog_ref, h00, h01,
                     y0, y1, fin0, fin1, ht0, ht1):
    c = pl.program_id(0)
    dirs = ((0, xlo0, xhi0, bc0, dtr0, h00, y0, fin0, ht0), (1, xlo1, xhi1, bc1, dtr1, h01, y1, fin1, ht1))
    flags = [_chunk_flags(d, _chunk_of_step(d, c)) for d in (0, 1)]

    def set_state(ht, value_fn):
        ht[...] = value_fn()

    for (d, xlo, xhi, bc, dtr, h0, y, fin, ht), (is_ctx, starts, ends) in zip(dirs, flags):
        pl.when(starts & is_ctx)(functools.partial(
            set_state, ht, lambda: jnp.zeros((SSD_N, SSD_HEADS * SSD_P), F32)))
        pl.when(starts & jnp.logical_not(is_ctx))(functools.partial(set_state, ht, lambda h0=h0: h0[0, 0]))
    for (d, xlo, xhi, bc, dtr, h0, y, fin, ht) in dirs:
        _ssd_main(d, xlo, xhi, bc, dtr, dtb_ref, alog_ref, y, ht)
    def write_final(fin, ht):
        fin[0] = ht[...]

    for (d, xlo, xhi, bc, dtr, h0, y, fin, ht), (is_ctx, starts, ends) in zip(dirs, flags):
        pl.when(ends)(functools.partial(write_final, fin, ht))


def ssd_scan(xbc, dtr, dt_bias, a_log, h0t):
    nh2 = 2 * SSD_HEADS
    hp = SSD_HEADS * SSD_P

    def tok(d, col):
        return lambda c: (_chunk_of_step(d, c), col)

    def lat_idx(d):
        return lambda c: (d, jnp.maximum(_seq_of_chunk(_chunk_of_step(d, c)) - N_CTX_SEQ, 0), 0, 0)

    def fin_idx(d):
        return lambda c: (jnp.minimum(_seq_of_chunk(_chunk_of_step(d, c)), N_CTX_SEQ), 0, 0)

    def tok_specs(d):
        return [pl.BlockSpec((CHUNK, D), tok(d, 0)), pl.BlockSpec((CHUNK, D), tok(d, 1)),
                pl.BlockSpec((CHUNK, D), tok(d, 2)), pl.BlockSpec((CHUNK, nh2), tok(d, 0))]

    return pl.pallas_call(
        _ssd_scan_kernel,
        grid=(N_CHUNKS,),
        in_specs=tok_specs(0) + tok_specs(1) + [
            pl.BlockSpec((1, nh2), lambda c: (0, 0)), pl.BlockSpec((1, nh2), lambda c: (0, 0)),
            pl.BlockSpec((1, 1, SSD_N, hp), lat_idx(0)), pl.BlockSpec((1, 1, SSD_N, hp), lat_idx(1))],
        out_specs=[pl.BlockSpec((CHUNK, hp), tok(0, 0)), pl.BlockSpec((CHUNK, hp), tok(1, 0)),
                   pl.BlockSpec((1, SSD_N, hp), fin_idx(0)), pl.BlockSpec((1, SSD_N, hp), fin_idx(1))],
        out_shape=[jax.ShapeDtypeStruct((N_TOK, hp), F32), jax.ShapeDtypeStruct((N_TOK, hp), F32),
                   jax.ShapeDtypeStruct((N_CTX_SEQ + 1, SSD_N, hp), F32),
                   jax.ShapeDtypeStruct((N_CTX_SEQ + 1, SSD_N, hp), F32)],
        scratch_shapes=[pltpu.VMEM((SSD_N, hp), F32), pltpu.VMEM((SSD_N, hp), F32)],
        compiler_params=_cparams("arbitrary"),
        name="ssd_scan",
    )(xbc, xbc, xbc, dtr, xbc, xbc, xbc, dtr, dt_bias.reshape(1, nh2), a_log.reshape(1, nh2), h0t, h0t)


SSD_OUT_TM = 512


def _ssd_out_kernel(xlo_ref, xhi_ref, zlo_ref, zhi_ref, y0_ref, y1_ref, dsk_ref, gn_ref, w_ref,
                    x_ref, g1_ref, out_ref, yn_scr):
    @pl.when(pl.program_id(1) == 0)
    def _():
        halves = []
        ss = jnp.zeros((SSD_OUT_TM, 1), F32)
        for k, (xr, zr) in enumerate(((xlo_ref, zlo_ref), (xhi_ref, zhi_ref))):
            sl = slice(D * k, D * (k + 1))
            y = dsk_ref[:, sl] * xr[...] + y0_ref[:, sl] + y1_ref[:, sl]
            y = y * _silu(zr[...])
            ss = ss + jnp.sum(y * y, axis=-1, keepdims=True)
            halves.append(y)
        scale = lax.rsqrt(ss / SSD_INNER + NORM_EPS)
        for k, y in enumerate(halves):
            sl = slice(D * k, D * (k + 1))
            yn_scr[:, sl] = (y * scale * gn_ref[:, sl]).astype(BF16)

    out_ref[...] = x_ref[...] + g1_ref[0] * jnp.dot(yn_scr[...], w_ref[...].astype(BF16),
                                                    preferred_element_type=F32)


def ssd_out(xbc, zx, y_fwd, y_bwd, d_skip_cols, g_norm, w_out, x, g1, tn=512):
    tm = SSD_OUT_TM
    tpg = GROUP_ROWS // tm
    return pl.pallas_call(
        _ssd_out_kernel,
        grid=(N_TOK // tm, D // tn),
        in_specs=[pl.BlockSpec((tm, D), lambda i, j: (i, 0)),
                  pl.BlockSpec((tm, D), lambda i, j: (i, 1)),
                  pl.BlockSpec((tm, D), lambda i, j: (i, 0)),
                  pl.BlockSpec((tm, D), lambda i, j: (i, 1)),
                  pl.BlockSpec((tm, SSD_INNER), lambda i, j: (i, 0)),
                  pl.BlockSpec((tm, SSD_INNER), lambda i, j: (i, 0)),
                  pl.BlockSpec((1, SSD_INNER), lambda i, j: (0, 0)),
                  pl.BlockSpec((1, SSD_INNER), lambda i, j: (0, 0)),
                  pl.BlockSpec((SSD_INNER, tn), lambda i, j: (0, j)),
                  pl.BlockSpec((tm, tn), lambda i, j: (i, j)),
                  pl.BlockSpec((1, 1, tn), lambda i, j: (i // tpg, 0, j))],
        out_specs=pl.BlockSpec((tm, tn), lambda i, j: (i, j)),
        out_shape=jax.ShapeDtypeStruct((N_TOK, D), F32),
        scratch_shapes=[pltpu.VMEM((tm, SSD_INNER), BF16)],
        compiler_params=_cparams("parallel", "arbitrary"),
        name="ssd_out",
    )(xbc, xbc, zx, zx, y_fwd, y_bwd, d_skip_cols, g_norm.reshape(1, SSD_INNER), w_out, x, g1)


def ssd_layer(x, mods, g_mix, state, w_in, conv_w, conv_b, dt_bias, a_log, d_skip, g_norm, w_out):
    sh1, sc1, g1 = mods
    zx = nm_matmul(x, g_mix, sc1, sh1, w_in, jnp.zeros((SSD_ZX,), F32), SSD_ZX, name="ssd_proj")
    nh2 = 2 * SSD_HEADS
    dtr = nm_matmul(x, g_mix, sc1, sh1, w_in[:, SSD_ZX:], jnp.zeros((nh2,), F32), nh2, tn=nh2,
                    name="ssd_proj_dt")
    xbc = ssd_conv(zx, conv_w, conv_b)
    h0t = jnp.transpose(state, (1, 0, 4, 2, 3)).reshape(2, N_LAT_SEQ, SSD_N, SSD_HEADS * SSD_P)
    y_fwd, y_bwd, fin_fwd, fin_bwd = ssd_scan(xbc, dtr, dt_bias, a_log, h0t)
    dcols = jnp.repeat(d_skip, SSD_P).reshape(1, SSD_INNER)
    x = ssd_out(xbc, zx, y_fwd, y_bwd, dcols, g_norm, w_out, x, g1)
    fin = jnp.stack([fin_fwd[:N_CTX_SEQ], fin_bwd[:N_CTX_SEQ]], axis=1)
    fin = jnp.transpose(fin.reshape(N_CTX_SEQ, 2, SSD_N, SSD_HEADS, SSD_P), (0, 1, 3, 4, 2))
    return x, fin


N_EXP = 16
FF = 2 * D
CAP_CTX = 2 * CTX_LEN // N_EXP
CAP_LAT = 2 * LAT_LEN // N_EXP
SLOTS_PER_GROUP = 512
SLOTS = N_GROUPS * SLOTS_PER_GROUP


def _router_kernel(x_ref, g_ref, sc_ref, sh_ref, wt_ref, h_ref, aff_ref):
    h = _normmod(x_ref[...], g_ref[...], sc_ref[0], sh_ref[0])
    h_ref[...] = h.astype(BF16)
    logits = lax.dot_general(wt_ref[...], h, (((1,), (1,)), ((), ())), precision=HIGHEST,
                             preferred_element_type=F32)
    e = jnp.exp(logits - jnp.max(logits, axis=0, keepdims=True))
    aff_ref[...] = e / jnp.sum(e, axis=0, keepdims=True)


def moe_route(x, g, sc, sh, w_router_t):
    return pl.pallas_call(
        _router_kernel,
        grid=(N_TOK // TM,),
        in_specs=[pl.BlockSpec((TM, D), lambda i: (i, 0)),
                  pl.BlockSpec((1, D), lambda i: (0, 0)),
                  pl.BlockSpec((1, 1, D), lambda i: (_group_of_tile(i), 0, 0)),
                  pl.BlockSpec((1, 1, D), lambda i: (_group_of_tile(i), 0, 0)),
                  pl.BlockSpec((N_EXP, D), lambda i: (0, 0))],
        out_specs=[pl.BlockSpec((TM, D), lambda i: (i, 0)),
                   pl.BlockSpec((N_EXP, TM), lambda i: (0, i))],
        out_shape=[jax.ShapeDtypeStruct((N_TOK, D), BF16),
                   jax.ShapeDtypeStruct((N_EXP, N_TOK), F32)],
        compiler_params=_cparams("parallel"),
        name="moe_router",
    )(x, g.reshape(1, D), sc, sh, w_router_t)


def _lane_prefix_excl(m):
    s, t = m.shape
    r = lax.broadcasted_iota(jnp.int32, (128, 128), 0)
    c = lax.broadcasted_iota(jnp.int32, (128, 128), 1)
    upper = (r <= c).astype(BF16)
    run = jnp.zeros((s, 1), F32)
    out = []
    for k in range(t // 128):
        blk = m[:, 128 * k:128 * (k + 1)]
        inc = jnp.dot(blk.astype(BF16), upper, preferred_element_type=F32) + run
        out.append(inc - blk)
        run = inc[:, 127:128]
    return jnp.concatenate(out, axis=1)


def _select_kernel(a_ref, off_ref, slot_ref, base_ref, *, cap):
    bits = pltpu.bitcast(a_ref[...], jnp.int32)
    s = bits.shape[0]
    capf = float(cap)

    def body(_, lohi):
        lo, hi = lohi
        mid = lo + ((hi - lo + 1) >> 1)
        cnt = jnp.sum((bits >= mid).astype(F32), axis=1, keepdims=True)
        ok = cnt >= capf
        return jnp.where(ok, mid, lo), jnp.where(ok, hi, mid - 1)

    lo0 = jnp.zeros((s, 1), jnp.int32)
    hi0 = jnp.full((s, 1), 0x7F800000, jnp.int32)
    thr, _ = lax.fori_loop(0, 31, body, (lo0, hi0))
    gt = (bits > thr).astype(F32)
    eq = (bits == thr).astype(F32)
    need = capf - jnp.sum(gt, axis=1, keepdims=True)
    sel = gt + eq * (_lane_prefix_excl(eq) < need).astype(F32)
    slot = _lane_prefix_excl(sel) + off_ref[...]
    slot_ref[...] = jnp.where(sel > 0.0, slot, -1.0).astype(jnp.int32)
    t = bits.shape[1]
    tok = lax.broadcasted_iota(jnp.int32, (t, 128), 0)
    tile = lax.broadcasted_iota(jnp.int32, (t, 128), 1)
    ahead = (tok < tile * TOK_TILE).astype(BF16)
    base_ref[...] = jnp.dot(sel.astype(BF16), ahead, preferred_element_type=F32).astype(jnp.int32)


def moe_select(aff, off, cap):
    s, t = aff.shape
    return pl.pallas_call(
        functools.partial(_select_kernel, cap=cap),
        grid=(1,),
        in_specs=[pl.BlockSpec((s, t), lambda i: (0, 0)),
                  pl.BlockSpec((s, 1), lambda i: (0, 0))],
        out_specs=[pl.BlockSpec((s, t), lambda i: (0, 0)),
                   pl.BlockSpec((s, 128), lambda i: (0, 0))],
        out_shape=[jax.ShapeDtypeStruct((s, t), jnp.int32),
                   jax.ShapeDtypeStruct((s, 128), jnp.int32)],
        compiler_params=_cparams("arbitrary"),
        name="moe_select",
    )(aff, off)


TOK_TILE = 256
TILES = GROUP_ROWS // TOK_TILE
CNT_STRIDE = TILES + 1
SLOT_BLK = 128
SLOT_ALIGN = 16
GATHER_SPAN = 8


def _gather_kernel(cnt_ref, slot_ref, aff_ref, h_ref, xs_ref, gs_ref):
    base = (pl.program_id(0) * N_EXP + pl.program_id(1)) * CNT_STRIDE

    def gather(k, t_first, n_tiles):
        s = lax.broadcasted_iota(jnp.int32, (SLOT_BLK, TOK_TILE), 0) + SLOT_BLK * k
        gate = jnp.zeros((SLOT_BLK, 1), F32)
        ohs = []
        for i in range(n_tiles):
            oh = s == slot_ref[0, 0, pl.ds(t_first + i, 1), :]
            ohs.append(oh.astype(BF16))
            gate = gate + jnp.sum(jnp.where(oh, aff_ref[0, pl.ds(t_first + i, 1), :], 0.0),
                                  axis=1, keepdims=True)
        first = t_first * TOK_TILE
        rows = pl.ds(first if isinstance(first, int) else pl.multiple_of(first, TOK_TILE),
                     n_tiles * TOK_TILE)
        out = slice(SLOT_BLK * k, SLOT_BLK * (k + 1))
        xs_ref[0, out, :] = jnp.dot(jnp.concatenate(ohs, axis=1), h_ref[rows, :],
                                    preferred_element_type=F32).astype(BF16)
        gs_ref[0, out, :] = gate

    for k in range(SLOTS_PER_GROUP // SLOT_BLK):
        before = 0
        for j in range(1, TILES + 1):
            before = before + (cnt_ref[base + j] <= SLOT_BLK * k).astype(jnp.int32)
        t0 = jnp.minimum(before, TILES - GATHER_SPAN)
        covered = cnt_ref[base + t0 + GATHER_SPAN] >= SLOT_BLK * (k + 1)

        @pl.when(covered)
        def _():
            gather(k, t0, GATHER_SPAN)

        @pl.when(jnp.logical_not(covered))
        def _():
            gather(k, 0, TILES)


def moe_gather(cnt, slot, aff3, h2):
    return pl.pallas_call(
        _gather_kernel,
        grid_spec=pltpu.PrefetchScalarGridSpec(
            num_scalar_prefetch=1,
            grid=(N_GROUPS, N_EXP),
            in_specs=[pl.BlockSpec((1, 1, TILES, TOK_TILE), lambda r, e, c: (r, e, 0, 0)),
                      pl.BlockSpec((1, TILES, TOK_TILE), lambda r, e, c: (e, r, 0)),
                      pl.BlockSpec((GROUP_ROWS, D), lambda r, e, c: (r, 0))],
            out_specs=[pl.BlockSpec((1, SLOTS_PER_GROUP, D), lambda r, e, c: (e, r, 0)),
                       pl.BlockSpec((1, SLOTS_PER_GROUP, 1), lambda r, e, c: (e, r, 0))]),
        out_shape=[jax.ShapeDtypeStruct((N_EXP, SLOTS, D), BF16),
                   jax.ShapeDtypeStruct((N_EXP, SLOTS, 1), F32)],
        compiler_params=_cparams("parallel", "parallel"),
        name="moe_gather",
    )(cnt, slot, aff3, h2)


FF_TILE = 1024
FFN_ROWS = 512


def _ffn_kernel(xs_ref, gs_ref, wg_ref, wu_ref, wd_ref, ys_ref, acc):
    f = pl.program_id(1)

    @pl.when(f == 0)
    def _():
        acc[...] = jnp.zeros_like(acc)

    wg = wg_ref[0, 0].astype(BF16)
    wu = wu_ref[0, 0].astype(BF16)
    wd = wd_ref[0, 0].astype(BF16)
    for rb in range(SLOTS // FFN_ROWS):
        rows = slice(FFN_ROWS * rb, FFN_ROWS * (rb + 1))
        x = xs_ref[0, rows, :]
        g = jnp.dot(x, wg, preferred_element_type=F32)
        u = jnp.dot(x, wu, preferred_element_type=F32)
        hid = (_silu(g) * u).astype(BF16)
        acc[rows, :] += jnp.dot(hid, wd, preferred_element_type=F32)

    @pl.when(f == FF // FF_TILE - 1)
    def _():
        ys_ref[0] = (acc[...] * gs_ref[0]).astype(BF16)


def moe_ffn(xs, gs, w_gate, w_up, w_down, layer):
    return pl.pallas_call(
        _ffn_kernel,
        grid=(N_EXP, FF // FF_TILE),
        in_specs=[pl.BlockSpec((1, SLOTS, D), lambda e, f: (e, 0, 0)),
                  pl.BlockSpec((1, SLOTS, 1), lambda e, f: (e, 0, 0)),
                  pl.BlockSpec((1, 1, D, FF_TILE), lambda e, f: (layer, e, 0, f)),
                  pl.BlockSpec((1, 1, D, FF_TILE), lambda e, f: (layer, e, 0, f)),
                  pl.BlockSpec((1, 1, FF_TILE, D), lambda e, f: (layer, e, f, 0))],
        out_specs=pl.BlockSpec((1, SLOTS, D), lambda e, f: (e, 0, 0)),
        out_shape=jax.ShapeDtypeStruct((N_EXP, SLOTS, D), BF16),
        scratch_shapes=[pltpu.VMEM((SLOTS, D), F32)],
        compiler_params=_cparams("parallel", "arbitrary"),
        name="moe_ffn",
    )(xs, gs, w_gate, w_up, w_down)


SCAT_TN = 512


def _scatter_kernel(cnt_ref, slot_ref, ys_ref, x_ref, g2_ref, o_ref):
    r, j = pl.program_id(0), pl.program_id(2)
    starts, covered = [], True
    for e in range(N_EXP):
        base = (r * N_EXP + e) * CNT_STRIDE + j
        s = jnp.minimum(cnt_ref[base] // SLOT_ALIGN * SLOT_ALIGN, SLOTS_PER_GROUP - SLOT_BLK)
        starts.append(pl.multiple_of(s, SLOT_ALIGN))
        covered = covered & (cnt_ref[base + 1] <= s + SLOT_BLK)

    def scatter(windows, n):
        lane = lax.broadcasted_iota(jnp.int32, (TOK_TILE, n), 1)
        acc = jnp.zeros((TOK_TILE, SCAT_TN), F32)
        for e in range(N_EXP):
            oh = (slot_ref[0, :, e:e + 1] == lane + windows[e]).astype(BF16)
            acc = acc + jnp.dot(oh, ys_ref[e, pl.ds(windows[e], n), :], preferred_element_type=F32)
        o_ref[...] = x_ref[...] + g2_ref[0] * acc

    @pl.when(covered)
    def _():
        scatter(starts, SLOT_BLK)

    @pl.when(jnp.logical_not(covered))
    def _():
        scatter([0] * N_EXP, SLOTS_PER_GROUP)


def moe_scatter(cnt, slot_t, ys, x, g2):
    return pl.pallas_call(
        _scatter_kernel,
        grid_spec=pltpu.PrefetchScalarGridSpec(
            num_scalar_prefetch=1,
            grid=(N_GROUPS, D // SCAT_TN, TILES),
            in_specs=[pl.BlockSpec((1, TOK_TILE, 128), lambda r, c, j, n: (r, j, 0)),
                      pl.BlockSpec((N_EXP, SLOTS_PER_GROUP, SCAT_TN), lambda r, c, j, n: (0, r, c)),
                      pl.BlockSpec((TOK_TILE, SCAT_TN), lambda r, c, j, n: (r * TILES + j, c)),
                      pl.BlockSpec((1, 1, SCAT_TN), lambda r, c, j, n: (r, 0, c))],
            out_specs=pl.BlockSpec((TOK_TILE, SCAT_TN), lambda r, c, j, n: (r * TILES + j, c))),
        out_shape=jax.ShapeDtypeStruct((N_TOK, D), F32),
        compiler_params=_cparams("parallel", "parallel", "arbitrary"),
        name="moe_scatter",
    )(cnt, slot_t, ys, x, g2)


def moe_layer(x, mods, g_ffn, layer, w_router, w_gate, w_up, w_down):
    sh2, sc2, g2 = mods
    h2, aff = moe_route(x, g_ffn, sc2, sh2, w_router.T)
    aff_ctx = aff[:, :GROUP_ROWS].reshape(N_EXP * N_CTX_SEQ, CTX_LEN)
    off_ctx = jnp.tile(jnp.arange(N_CTX_SEQ, dtype=F32) * CAP_CTX, N_EXP).reshape(-1, 1)
    slot_ctx, _ = moe_select(aff_ctx, off_ctx, CAP_CTX)
    aff_lat = jnp.transpose(aff[:, GROUP_ROWS:].reshape(N_EXP, N_LAT_SEQ, LAT_LEN), (1, 0, 2)
                            ).reshape(N_LAT_SEQ * N_EXP, LAT_LEN)
    slot_lat, base_lat = moe_select(aff_lat, jnp.zeros((N_LAT_SEQ * N_EXP, 1), F32), CAP_LAT)
    slot = jnp.concatenate([slot_ctx.reshape(1, N_EXP, GROUP_ROWS),
                            slot_lat.reshape(N_LAT_SEQ, N_EXP, LAT_LEN)], axis=0)
    slot_t = jnp.pad(jnp.transpose(slot, (0, 2, 1)), ((0, 0), (0, 0), (0, 128 - N_EXP)),
                     constant_values=-1)
    slot = slot.reshape(N_GROUPS, N_EXP, TILES, TOK_TILE)
    cnt_ctx = jnp.broadcast_to(jnp.arange(CNT_STRIDE, dtype=jnp.int32) * CAP_CTX, (1, N_EXP, CNT_STRIDE))
    cnt = jnp.concatenate([cnt_ctx, base_lat[:, :CNT_STRIDE].reshape(N_LAT_SEQ, N_EXP, CNT_STRIDE)],
                          axis=0).reshape(-1)
    xs, gs = moe_gather(cnt, slot, aff.reshape(N_EXP, N_TOK // TOK_TILE, TOK_TILE), h2)
    ys = moe_ffn(xs, gs, w_gate, w_up, w_down, layer)
    return moe_scatter(cnt, slot_t, ys, x, g2)


def _final_norm_kernel(x_ref, g_ref, ctx_ref, lat_ref):
    x = x_ref[...]
    y = x * lax.rsqrt(jnp.mean(x * x, axis=-1, keepdims=True) + NORM_EPS) * g_ref[...]
    is_ctx = pl.program_id(0) < TILES_PER_GROUP

    @pl.when(is_ctx)
    def _():
        ctx_ref[...] = y

    @pl.when(jnp.logical_not(is_ctx))
    def _():
        lat_ref[...] = y


def final_norm(x, g):
    t = TILES_PER_GROUP
    return pl.pallas_call(
        _final_norm_kernel,
        grid=(N_TOK // TM,),
        in_specs=[pl.BlockSpec((TM, D), lambda i: (i, 0)),
                  pl.BlockSpec((1, D), lambda i: (0, 0))],
        out_specs=[pl.BlockSpec((TM, D), lambda i: (jnp.minimum(i, t - 1), 0)),
                   pl.BlockSpec((TM, D), lambda i: (jnp.maximum(i - t, 0), 0))],
        out_shape=[jax.ShapeDtypeStruct((GROUP_ROWS, D), F32),
                   jax.ShapeDtypeStruct((N_TOK - GROUP_ROWS, D), F32)],
        compiler_params=_cparams("arbitrary"),
        name="final_norm",
    )(x, g.reshape(1, D))


def _grid_pos_embed():
    rows = LAT_LEN // GRID_W
    quarter = D // 4
    omega = 1.0 / (10000.0 ** (jnp.arange(quarter, dtype=F32) / quarter))
    r = jnp.arange(rows, dtype=F32)[:, None] * omega
    cl = jnp.arange(GRID_W, dtype=F32)[:, None] * omega
    emb_r = jnp.concatenate([jnp.sin(r), jnp.cos(r)], axis=-1)
    emb_c = jnp.concatenate([jnp.sin(cl), jnp.cos(cl)], axis=-1)
    emb = jnp.concatenate([jnp.broadcast_to(emb_r[:, None], (rows, GRID_W, D // 2)),
                           jnp.broadcast_to(emb_c[None], (rows, GRID_W, D // 2))], axis=-1)
    return emb.reshape(LAT_LEN, D)


def kernel(x_prompt, x_sample, state_s5_re, state_s5_im, state_hgrn, state_ssd, c, c_ctx, w_ada, b_ada, norm_mix, norm_ffn, norm_final, s5_lam_re, s5_lam_im, s5_log_dt, s5_b_re, s5_b_im, s5_c_re, s5_c_im, s5_d, s5_w_glu, s5_b_glu, hg_w_qig, hg_w_f, hg_b_f, hg_lb_logits, hg_norm, hg_w_o, ssd_w_in, ssd_conv_w, ssd_conv_b, ssd_dt_bias, ssd_a_log, ssd_d, ssd_norm, ssd_w_out, moe_router, moe_w_gate, moe_w_up, moe_w_down):
    cond8 = jnp.concatenate([c_ctx[None], c, jnp.zeros((5, D), F32)], axis=0)
    mod = ada_mod(cond8, w_ada, b_ada)
    mods = jnp.transpose(mod.reshape(DEPTH, 8, 6, D)[:, :3], (0, 2, 1, 3)).reshape(DEPTH, 6, 3, 1, D)
    x = embed_tokens(x_prompt.reshape(-1, D), x_sample.reshape(-1, D), _grid_pos_embed())
    s5_re, s5_im, hg_fin, ssd_fin = [], [], [], []
    for i in range(DEPTH):
        mix_mods = (mods[i, 0], mods[i, 1], mods[i, 2])
        kind, j = i % 3, i // 3
        if kind == 0:
            x, fr, fi = s5_layer(x, mix_mods, norm_mix[i], state_s5_re[:, j], state_s5_im[:, j],
                                 s5_lam_re[j], s5_lam_im[j], s5_log_dt[j], s5_b_re[j], s5_b_im[j],
                                 s5_c_re[j], s5_c_im[j], s5_d[j], s5_w_glu[j], s5_b_glu[j])
            s5_re.append(fr)
            s5_im.append(fi)
        elif kind == 1:
            x, fh = hgrn_layer(x, mix_mods, norm_mix[i], state_hgrn[:, j], i, hg_lb_logits,
                               hg_w_qig[j], hg_w_f[j], hg_b_f[j], hg_norm[j], hg_w_o[j])
            hg_fin.append(fh)
        else:
            x, fs = ssd_layer(x, mix_mods, norm_mix[i], state_ssd[:, j], ssd_w_in[j], ssd_conv_w[j],
                              ssd_conv_b[j], ssd_dt_bias[j], ssd_a_log[j], ssd_d[j], ssd_norm[j],
                              ssd_w_out[j])
            ssd_fin.append(fs)
        x = moe_layer(x, (mods[i, 3], mods[i, 4], mods[i, 5]), norm_ffn[i], i, moe_router[i],
                      moe_w_gate, moe_w_up, moe_w_down)
    y_ctx, y_lat = final_norm(x, norm_final)
    return (y_ctx.reshape(N_CTX_SEQ, CTX_LEN, D), y_lat.reshape(N_LAT_SEQ, LAT_LEN, D),
            jnp.stack(s5_re, axis=1), jnp.stack(s5_im, axis=1),
            jnp.stack(hg_fin, axis=1), jnp.stack(ssd_fin, axis=1))
```

```python
import functools
import math

import jax
import jax.numpy as jnp
from jax import lax
from jax.experimental import pallas as pl
from jax.experimental.pallas import tpu as pltpu

F32 = jnp.float32
BF16 = jnp.bfloat16
HIGHEST = lax.Precision.HIGHEST

D = 1024
DEPTH = 4
N_CTX_SEQ = 16
CTX_LEN = 256
N_LAT_SEQ = 2
LAT_LEN = 4096
GROUP_ROWS = 4096
N_GROUPS = 3
N_TOK = N_GROUPS * GROUP_ROWS
N_SEQ = N_CTX_SEQ + N_LAT_SEQ
NORM_EPS = 1e-6
GRID_W = 64

VMEM_LIMIT_BYTES = 56 * 1024 * 1024


def _cparams(*sem):
    return pltpu.CompilerParams(dimension_semantics=sem, vmem_limit_bytes=VMEM_LIMIT_BYTES)


def _silu(x):
    return x * jax.nn.sigmoid(x)


def _normmod(x, g, sc, sh):
    ms = jnp.mean(x * x, axis=-1, keepdims=True)
    return x * lax.rsqrt(ms + NORM_EPS) * g * (1.0 + sc) + sh


def _cmul(ar, ai, br, bi):
    return ar * br - ai * bi, ar * bi + ai * br


def _mod_kernel(c_ref, w_ref, b_ref, o_ref):
    o_ref[0] = jnp.dot(_silu(c_ref[...]), w_ref[0], precision=HIGHEST,
                       preferred_element_type=F32) + b_ref[0]


def ada_mod(cond8, w_ada, b_ada):
    tn = 1536
    return pl.pallas_call(
        _mod_kernel,
        grid=(DEPTH, 6 * D // tn),
        in_specs=[pl.BlockSpec((8, D), lambda i, j: (0, 0)),
                  pl.BlockSpec((1, D, tn), lambda i, j: (i, 0, j)),
                  pl.BlockSpec((1, 1, tn), lambda i, j: (i, 0, j))],
        out_specs=pl.BlockSpec((1, 8, tn), lambda i, j: (i, 0, j)),
        out_shape=jax.ShapeDtypeStruct((DEPTH, 8, 6 * D), F32),
        compiler_params=_cparams("parallel", "parallel"),
        name="ada_mod",
    )(cond8, w_ada, b_ada.reshape(DEPTH, 1, 6 * D))


def _embed_kernel(xp_ref, xs_ref, pos_ref, o_ref):
    r = pl.program_id(0)

    @pl.when(r == 0)
    def _():
        o_ref[...] = xp_ref[...]

    @pl.when(r > 0)
    def _():
        o_ref[...] = xs_ref[...] + pos_ref[...]


def embed_tokens(xp, xs, pos):
    tm = 1024
    nt = GROUP_ROWS // tm
    return pl.pallas_call(
        _embed_kernel,
        grid=(N_GROUPS, nt),
        in_specs=[pl.BlockSpec((tm, D), lambda r, i: (jnp.where(r == 0, i, 0), 0)),
                  pl.BlockSpec((tm, D), lambda r, i: (jnp.where(r == 0, 0, (r - 1) * nt + i), 0)),
                  pl.BlockSpec((tm, D), lambda r, i: (i, 0))],
        out_specs=pl.BlockSpec((tm, D), lambda r, i: (r * nt + i, 0)),
        out_shape=jax.ShapeDtypeStruct((N_TOK, D), F32),
        compiler_params=_cparams("parallel", "parallel"),
        name="embed_tokens",
    )(xp, xs, pos)


TM = 1024
TILES_PER_GROUP = GROUP_ROWS // TM


def _group_of_tile(i):
    return i // TILES_PER_GROUP


def _normmod_kernel(x_ref, g_ref, sc_ref, sh_ref, o_ref):
    o_ref[...] = _normmod(x_ref[...], g_ref[...], sc_ref[0], sh_ref[0])


def normmod(x, g, sc, sh):
    return pl.pallas_call(
        _normmod_kernel,
        grid=(N_TOK // TM,),
        in_specs=[pl.BlockSpec((TM, D), lambda i: (i, 0)),
                  pl.BlockSpec((1, D), lambda i: (0, 0)),
                  pl.BlockSpec((1, 1, D), lambda i: (_group_of_tile(i), 0, 0)),
                  pl.BlockSpec((1, 1, D), lambda i: (_group_of_tile(i), 0, 0))],
        out_specs=pl.BlockSpec((TM, D), lambda i: (i, 0)),
        out_shape=jax.ShapeDtypeStruct((N_TOK, D), F32),
        compiler_params=_cparams("parallel"),
        name="normmod",
    )(x, g.reshape(1, D), sc, sh)


def _nm_matmul_kernel(x_ref, g_ref, sc_ref, sh_ref, w_ref, b_ref, o_ref, h_scr):
    @pl.when(pl.program_id(1) == 0)
    def _():
        h_scr[...] = _normmod(x_ref[...], g_ref[...], sc_ref[0], sh_ref[0]).astype(BF16)

    o_ref[...] = jnp.dot(h_scr[...], w_ref[...].astype(BF16),
                         preferred_element_type=F32) + b_ref[...]


def nm_matmul(x, g, sc, sh, w, b, n_out, tn=1024, name="nm_matmul"):
    return pl.pallas_call(
        _nm_matmul_kernel,
        grid=(N_TOK // TM, n_out // tn),
        in_specs=[pl.BlockSpec((TM, D), lambda i, j: (i, 0)),
                  pl.BlockSpec((1, D), lambda i, j: (0, 0)),
                  pl.BlockSpec((1, 1, D), lambda i, j: (_group_of_tile(i), 0, 0)),
                  pl.BlockSpec((1, 1, D), lambda i, j: (_group_of_tile(i), 0, 0)),
                  pl.BlockSpec((D, tn), lambda i, j: (0, j)),
                  pl.BlockSpec((1, tn), lambda i, j: (0, j))],
        out_specs=pl.BlockSpec((TM, tn), lambda i, j: (i, j)),
        out_shape=jax.ShapeDtypeStruct((N_TOK, n_out), F32),
        scratch_shapes=[pltpu.VMEM((TM, D), BF16)],
        compiler_params=_cparams("parallel", "arbitrary"),
        name=name,
    )(x, g.reshape(1, D), sc, sh, w.astype(BF16), b.reshape(1, -1))


S5_G = 64
S5_H = 16
S5_P = 64
S5_L = 16
S5_GB = 8
S5_NB = S5_G // S5_GB
S5_CH = GROUP_ROWS // S5_L
S5_CTX_CH = CTX_LEN // S5_L
S5_NPOW = 8
S5_XW = S5_L * 128
S5_SW = 2 * S5_GB * S5_P


S5_LH = S5_L * S5_H
S5_NPWR = 24


def _s5_prep_kernel(lr_ref, li_ref, ldt_ref, btr_ref, bti_ref, cxr_ref, cxi_ref,
                    sel0_ref, sel1_ref, exp_ref,
                    kk_ref, wst_ref, wout_ref, apr_ref, api_ref):
    dt = jnp.exp(ldt_ref[0])
    lam_r, lam_i = lr_ref[0], li_ref[0]
    ar, ai = lam_r * dt, lam_i * dt
    pw = lax.broadcasted_iota(jnp.int32, (S5_GB, S5_NPWR, 2 * S5_P), 1).astype(F32)
    ep = jnp.exp(pw * ar)
    pwr, pwi = ep * jnp.cos(pw * ai), ep * jnp.sin(pw * ai)
    den = lam_r * lam_r + lam_i * lam_i
    nr, ni = pwr[:, 1:2] - 1.0, pwi[:, 1:2]
    beta_r = (nr * lam_r + ni * lam_i) / den
    beta_i = (ni * lam_r - nr * lam_i) / den
    bbr, bbi = _cmul(beta_r, beta_i, btr_ref[0], bti_ref[0])

    lane = lax.broadcasted_iota(jnp.int32, (S5_H, 2 * S5_P), 1)
    wst_ref[0, 0] = jnp.zeros((S5_XW, S5_SW), BF16)
    half = S5_SW // 2
    for g in range(S5_GB):
        mine = (lane >= S5_P) if g % 2 else (lane < S5_P)
        col = 128 * (g // 2)
        for s in range(S5_L):
            k = S5_L - 1 - s
            wr, wi = _cmul(pwr[g, k:k + 1], pwi[g, k:k + 1], bbr[g], bbi[g])
            rows = slice(128 * s + S5_H * g, 128 * s + S5_H * (g + 1))
            wst_ref[0, 0, rows, col:col + 128] = jnp.where(mine, wr, 0.0).astype(BF16)
            wst_ref[0, 0, rows, half + col:half + col + 128] = jnp.where(mine, wi, 0.0).astype(BF16)

    lane1 = lax.broadcasted_iota(jnp.int32, (1, 2 * S5_P), 1)

    def group_lanes(a):
        return jnp.concatenate([jnp.where(lane1 < S5_P, a[2 * q], a[2 * q + 1])
                                for q in range(S5_GB // 2)], axis=1)

    pr, pi_ = pwr[:, S5_L:S5_L + 1], pwi[:, S5_L:S5_L + 1]
    for k in range(S5_NPOW):
        apr_ref[0, 0, k:k + 1, :] = group_lanes(pr)
        api_ref[0, 0, k:k + 1, :] = group_lanes(pi_)
        pr, pi_ = _cmul(pr, pi_, pr, pi_)

    tn = (((0,), (0,)), ((), ()))
    kks, wre, wim = [], [], []
    for g in range(S5_GB):
        pg_r, pg_i = pwr[g, :, :S5_P], pwi[g, :, :S5_P]
        cr, ci = cxr_ref[0, g], cxi_ref[0, g]

        def c_times_pow(sel):
            er = lax.dot_general(pg_r, sel, tn, precision=HIGHEST, preferred_element_type=F32)
            ei = lax.dot_general(pg_i, sel, tn, precision=HIGHEST, preferred_element_type=F32)
            return _cmul(cr, ci, er, ei)

        k_r, k_i = c_times_pow(sel0_ref[...])
        kks.append(jnp.dot(bbr[g, :, :S5_P], k_r, precision=HIGHEST, preferred_element_type=F32)
                   - jnp.dot(bbi[g, :, :S5_P], k_i, precision=HIGHEST, preferred_element_type=F32))
        o_r, o_i = c_times_pow(sel1_ref[...])
        wre.append(o_r)
        wim.append(-o_i)
    glane = (lax.broadcasted_iota(jnp.int32, (1, S5_XW), 1) % 128) // S5_H

    def spread(parts, rows_per_group):
        a = jnp.concatenate(parts, axis=0).astype(BF16)
        a = jnp.dot(a, exp_ref[...], preferred_element_type=F32)
        grow = lax.broadcasted_iota(jnp.int32, (a.shape[0], 1), 0) // rows_per_group % S5_GB
        return jnp.where(grow == glane, a, 0.0).astype(BF16)

    wout_ref[0, 0] = spread(wre + wim, S5_P)
    kk_ref[0, 0] = spread(kks, S5_H)


def s5_prepare(lam_re, lam_im, log_dt, b_re, b_im, c_re, c_im):
    half = S5_SW // 2
    lr = jnp.tile(lam_re.reshape(2, S5_G, 1, S5_P), (1, 1, 1, 2))
    li = jnp.tile(lam_im.reshape(2, S5_G, 1, S5_P), (1, 1, 1, 2))
    ldt = log_dt.reshape(2, S5_G, 1, 1)
    btr = jnp.tile(jnp.swapaxes(b_re, 2, 3), (1, 1, 1, 2))
    bti = jnp.tile(jnp.swapaxes(b_im, 2, 3), (1, 1, 1, 2))
    cxr = jnp.tile(jnp.swapaxes(c_re, 2, 3), (1, 1, 1, S5_L))
    cxi = jnp.tile(jnp.swapaxes(c_im, 2, 3), (1, 1, 1, S5_L))
    k = jnp.arange(S5_NPWR)[:, None]
    t = (jnp.arange(S5_LH) // S5_H)[None, :]
    sel0 = (k == t).astype(F32)
    sel1 = (k == t + 1).astype(F32)
    src = jnp.arange(S5_LH)[:, None]
    dst = jnp.arange(S5_XW)[None, :]
    expand = ((src // S5_H == dst // 128) & (src % S5_H == dst % S5_H)).astype(BF16)

    def spec(*tail):
        return pl.BlockSpec((1, S5_GB) + tail, lambda d, j: (d, j) + (0,) * len(tail))

    def const(shape):
        return pl.BlockSpec(shape, lambda d, j: (0,) * len(shape))

    def blk(*tail):
        return pl.BlockSpec((1, 1) + tail, lambda d, j: (d, j) + (0,) * len(tail))

    return pl.pallas_call(
        _s5_prep_kernel,
        grid=(2, S5_NB),
        in_specs=[spec(1, 2 * S5_P), spec(1, 2 * S5_P), spec(1, 1),
                  spec(S5_H, 2 * S5_P), spec(S5_H, 2 * S5_P), spec(S5_P, S5_LH), spec(S5_P, S5_LH),
                  const((S5_NPWR, S5_LH)), const((S5_NPWR, S5_LH)), const((S5_LH, S5_XW))],
        out_specs=[blk(128, S5_XW), blk(S5_XW, S5_SW), blk(S5_SW, S5_XW),
                   blk(S5_NPOW, half), blk(S5_NPOW, half)],
        out_shape=[jax.ShapeDtypeStruct((2, S5_NB, 128, S5_XW), BF16),
                   jax.ShapeDtypeStruct((2, S5_NB, S5_XW, S5_SW), BF16),
                   jax.ShapeDtypeStruct((2, S5_NB, S5_SW, S5_XW), BF16),
                   jax.ShapeDtypeStruct((2, S5_NB, S5_NPOW, half), F32),
                   jax.ShapeDtypeStruct((2, S5_NB, S5_NPOW, half), F32)],
        compiler_params=_cparams("parallel", "parallel"),
        name="s5_prepare",
    )(lr, li, ldt, btr, bti, cxr, cxi, sel0, sel1, expand)


def _s5_scan_body(d, r, h_ref, kk_ref, wst_ref, wout_ref, apr_ref, api_ref, h0r_ref, h0i_ref,
                  y_ref, fr_ref, fi_ref, m8, zr_s, zi_s):
    half = S5_SW // 2

    @pl.when(r == 0)
    def _build():
        for s in range(S5_L):
            if s:
                m8[128 * s:128 * (s + 1), 0:128 * s] = jnp.zeros((128, 128 * s), BF16)
            m8[128 * s:128 * (s + 1), 128 * s:] = kk_ref[0, 0, :, :S5_XW - 128 * s]

    def tloc(s):
        return s if d == 0 else S5_L - 1 - s

    slabs = [h_ref[pl.ds(tloc(s), S5_CH, stride=S5_L), :] for s in range(S5_L)]
    x8 = jnp.concatenate(slabs, axis=1).astype(BF16)
    delta = jnp.dot(x8, wst_ref[0, 0], preferred_element_type=F32)
    zr, zi = delta[:, :half], delta[:, half:]

    row = lax.broadcasted_iota(jnp.int32, (S5_CH, 1), 0)
    is_ctx = r == 0
    pos = jnp.where(is_ctx, row & (S5_CTX_CH - 1), row)
    last = jnp.where(is_ctx, S5_CTX_CH - 1, S5_CH - 1)
    a = jnp.maximum(r - 1, 0)
    lat = (r > 0).astype(F32)
    h0r = h0r_ref[0, 0, pl.ds(a, 1), :] * lat
    h0i = h0i_ref[0, 0, pl.ds(a, 1), :] * lat
    first = (pos == 0) if d == 0 else (pos == last)
    ir, ii = _cmul(apr_ref[0, 0, 0:1, :], api_ref[0, 0, 0:1, :], h0r, h0i)
    zr = zr + jnp.where(first, ir, 0.0)
    zi = zi + jnp.where(first, ii, 0.0)
    for k in range(S5_NPOW):
        m = 1 << k
        akr, aki = apr_ref[0, 0, k:k + 1, :], api_ref[0, 0, k:k + 1, :]
        if d == 0:
            sr, si = pltpu.roll(zr, m, 0), pltpu.roll(zi, m, 0)
            valid = pos >= m
        else:
            sr, si = pltpu.roll(zr, S5_CH - m, 0), pltpu.roll(zi, S5_CH - m, 0)
            valid = pos <= last - m
        pr, pi_ = _cmul(akr, aki, sr, si)
        zr = zr + jnp.where(valid, pr, 0.0)
        zi = zi + jnp.where(valid, pi_, 0.0)
    if d == 0:
        sr, si = pltpu.roll(zr, 1, 0), pltpu.roll(zi, 1, 0)
    else:
        sr, si = pltpu.roll(zr, S5_CH - 1, 0), pltpu.roll(zi, S5_CH - 1, 0)
    sr = jnp.where(first, h0r, sr)
    si = jnp.where(first, h0i, si)
    s_in = jnp.concatenate([sr, si], axis=1).astype(BF16)
    y8 = (jnp.dot(x8, m8[...], preferred_element_type=F32)
          + jnp.dot(s_in, wout_ref[0, 0], preferred_element_type=F32))
    for t in range(S5_L):
        y_ref[0, pl.ds(tloc(t), S5_CH, stride=S5_L), :] = y8[:, 128 * t:128 * (t + 1)]

    @pl.when(r == 0)
    def _fin():
        off = S5_CTX_CH - 1 if d == 0 else 0
        for q in range(half // 128):
            zr_s[q] = zr[:, 128 * q:128 * (q + 1)]
            zi_s[q] = zi[:, 128 * q:128 * (q + 1)]
            fr_ref[0, 0, :, 128 * q:128 * (q + 1)] = zr_s[q, pl.ds(off, N_CTX_SEQ, stride=S5_CTX_CH), :]
            fi_ref[0, 0, :, 128 * q:128 * (q + 1)] = zi_s[q, pl.ds(off, N_CTX_SEQ, stride=S5_CTX_CH), :]


def _s5_scan_kernel(*refs):
    d = pl.program_id(1)
    r = pl.program_id(2)

    @pl.when(d == 0)
    def _():
        _s5_scan_body(0, r, *refs)

    @pl.when(d == 1)
    def _():
        _s5_scan_body(1, r, *refs)


def s5_scan(h, kk8, wst8, wout8, apr, api, h0r, h0i):
    half = S5_SW // 2
    return pl.pallas_call(
        _s5_scan_kernel,
        grid=(S5_NB, 2, N_GROUPS),
        in_specs=[pl.BlockSpec((GROUP_ROWS, 128), lambda j, d, r: (r, j)),
                  pl.BlockSpec((1, 1, 128, S5_XW), lambda j, d, r: (d, j, 0, 0)),
                  pl.BlockSpec((1, 1, S5_XW, S5_SW), lambda j, d, r: (d, j, 0, 0)),
                  pl.BlockSpec((1, 1, S5_SW, S5_XW), lambda j, d, r: (d, j, 0, 0)),
                  pl.BlockSpec((1, 1, S5_NPOW, half), lambda j, d, r: (d, j, 0, 0)),
                  pl.BlockSpec((1, 1, S5_NPOW, half), lambda j, d, r: (d, j, 0, 0)),
                  pl.BlockSpec((1, 1, N_LAT_SEQ, half), lambda j, d, r: (d, j, 0, 0)),
                  pl.BlockSpec((1, 1, N_LAT_SEQ, half), lambda j, d, r: (d, j, 0, 0))],
        out_specs=[pl.BlockSpec((1, GROUP_ROWS, 128), lambda j, d, r: (d, r, j)),
                   pl.BlockSpec((1, 1, N_CTX_SEQ, half), lambda j, d, r: (d, j, 0, 0)),
                   pl.BlockSpec((1, 1, N_CTX_SEQ, half), lambda j, d, r: (d, j, 0, 0))],
        out_shape=[jax.ShapeDtypeStruct((2, N_TOK, D), F32),
                   jax.ShapeDtypeStruct((2, S5_NB, N_CTX_SEQ, half), F32),
                   jax.ShapeDtypeStruct((2, S5_NB, N_CTX_SEQ, half), F32)],
        scratch_shapes=[pltpu.VMEM((S5_XW, S5_XW), BF16),
                        pltpu.VMEM((half // 128, S5_CH, 128), F32),
                        pltpu.VMEM((half // 128, S5_CH, 128), F32)],
        compiler_params=_cparams("arbitrary", "arbitrary", "arbitrary"),
        name="s5_scan",
    )(h, kk8, wst8, wout8, apr, api, h0r, h0i)


def _s5_glu_kernel(h_ref, y0_ref, y1_ref, dsk_ref, wa_ref, wb_ref, ba_ref, bb_ref, x_ref, g1_ref,
                   o_ref, yg_scr):
    @pl.when(pl.program_id(1) == 0)
    def _():
        y = dsk_ref[...] * h_ref[...] + y0_ref[0] + y1_ref[0]
        yg_scr[...] = jax.nn.gelu(y).astype(BF16)

    yg = yg_scr[...]
    a = jnp.dot(yg, wa_ref[...].astype(BF16), preferred_element_type=F32) + ba_ref[...]
    b = jnp.dot(yg, wb_ref[...].astype(BF16), preferred_element_type=F32) + bb_ref[...]
    o_ref[...] = x_ref[...] + g1_ref[0] * (a * jax.nn.sigmoid(b))


def s5_glu(h, y, d_skip, w_glu, b_glu, x, g1, tn=512):
    nj = D // tn
    b2 = b_glu.reshape(1, 2 * D)
    return pl.pallas_call(
        _s5_glu_kernel,
        grid=(N_TOK // TM, nj),
        in_specs=[pl.BlockSpec((TM, D), lambda i, j: (i, 0)),
                  pl.BlockSpec((1, TM, D), lambda i, j: (0, i, 0)),
                  pl.BlockSpec((1, TM, D), lambda i, j: (1, i, 0)),
                  pl.BlockSpec((1, D), lambda i, j: (0, 0)),
                  pl.BlockSpec((D, tn), lambda i, j: (0, j)),
                  pl.BlockSpec((D, tn), lambda i, j: (0, nj + j)),
                  pl.BlockSpec((1, tn), lambda i, j: (0, j)),
                  pl.BlockSpec((1, tn), lambda i, j: (0, nj + j)),
                  pl.BlockSpec((TM, tn), lambda i, j: (i, j)),
                  pl.BlockSpec((1, 1, tn), lambda i, j: (_group_of_tile(i), 0, j))],
        out_specs=pl.BlockSpec((TM, tn), lambda i, j: (i, j)),
        out_shape=jax.ShapeDtypeStruct((N_TOK, D), F32),
        scratch_shapes=[pltpu.VMEM((TM, D), BF16)],
        compiler_params=_cparams("parallel", "arbitrary"),
        name="s5_glu",
    )(h, y, y, d_skip.reshape(1, D), w_glu, w_glu, b2, b2, x, g1)


def s5_layer(x, mods, g_norm, st_re, st_im, lam_re, lam_im, log_dt, b_re, b_im, c_re, c_im,
             d_skip, w_glu, b_glu):
    sh1, sc1, g1 = mods
    h = normmod(x, g_norm, sc1, sh1)
    kk8, wst8, wout8, apr, api = s5_prepare(lam_re, lam_im, log_dt, b_re, b_im, c_re, c_im)

    def h0(st):
        return jnp.transpose(st.reshape(N_LAT_SEQ, 2, S5_NB, S5_GB * S5_P), (1, 2, 0, 3))

    y, fr, fi = s5_scan(h, kk8, wst8, wout8, apr, api, h0(st_re), h0(st_im))

    def fin(f):
        return jnp.transpose(f.reshape(2, S5_NB, N_CTX_SEQ, S5_GB, S5_P), (2, 0, 1, 3, 4)
                             ).reshape(N_CTX_SEQ, 2, S5_G, S5_P)

    x = s5_glu(h, y, d_skip, w_glu, b_glu, x, g1)
    return x, fin(fr), fin(fi)


CHUNK = 64
N_CHUNKS = N_TOK // CHUNK
CTX_CHUNKS = N_CTX_SEQ * CTX_LEN // CHUNK
CH_PER_CTX = CTX_LEN // CHUNK
CH_PER_LAT = LAT_LEN // CHUNK


def _chunk_of_step(d, c):
    return jnp.where(d == 0, c, N_CHUNKS - 1 - c)


def _seq_of_chunk(ce):
    return jnp.where(ce < CTX_CHUNKS, ce // CH_PER_CTX, N_CTX_SEQ + (ce - CTX_CHUNKS) // CH_PER_LAT)


def _chunk_flags(d, ce):
    is_ctx = ce < CTX_CHUNKS
    pos = jnp.where(is_ctx, ce % CH_PER_CTX, (ce - CTX_CHUNKS) % CH_PER_LAT)
    n = jnp.where(is_ctx, CH_PER_CTX, CH_PER_LAT)
    t_first, t_last = pos == 0, pos == n - 1
    if d == 0:
        return is_ctx, t_first, t_last
    return is_ctx, t_last, t_first


def _tri(d, shape, row_axis=0, col_axis=1):
    r = lax.broadcasted_iota(jnp.int32, shape, row_axis)
    c = lax.broadcasted_iota(jnp.int32, shape, col_axis)
    return (r >= c) if d == 0 else (r <= c)


HG_H = 8
HG_K = 128
HG_SAFE_SPAN = 60.0


def _hg_decay(d, layer, z_ref, lbl_ref):
    lg = lbl_ref[d]
    e = jnp.exp(lg - jnp.max(lg, axis=0, keepdims=True))
    sm = e / jnp.sum(e, axis=0, keepdims=True)
    lb = jnp.sum(sm[1:layer + 1], axis=0, keepdims=True)
    f = lb + (1.0 - lb) * jax.nn.sigmoid(z_ref[...])
    g = jnp.log(f)
    g_hi = g.astype(BF16)
    g_lo = (g - g_hi.astype(F32)).astype(BF16)
    tri = _tri(d, (CHUNK, CHUNK)).astype(BF16)
    cum = (jnp.dot(tri, g_hi, preferred_element_type=F32)
           + jnp.dot(tri, g_lo, preferred_element_type=F32))
    return 1.0 - f, cum


def _hg_main(d, layer, q_ref, v_ref, z_ref, lbl_ref, o_ref, st_scr, inter_scr):
    kk, cum = _hg_decay(d, layer, z_ref, lbl_ref)
    tri = _tri(d, (CHUNK, CHUNK))
    head, tot = (cum[0:1], cum[CHUNK - 1:CHUNK]) if d == 0 else (cum[CHUNK - 1:CHUNK], cum[0:1])
    mid = cum[CHUNK // 2 - 1:CHUNK // 2]
    q = q_ref[...]
    v = v_ref[...]
    qa = (q * jnp.exp(cum - mid)).astype(BF16)
    ka = (kk * jnp.exp(mid - cum)).astype(BF16)
    qs = (q * jnp.exp(cum)).astype(BF16)
    kd = (kk * jnp.exp(tot - cum)).astype(BF16)
    vb = v.astype(BF16)
    etot = jnp.exp(tot)
    nt = (((1,), (1,)), ((), ()))
    tn = (((0,), (0,)), ((), ()))
    for hd in range(HG_H):
        sl = slice(HG_K * hd, HG_K * (hd + 1))
        a = lax.dot_general(qa[:, sl], ka[:, sl], nt, preferred_element_type=F32)
        a = jnp.where(tri, a, 0.0).astype(BF16)
        st = st_scr[hd]
        inter = lax.dot_general(qs[:, sl], st.astype(BF16), nt, preferred_element_type=F32)
        inter_scr[:, sl] = inter
        o_ref[:, sl] = jnp.dot(a, vb[:, sl], preferred_element_type=F32) + inter
        st_scr[hd] = st * etot[:, sl] + lax.dot_general(vb[:, sl], kd[:, sl], tn,
                                                        preferred_element_type=F32)

    return jnp.max(jnp.maximum(head - mid, mid - tot))


def _hg_exact_intra(d, layer, q_ref, v_ref, z_ref, lbl_ref, o_ref, inter_scr, cum_scr, k_scr):
    kk, cum = _hg_decay(d, layer, z_ref, lbl_ref)
    cum_scr[...] = cum
    k_scr[...] = kk
    q = q_ref[...]
    c_idx = lax.broadcasted_iota(jnp.int32, (D, 128), 0) // HG_K
    h_idx = lax.broadcasted_iota(jnp.int32, (D, 128), 1)
    head_sum = (c_idx == h_idx).astype(F32)
    c_idx_t = lax.broadcasted_iota(jnp.int32, (128, D), 1) // HG_K
    h_idx_t = lax.broadcasted_iota(jnp.int32, (128, D), 0)
    head_bcast = (c_idx_t == h_idx_t).astype(F32)
    row = lax.broadcasted_iota(jnp.int32, (CHUNK, 1), 0)

    def source_row(s, acc):
        seen = (row >= s) if d == 0 else (row <= s)
        w = jnp.exp(jnp.where(seen, cum - cum_scr[pl.ds(s, 1), :], -jnp.inf))
        p = q * k_scr[pl.ds(s, 1), :] * w
        a_s = jnp.dot(p, head_sum, precision=HIGHEST, preferred_element_type=F32)
        a_s = jnp.dot(a_s, head_bcast, precision=HIGHEST, preferred_element_type=F32)
        return acc + a_s * v_ref[pl.ds(s, 1), :]

    intra = lax.fori_loop(0, CHUNK, source_row, jnp.zeros((CHUNK, D), F32))
    o_ref[...] = inter_scr[...] + intra


def _hg_scan_kernel(q0, v0, z0, q1, v1, z1, lbl_ref, s00, s01, o0, o1, fin0, fin1,
                    st0, st1, inter0, inter1, cum_scr, k_scr, *, layer):
    c = pl.program_id(0)
    dirs = ((0, q0, v0, z0, s00, o0, fin0, st0, inter0), (1, q1, v1, z1, s01, o1, fin1, st1, inter1))
    flags = [_chunk_flags(d, _chunk_of_step(d, c)) for d in (0, 1)]

    def set_state(st, value_fn):
        st[...] = value_fn()

    for (d, q, v, z, s0, o, fin, st, inter), (is_ctx, starts, ends) in zip(dirs, flags):
        pl.when(starts & is_ctx)(functools.partial(set_state, st, lambda: jnp.zeros((HG_H, HG_K, HG_K), F32)))
        pl.when(starts & jnp.logical_not(is_ctx))(functools.partial(set_state, st, lambda s0=s0: s0[0, 0]))
    spans = [_hg_main(d, layer, q, v, z, lbl_ref, o, st, inter)
             for (d, q, v, z, s0, o, fin, st, inter) in dirs]
    for (d, q, v, z, s0, o, fin, st, inter), span in zip(dirs, spans):
        pl.when(span > HG_SAFE_SPAN)(functools.partial(
            _hg_exact_intra, d, layer, q, v, z, lbl_ref, o, inter, cum_scr, k_scr))

    def write_final(fin, st):
        for hd in range(HG_H):
            fin[0, hd] = st[hd].T

    for (d, q, v, z, s0, o, fin, st, inter), (is_ctx, starts, ends) in zip(dirs, flags):
        pl.when(ends)(functools.partial(write_final, fin, st))


def hg_scan(proj, lb_logits, s0t, layer):
    def tok(d, col):
        return lambda c: (_chunk_of_step(d, c), col)

    def lat_idx(d):
        return lambda c: (d, jnp.maximum(_seq_of_chunk(_chunk_of_step(d, c)) - N_CTX_SEQ, 0), 0, 0, 0)

    def fin_idx(d):
        return lambda c: (jnp.minimum(_seq_of_chunk(_chunk_of_step(d, c)), N_CTX_SEQ), 0, 0, 0)

    state = (HG_H, HG_K, HG_K)
    return pl.pallas_call(
        functools.partial(_hg_scan_kernel, layer=layer),
        grid=(N_CHUNKS,),
        in_specs=[pl.BlockSpec((CHUNK, D), tok(0, 0)), pl.BlockSpec((CHUNK, D), tok(0, 1)),
                  pl.BlockSpec((CHUNK, D), tok(0, 3)),
                  pl.BlockSpec((CHUNK, D), tok(1, 0)), pl.BlockSpec((CHUNK, D), tok(1, 1)),
                  pl.BlockSpec((CHUNK, D), tok(1, 4)),
                  pl.BlockSpec((2, DEPTH, D), lambda c: (0, 0, 0)),
                  pl.BlockSpec((1, 1) + state, lat_idx(0)), pl.BlockSpec((1, 1) + state, lat_idx(1))],
        out_specs=[pl.BlockSpec((CHUNK, D), tok(0, 0)), pl.BlockSpec((CHUNK, D), tok(1, 0)),
                   pl.BlockSpec((1,) + state, fin_idx(0)), pl.BlockSpec((1,) + state, fin_idx(1))],
        out_shape=[jax.ShapeDtypeStruct((N_TOK, D), F32), jax.ShapeDtypeStruct((N_TOK, D), F32),
                   jax.ShapeDtypeStruct((N_CTX_SEQ + 1,) + state, F32),
                   jax.ShapeDtypeStruct((N_CTX_SEQ + 1,) + state, F32)],
        scratch_shapes=[pltpu.VMEM(state, F32), pltpu.VMEM(state, F32),
                        pltpu.VMEM((CHUNK, D), F32), pltpu.VMEM((CHUNK, D), F32),
                        pltpu.VMEM((CHUNK, D), F32), pltpu.VMEM((CHUNK, D), F32)],
        compiler_params=_cparams("arbitrary"),
        name="hg_scan",
    )(proj, proj, proj, proj, proj, proj, lb_logits, s0t, s0t)


def _hg_out_kernel(o0_ref, o1_ref, gate_ref, gn_ref, w_ref, x_ref, g1_ref, out_ref, on_scr):
    @pl.when(pl.program_id(1) == 0)
    def _():
        for hd in range(HG_H):
            sl = slice(HG_K * hd, HG_K * (hd + 1))
            o = o0_ref[:, sl] + o1_ref[:, sl]
            o = o * lax.rsqrt(jnp.mean(o * o, axis=-1, keepdims=True) + NORM_EPS) * gn_ref[...]
            on_scr[:, sl] = (o * _silu(gate_ref[:, sl])).astype(BF16)

    out_ref[...] = x_ref[...] + g1_ref[0] * jnp.dot(on_scr[...], w_ref[...].astype(BF16),
                                                    preferred_element_type=F32)


def hg_out(o_fwd, o_bwd, proj, g_norm, w_o, x, g1, tn=512):
    return pl.pallas_call(
        _hg_out_kernel,
        grid=(N_TOK // TM, D // tn),
        in_specs=[pl.BlockSpec((TM, D), lambda i, j: (i, 0)),
                  pl.BlockSpec((TM, D), lambda i, j: (i, 0)),
                  pl.BlockSpec((TM, D), lambda i, j: (i, 2)),
                  pl.BlockSpec((1, HG_K), lambda i, j: (0, 0)),
                  pl.BlockSpec((D, tn), lambda i, j: (0, j)),
                  pl.BlockSpec((TM, tn), lambda i, j: (i, j)),
                  pl.BlockSpec((1, 1, tn), lambda i, j: (_group_of_tile(i), 0, j))],
        out_specs=pl.BlockSpec((TM, tn), lambda i, j: (i, j)),
        out_shape=jax.ShapeDtypeStruct((N_TOK, D), F32),
        scratch_shapes=[pltpu.VMEM((TM, D), BF16)],
        compiler_params=_cparams("parallel", "arbitrary"),
        name="hg_out",
    )(o_fwd, o_bwd, proj, g_norm.reshape(1, HG_K), w_o, x, g1)


def hgrn_layer(x, mods, g_mix, state, layer, lb_logits, w_qig, w_f, b_f, g_norm, w_o):
    sh1, sc1, g1 = mods
    w5 = jnp.concatenate([w_qig, w_f[0], w_f[1]], axis=1)
    b5 = jnp.concatenate([jnp.zeros((3 * D,), F32), b_f[0], b_f[1]])
    proj = nm_matmul(x, g_mix, sc1, sh1, w5, b5, 5 * D, name="hg_proj")
    s0t = jnp.transpose(state, (1, 0, 2, 4, 3))
    o_fwd, o_bwd, fin_fwd, fin_bwd = hg_scan(proj, lb_logits, s0t, layer)
    x = hg_out(o_fwd, o_bwd, proj, g_norm, w_o, x, g1)
    return x, jnp.stack([fin_fwd[:N_CTX_SEQ], fin_bwd[:N_CTX_SEQ]], axis=1)


SSD_INNER = 2 * D
SSD_HEADS = 32
SSD_P = 64
SSD_NG = 4
SSD_N = 128
SSD_XBC = SSD_INNER + 2 * SSD_NG * SSD_N
SSD_ZX = SSD_INNER + SSD_XBC
SSD_CONV = 5
CONV_TM = 256
CONV_HALO = 8


def _ssd_conv_kernel(cur_ref, prev_ref, next_ref, w_ref, b_ref, o_ref, ext):
    i = pl.program_id(0)
    n_ctx_tiles = N_CTX_SEQ * CTX_LEN // CONV_TM
    per_lat = LAT_LEN // CONV_TM
    is_ctx = i < n_ctx_tiles
    k = (i - n_ctx_tiles) % per_lat
    seq_start = is_ctx | (k == 0)
    seq_end = is_ctx | (k == per_lat - 1)
    ext[0:CONV_HALO] = jnp.where(seq_start, 0.0, prev_ref[...])
    ext[CONV_HALO:CONV_HALO + CONV_TM] = cur_ref[...]
    ext[CONV_HALO + CONV_TM:] = jnp.where(seq_end, 0.0, next_ref[...])
    acc = jnp.broadcast_to(b_ref[...], (CONV_TM, D))
    for t in range(SSD_CONV):
        acc = acc + w_ref[t:t + 1, :] * ext[pl.ds(CONV_HALO - SSD_CONV // 2 + t, CONV_TM), :]
    o_ref[...] = _silu(acc)


def ssd_conv(zx, conv_w, conv_b):
    nrb = N_TOK // CONV_HALO
    rpt = CONV_TM // CONV_HALO
    c0 = SSD_INNER // D
    return pl.pallas_call(
        _ssd_conv_kernel,
        grid=(N_TOK // CONV_TM, SSD_XBC // D),
        in_specs=[pl.BlockSpec((CONV_TM, D), lambda i, j: (i, c0 + j)),
                  pl.BlockSpec((CONV_HALO, D), lambda i, j: (jnp.maximum(i * rpt - 1, 0), c0 + j)),
                  pl.BlockSpec((CONV_HALO, D), lambda i, j: (jnp.minimum((i + 1) * rpt, nrb - 1), c0 + j)),
                  pl.BlockSpec((SSD_CONV, D), lambda i, j: (0, j)),
                  pl.BlockSpec((1, D), lambda i, j: (0, j))],
        out_specs=pl.BlockSpec((CONV_TM, D), lambda i, j: (i, j)),
        out_shape=jax.ShapeDtypeStruct((N_TOK, SSD_XBC), F32),
        scratch_shapes=[pltpu.VMEM((CONV_TM + 2 * CONV_HALO, D), F32)],
        compiler_params=_cparams("parallel", "parallel"),
        name="ssd_conv",
    )(zx, zx, zx, conv_w, conv_b.reshape(1, SSD_XBC))


def _ssd_main(d, xlo_ref, xhi_ref, bc_ref, dtr_ref, dtb_ref, alog_ref, y_ref, ht_scr):
    xr = dtr_ref[...] + dtb_ref[...]
    dt = jnp.maximum(xr, 0.0) + jnp.log(1.0 + jnp.exp(-jnp.abs(xr)))
    dta = dt * (-jnp.exp(alog_ref[...]))
    tri = _tri(d, (CHUNK, CHUNK))
    cum = jnp.dot(tri.astype(F32), dta, precision=HIGHEST, preferred_element_type=F32)
    r = lax.broadcasted_iota(jnp.int32, (CHUNK, 2 * CHUNK), 0)
    cc = lax.broadcasted_iota(jnp.int32, (CHUNK, 2 * CHUNK), 1)
    lo_half = cc < CHUNK
    ccm = jnp.where(lo_half, cc, cc - CHUNK)
    trit = (ccm >= r) if d == 0 else (ccm <= r)
    tn = (((0,), (0,)), ((), ()))
    nt = (((1,), (1,)), ((), ()))
    cumt_lo = lax.dot_general(dta, (trit & lo_half).astype(F32), tn, precision=HIGHEST,
                              preferred_element_type=F32)
    cumt_hi = lax.dot_general(dta, (trit & jnp.logical_not(lo_half)).astype(F32), tn,
                              precision=HIGHEST, preferred_element_type=F32)
    lane = lax.broadcasted_iota(jnp.int32, (CHUNK, 2 * SSD_P), 1)
    first_head = lane < SSD_P
    tri2 = (r >= ccm) if d == 0 else (r <= ccm)
    bc = bc_ref[...]
    for gq in range(SSD_NG):
        bg = bc[:, SSD_N * gq:SSD_N * (gq + 1)].astype(BF16)
        cg = bc[:, SSD_NG * SSD_N + SSD_N * gq:SSD_NG * SSD_N + SSD_N * (gq + 1)].astype(BF16)
        cb2 = lax.dot_general(cg, jnp.concatenate([bg, bg], axis=0), nt,
                              preferred_element_type=F32)
        for pp in range(4 * gq, 4 * gq + 4):
            h1 = SSD_HEADS * d + 2 * pp
            colp = jnp.where(first_head, cum[:, h1:h1 + 1], cum[:, h1 + 1:h1 + 2])
            rowp = cumt_lo[h1:h1 + 1, :] + cumt_hi[h1 + 1:h1 + 2, :]
            lmat = jnp.exp(jnp.where(tri2, colp - rowp, -jnp.inf))
            dtp = jnp.where(first_head, dt[:, h1:h1 + 1], dt[:, h1 + 1:h1 + 2])
            xref = xlo_ref if pp < 8 else xhi_ref
            c0 = 128 * (pp % 8)
            xdt = xref[:, c0:c0 + 128] * dtp
            rhs = jnp.concatenate([jnp.where(first_head, xdt, 0.0),
                                   jnp.where(first_head, 0.0, xdt)], axis=0).astype(BF16)
            y = jnp.dot((cb2 * lmat).astype(BF16), rhs, preferred_element_type=F32)
            ht = ht_scr[:, 128 * pp:128 * (pp + 1)]
            y = y + jnp.dot(cg, ht.astype(BF16), preferred_element_type=F32) * jnp.exp(colp)
            y_ref[:, 128 * pp:128 * (pp + 1)] = y
            totp = colp[CHUNK - 1:CHUNK] if d == 0 else colp[0:1]
            xw = (xdt * jnp.exp(totp - colp)).astype(BF16)
            ht_scr[:, 128 * pp:128 * (pp + 1)] = (
                ht * jnp.exp(totp) + lax.dot_general(bg, xw, tn, preferred_element_type=F32))


def _ssd_scan_kernel(xlo0, xhi0, bc0, dtr0, xlo1, xhi1, bc1, dtr1, dtb_ref, alog_ref, h00, h01,
                     y0, y1, fin0, fin1, ht0, ht1):
    c = pl.program_id(0)
    dirs = ((0, xlo0, xhi0, bc0, dtr0, h00, y0, fin0, ht0), (1, xlo1, xhi1, bc1, dtr1, h01, y1, fin1, ht1))
    flags = [_chunk_flags(d, _chunk_of_step(d, c)) for d in (0, 1)]

    def set_state(ht, value_fn):
        ht[...] = value_fn()

    for (d, xlo, xhi, bc, dtr, h0, y, fin, ht), (is_ctx, starts, ends) in zip(dirs, flags):
        pl.when(starts & is_ctx)(functools.partial(
            set_state, ht, lambda: jnp.zeros((SSD_N, SSD_HEADS * SSD_P), F32)))
        pl.when(starts & jnp.logical_not(is_ctx))(functools.partial(set_state, ht, lambda h0=h0: h0[0, 0]))
    for (d, xlo, xhi, bc, dtr, h0, y, fin, ht) in dirs:
        _ssd_main(d, xlo, xhi, bc, dtr, dtb_ref, alog_ref, y, ht)
    def write_final(fin, ht):
        fin[0] = ht[...]

    for (d, xlo, xhi, bc, dtr, h0, y, fin, ht), (is_ctx, starts, ends) in zip(dirs, flags):
        pl.when(ends)(functools.partial(write_final, fin, ht))


def ssd_scan(xbc, dtr, dt_bias, a_log, h0t):
    nh2 = 2 * SSD_HEADS
    hp = SSD_HEADS * SSD_P

    def tok(d, col):
        return lambda c: (_chunk_of_step(d, c), col)

    def lat_idx(d):
        return lambda c: (d, jnp.maximum(_seq_of_chunk(_chunk_of_step(d, c)) - N_CTX_SEQ, 0), 0, 0)

    def fin_idx(d):
        return lambda c: (jnp.minimum(_seq_of_chunk(_chunk_of_step(d, c)), N_CTX_SEQ), 0, 0)

    def tok_specs(d):
        return [pl.BlockSpec((CHUNK, D), tok(d, 0)), pl.BlockSpec((CHUNK, D), tok(d, 1)),
                pl.BlockSpec((CHUNK, D), tok(d, 2)), pl.BlockSpec((CHUNK, nh2), tok(d, 0))]

    return pl.pallas_call(
        _ssd_scan_kernel,
        grid=(N_CHUNKS,),
        in_specs=tok_specs(0) + tok_specs(1) + [
            pl.BlockSpec((1, nh2), lambda c: (0, 0)), pl.BlockSpec((1, nh2), lambda c: (0, 0)),
            pl.BlockSpec((1, 1, SSD_N, hp), lat_idx(0)), pl.BlockSpec((1, 1, SSD_N, hp), lat_idx(1))],
        out_specs=[pl.BlockSpec((CHUNK, hp), tok(0, 0)), pl.BlockSpec((CHUNK, hp), tok(1, 0)),
                   pl.BlockSpec((1, SSD_N, hp), fin_idx(0)), pl.BlockSpec((1, SSD_N, hp), fin_idx(1))],
        out_shape=[jax.ShapeDtypeStruct((N_TOK, hp), F32), jax.ShapeDtypeStruct((N_TOK, hp), F32),
                   jax.ShapeDtypeStruct((N_CTX_SEQ + 1, SSD_N, hp), F32),
                   jax.ShapeDtypeStruct((N_CTX_SEQ + 1, SSD_N, hp), F32)],
        scratch_shapes=[pltpu.VMEM((SSD_N, hp), F32), pltpu.VMEM((SSD_N, hp), F32)],
        compiler_params=_cparams("arbitrary"),
        name="ssd_scan",
    )(xbc, xbc, xbc, dtr, xbc, xbc, xbc, dtr, dt_bias.reshape(1, nh2), a_log.reshape(1, nh2), h0t, h0t)


SSD_OUT_TM = 512


def _ssd_out_kernel(xlo_ref, xhi_ref, zlo_ref, zhi_ref, y0_ref, y1_ref, dsk_ref, gn_ref, w_ref,
                    x_ref, g1_ref, out_ref, yn_scr):
    @pl.when(pl.program_id(1) == 0)
    def _():
        halves = []
        ss = jnp.zeros((SSD_OUT_TM, 1), F32)
        for k, (xr, zr) in enumerate(((xlo_ref, zlo_ref), (xhi_ref, zhi_ref))):
            sl = slice(D * k, D * (k + 1))
            y = dsk_ref[:, sl] * xr[...] + y0_ref[:, sl] + y1_ref[:, sl]
            y = y * _silu(zr[...])
            ss = ss + jnp.sum(y * y, axis=-1, keepdims=True)
            halves.append(y)
        scale = lax.rsqrt(ss / SSD_INNER + NORM_EPS)
        for k, y in enumerate(halves):
            sl = slice(D * k, D * (k + 1))
            yn_scr[:, sl] = (y * scale * gn_ref[:, sl]).astype(BF16)

    out_ref[...] = x_ref[...] + g1_ref[0] * jnp.dot(yn_scr[...], w_ref[...].astype(BF16),
                                                    preferred_element_type=F32)


def ssd_out(xbc, zx, y_fwd, y_bwd, d_skip_cols, g_norm, w_out, x, g1, tn=512):
    tm = SSD_OUT_TM
    tpg = GROUP_ROWS // tm
    return pl.pallas_call(
        _ssd_out_kernel,
        grid=(N_TOK // tm, D // tn),
        in_specs=[pl.BlockSpec((tm, D), lambda i, j: (i, 0)),
                  pl.BlockSpec((tm, D), lambda i, j: (i, 1)),
                  pl.BlockSpec((tm, D), lambda i, j: (i, 0)),
                  pl.BlockSpec((tm, D), lambda i, j: (i, 1)),
                  pl.BlockSpec((tm, SSD_INNER), lambda i, j: (i, 0)),
                  pl.BlockSpec((tm, SSD_INNER), lambda i, j: (i, 0)),
                  pl.BlockSpec((1, SSD_INNER), lambda i, j: (0, 0)),
                  pl.BlockSpec((1, SSD_INNER), lambda i, j: (0, 0)),
                  pl.BlockSpec((SSD_INNER, tn), lambda i, j: (0, j)),
                  pl.BlockSpec((tm, tn), lambda i, j: (i, j)),
                  pl.BlockSpec((1, 1, tn), lambda i, j: (i // tpg, 0, j))],
        out_specs=pl.BlockSpec((tm, tn), lambda i, j: (i, j)),
        out_shape=jax.ShapeDtypeStruct((N_TOK, D), F32),
        scratch_shapes=[pltpu.VMEM((tm, SSD_INNER), BF16)],
        compiler_params=_cparams("parallel", "arbitrary"),
        name="ssd_out",
    )(xbc, xbc, zx, zx, y_fwd, y_bwd, d_skip_cols, g_norm.reshape(1, SSD_INNER), w_out.astype(BF16), x, g1)


def ssd_layer(x, mods, g_mix, state, w_in, conv_w, conv_b, dt_bias, a_log, d_skip, g_norm, w_out):
    sh1, sc1, g1 = mods
    zx = nm_matmul(x, g_mix, sc1, sh1, w_in, jnp.zeros((SSD_ZX,), F32), SSD_ZX, name="ssd_proj")
    nh2 = 2 * SSD_HEADS
    dtr = nm_matmul(x, g_mix, sc1, sh1, w_in[:, SSD_ZX:], jnp.zeros((nh2,), F32), nh2, tn=nh2,
                    name="ssd_proj_dt")
    xbc = ssd_conv(zx, conv_w, conv_b)
    h0t = jnp.transpose(state, (1, 0, 4, 2, 3)).reshape(2, N_LAT_SEQ, SSD_N, SSD_HEADS * SSD_P)
    y_fwd, y_bwd, fin_fwd, fin_bwd = ssd_scan(xbc, dtr, dt_bias, a_log, h0t)
    dcols = jnp.repeat(d_skip, SSD_P).reshape(1, SSD_INNER)
    x = ssd_out(xbc, zx, y_fwd, y_bwd, dcols, g_norm, w_out, x, g1)
    fin = jnp.stack([fin_fwd[:N_CTX_SEQ], fin_bwd[:N_CTX_SEQ]], axis=1)
    fin = jnp.transpose(fin.reshape(N_CTX_SEQ, 2, SSD_N, SSD_HEADS, SSD_P), (0, 1, 3, 4, 2))
    return x, fin


N_EXP = 16
FF = 2 * D
CAP_CTX = 2 * CTX_LEN // N_EXP
CAP_LAT = 2 * LAT_LEN // N_EXP
SLOTS_PER_GROUP = 512
SLOTS = N_GROUPS * SLOTS_PER_GROUP


def _router_kernel(x_ref, g_ref, sc_ref, sh_ref, wt_ref, h_ref, aff_ref):
    h = _normmod(x_ref[...], g_ref[...], sc_ref[0], sh_ref[0])
    h_ref[...] = h.astype(BF16)
    logits = lax.dot_general(wt_ref[...], h, (((1,), (1,)), ((), ())), precision=HIGHEST,
                             preferred_element_type=F32)
    e = jnp.exp(logits - jnp.max(logits, axis=0, keepdims=True))
    aff_ref[...] = e / jnp.sum(e, axis=0, keepdims=True)


def moe_route(x, g, sc, sh, w_router_t):
    return pl.pallas_call(
        _router_kernel,
        grid=(N_TOK // TM,),
        in_specs=[pl.BlockSpec((TM, D), lambda i: (i, 0)),
                  pl.BlockSpec((1, D), lambda i: (0, 0)),
                  pl.BlockSpec((1, 1, D), lambda i: (_group_of_tile(i), 0, 0)),
                  pl.BlockSpec((1, 1, D), lambda i: (_group_of_tile(i), 0, 0)),
                  pl.BlockSpec((N_EXP, D), lambda i: (0, 0))],
        out_specs=[pl.BlockSpec((TM, D), lambda i: (i, 0)),
                   pl.BlockSpec((N_EXP, TM), lambda i: (0, i))],
        out_shape=[jax.ShapeDtypeStruct((N_TOK, D), BF16),
                   jax.ShapeDtypeStruct((N_EXP, N_TOK), F32)],
        compiler_params=_cparams("parallel"),
        name="moe_router",
    )(x, g.reshape(1, D), sc, sh, w_router_t)


def _lane_prefix_excl(m):
    s, t = m.shape
    r = lax.broadcasted_iota(jnp.int32, (128, 128), 0)
    c = lax.broadcasted_iota(jnp.int32, (128, 128), 1)
    upper = (r <= c).astype(BF16)
    run = jnp.zeros((s, 1), F32)
    out = []
    for k in range(t // 128):
        blk = m[:, 128 * k:128 * (k + 1)]
        inc = jnp.dot(blk.astype(BF16), upper, preferred_element_type=F32) + run
        out.append(inc - blk)
        run = inc[:, 127:128]
    return jnp.concatenate(out, axis=1)


def _select_kernel(a_ref, off_ref, slot_ref, base_ref, *, cap):
    bits = pltpu.bitcast(a_ref[...], jnp.int32)
    s = bits.shape[0]
    capf = float(cap)

    def body(_, lohi):
        lo, hi = lohi
        mid = lo + ((hi - lo + 1) >> 1)
        cnt = jnp.sum((bits >= mid).astype(F32), axis=1, keepdims=True)
        ok = cnt >= capf
        return jnp.where(ok, mid, lo), jnp.where(ok, hi, mid - 1)

    lo0 = jnp.zeros((s, 1), jnp.int32)
    hi0 = jnp.full((s, 1), 0x7F800000, jnp.int32)
    thr, _ = lax.fori_loop(0, 31, body, (lo0, hi0))
    gt = (bits > thr).astype(F32)
    eq = (bits == thr).astype(F32)
    need = capf - jnp.sum(gt, axis=1, keepdims=True)
    sel = gt + eq * (_lane_prefix_excl(eq) < need).astype(F32)
    slot = _lane_prefix_excl(sel) + off_ref[...]
    slot_ref[...] = jnp.where(sel > 0.0, slot, -1.0).astype(jnp.int32)
    t = bits.shape[1]
    tok = lax.broadcasted_iota(jnp.int32, (t, 128), 0)
    tile = lax.broadcasted_iota(jnp.int32, (t, 128), 1)
    ahead = (tok < tile * TOK_TILE).astype(BF16)
    base_ref[...] = jnp.dot(sel.astype(BF16), ahead, preferred_element_type=F32).astype(jnp.int32)


def moe_select(aff, off, cap):
    s, t = aff.shape
    return pl.pallas_call(
        functools.partial(_select_kernel, cap=cap),
        grid=(1,),
        in_specs=[pl.BlockSpec((s, t), lambda i: (0, 0)),
                  pl.BlockSpec((s, 1), lambda i: (0, 0))],
        out_specs=[pl.BlockSpec((s, t), lambda i: (0, 0)),
                   pl.BlockSpec((s, 128), lambda i: (0, 0))],
        out_shape=[jax.ShapeDtypeStruct((s, t), jnp.int32),
                   jax.ShapeDtypeStruct((s, 128), jnp.int32)],
        compiler_params=_cparams("arbitrary"),
        name="moe_select",
    )(aff, off)


TOK_TILE = 256
TILES = GROUP_ROWS // TOK_TILE
CNT_STRIDE = TILES + 1
SLOT_BLK = 128
SLOT_ALIGN = 16
GATHER_SPAN = 8


def _gather_kernel(cnt_ref, slot_ref, aff_ref, h_ref, xs_ref, gs_ref):
    base = (pl.program_id(0) * N_EXP + pl.program_id(1)) * CNT_STRIDE

    def gather(k, t_first, n_tiles):
        s = lax.broadcasted_iota(jnp.int32, (SLOT_BLK, TOK_TILE), 0) + SLOT_BLK * k
        gate = jnp.zeros((SLOT_BLK, 1), F32)
        ohs = []
        for i in range(n_tiles):
            oh = s == slot_ref[0, 0, pl.ds(t_first + i, 1), :]
            ohs.append(oh.astype(BF16))
            gate = gate + jnp.sum(jnp.where(oh, aff_ref[0, pl.ds(t_first + i, 1), :], 0.0),
                                  axis=1, keepdims=True)
        first = t_first * TOK_TILE
        rows = pl.ds(first if isinstance(first, int) else pl.multiple_of(first, TOK_TILE),
                     n_tiles * TOK_TILE)
        out = slice(SLOT_BLK * k, SLOT_BLK * (k + 1))
        xs_ref[0, out, :] = jnp.dot(jnp.concatenate(ohs, axis=1), h_ref[rows, :],
                                    preferred_element_type=F32).astype(BF16)
        gs_ref[0, out, :] = gate

    for k in range(SLOTS_PER_GROUP // SLOT_BLK):
        before = 0
        for j in range(1, TILES + 1):
            before = before + (cnt_ref[base + j] <= SLOT_BLK * k).astype(jnp.int32)
        t0 = jnp.minimum(before, TILES - GATHER_SPAN)
        covered = cnt_ref[base + t0 + GATHER_SPAN] >= SLOT_BLK * (k + 1)

        @pl.when(covered)
        def _():
            gather(k, t0, GATHER_SPAN)

        @pl.when(jnp.logical_not(covered))
        def _():
            gather(k, 0, TILES)


def moe_gather(cnt, slot, aff3, h2):
    return pl.pallas_call(
        _gather_kernel,
        grid_spec=pltpu.PrefetchScalarGridSpec(
            num_scalar_prefetch=1,
            grid=(N_GROUPS, N_EXP),
            in_specs=[pl.BlockSpec((1, 1, TILES, TOK_TILE), lambda r, e, c: (r, e, 0, 0)),
                      pl.BlockSpec((1, TILES, TOK_TILE), lambda r, e, c: (e, r, 0)),
                      pl.BlockSpec((GROUP_ROWS, D), lambda r, e, c: (r, 0))],
            out_specs=[pl.BlockSpec((1, SLOTS_PER_GROUP, D), lambda r, e, c: (e, r, 0)),
                       pl.BlockSpec((1, SLOTS_PER_GROUP, 1), lambda r, e, c: (e, r, 0))]),
        out_shape=[jax.ShapeDtypeStruct((N_EXP, SLOTS, D), BF16),
                   jax.ShapeDtypeStruct((N_EXP, SLOTS, 1), F32)],
        compiler_params=_cparams("parallel", "parallel"),
        name="moe_gather",
    )(cnt, slot, aff3, h2)


FF_TILE = 1024
FFN_ROWS = 512


def _ffn_kernel(xs_ref, gs_ref, wg_ref, wu_ref, wd_ref, ys_ref, acc):
    f = pl.program_id(1)

    @pl.when(f == 0)
    def _():
        acc[...] = jnp.zeros_like(acc)

    wg = wg_ref[0, 0].astype(BF16)
    wu = wu_ref[0, 0].astype(BF16)
    wd = wd_ref[0, 0].astype(BF16)
    for rb in range(SLOTS // FFN_ROWS):
        rows = slice(FFN_ROWS * rb, FFN_ROWS * (rb + 1))
        x = xs_ref[0, rows, :]
        g = jnp.dot(x, wg, preferred_element_type=F32)
        u = jnp.dot(x, wu, preferred_element_type=F32)
        hid = (_silu(g) * u).astype(BF16)
        acc[rows, :] += jnp.dot(hid, wd, preferred_element_type=F32)

    @pl.when(f == FF // FF_TILE - 1)
    def _():
        ys_ref[0] = (acc[...] * gs_ref[0]).astype(BF16)


def moe_ffn(xs, gs, w_gate, w_up, w_down, layer):
    return pl.pallas_call(
        _ffn_kernel,
        grid=(N_EXP, FF // FF_TILE),
        in_specs=[pl.BlockSpec((1, SLOTS, D), lambda e, f: (e, 0, 0)),
                  pl.BlockSpec((1, SLOTS, 1), lambda e, f: (e, 0, 0)),
                  pl.BlockSpec((1, 1, D, FF_TILE), lambda e, f: (layer, e, 0, f)),
                  pl.BlockSpec((1, 1, D, FF_TILE), lambda e, f: (layer, e, 0, f)),
                  pl.BlockSpec((1, 1, FF_TILE, D), lambda e, f: (layer, e, f, 0))],
        out_specs=pl.BlockSpec((1, SLOTS, D), lambda e, f: (e, 0, 0)),
        out_shape=jax.ShapeDtypeStruct((N_EXP, SLOTS, D), BF16),
        scratch_shapes=[pltpu.VMEM((SLOTS, D), F32)],
        compiler_params=_cparams("parallel", "arbitrary"),
        name="moe_ffn",
    )(xs, gs, w_gate, w_up, w_down)


SCAT_TN = 512


def _scatter_kernel(cnt_ref, slot_ref, ys_ref, x_ref, g2_ref, o_ref):
    r, j = pl.program_id(0), pl.program_id(2)
    starts, covered = [], True
    for e in range(N_EXP):
        base = (r * N_EXP + e) * CNT_STRIDE + j
        s = jnp.minimum(cnt_ref[base] // SLOT_ALIGN * SLOT_ALIGN, SLOTS_PER_GROUP - SLOT_BLK)
        starts.append(pl.multiple_of(s, SLOT_ALIGN))
        covered = covered & (cnt_ref[base + 1] <= s + SLOT_BLK)

    def scatter(windows, n):
        lane = lax.broadcasted_iota(jnp.int32, (TOK_TILE, n), 1)
        acc = jnp.zeros((TOK_TILE, SCAT_TN), F32)
        for e in range(N_EXP):
            oh = (slot_ref[0, :, e:e + 1] == lane + windows[e]).astype(BF16)
            acc = acc + jnp.dot(oh, ys_ref[e, pl.ds(windows[e], n), :], preferred_element_type=F32)
        o_ref[...] = x_ref[...] + g2_ref[0] * acc

    @pl.when(covered)
    def _():
        scatter(starts, SLOT_BLK)

    @pl.when(jnp.logical_not(covered))
    def _():
        scatter([0] * N_EXP, SLOTS_PER_GROUP)


def moe_scatter(cnt, slot_t, ys, x, g2):
    return pl.pallas_call(
        _scatter_kernel,
        grid_spec=pltpu.PrefetchScalarGridSpec(
            num_scalar_prefetch=1,
            grid=(N_GROUPS, D // SCAT_TN, TILES),
            in_specs=[pl.BlockSpec((1, TOK_TILE, 128), lambda r, c, j, n: (r, j, 0)),
                      pl.BlockSpec((N_EXP, SLOTS_PER_GROUP, SCAT_TN), lambda r, c, j, n: (0, r, c)),
                      pl.BlockSpec((TOK_TILE, SCAT_TN), lambda r, c, j, n: (r * TILES + j, c)),
                      pl.BlockSpec((1, 1, SCAT_TN), lambda r, c, j, n: (r, 0, c))],
            out_specs=pl.BlockSpec((TOK_TILE, SCAT_TN), lambda r, c, j, n: (r * TILES + j, c))),
        out_shape=jax.ShapeDtypeStruct((N_TOK, D), F32),
        compiler_params=_cparams("parallel", "parallel", "arbitrary"),
        name="moe_scatter",
    )(cnt, slot_t, ys, x, g2)


def moe_layer(x, mods, g_ffn, layer, w_router, w_gate, w_up, w_down):
    sh2, sc2, g2 = mods
    h2, aff = moe_route(x, g_ffn, sc2, sh2, w_router.T)
    aff_ctx = aff[:, :GROUP_ROWS].reshape(N_EXP * N_CTX_SEQ, CTX_LEN)
    off_ctx = jnp.tile(jnp.arange(N_CTX_SEQ, dtype=F32) * CAP_CTX, N_EXP).reshape(-1, 1)
    slot_ctx, _ = moe_select(aff_ctx, off_ctx, CAP_CTX)
    aff_lat = jnp.transpose(aff[:, GROUP_ROWS:].reshape(N_EXP, N_LAT_SEQ, LAT_LEN), (1, 0, 2)
                            ).reshape(N_LAT_SEQ * N_EXP, LAT_LEN)
    slot_lat, base_lat = moe_select(aff_lat, jnp.zeros((N_LAT_SEQ * N_EXP, 1), F32), CAP_LAT)
    slot = jnp.concatenate([slot_ctx.reshape(1, N_EXP, GROUP_ROWS),
                            slot_lat.reshape(N_LAT_SEQ, N_EXP, LAT_LEN)], axis=0)
    slot_t = jnp.pad(jnp.transpose(slot, (0, 2, 1)), ((0, 0), (0, 0), (0, 128 - N_EXP)),
                     constant_values=-1)
    slot = slot.reshape(N_GROUPS, N_EXP, TILES, TOK_TILE)
    cnt_ctx = jnp.broadcast_to(jnp.arange(CNT_STRIDE, dtype=jnp.int32) * CAP_CTX, (1, N_EXP, CNT_STRIDE))
    cnt = jnp.concatenate([cnt_ctx, base_lat[:, :CNT_STRIDE].reshape(N_LAT_SEQ, N_EXP, CNT_STRIDE)],
                          axis=0).reshape(-1)
    xs, gs = moe_gather(cnt, slot, aff.reshape(N_EXP, N_TOK // TOK_TILE, TOK_TILE), h2)
    ys = moe_ffn(xs, gs, w_gate, w_up, w_down, layer)
    return moe_scatter(cnt, slot_t, ys, x, g2)


def _final_norm_kernel(x_ref, g_ref, ctx_ref, lat_ref):
    x = x_ref[...]
    y = x * lax.rsqrt(jnp.mean(x * x, axis=-1, keepdims=True) + NORM_EPS) * g_ref[...]
    is_ctx = pl.program_id(0) < TILES_PER_GROUP

    @pl.when(is_ctx)
    def _():
        ctx_ref[...] = y

    @pl.when(jnp.logical_not(is_ctx))
    def _():
        lat_ref[...] = y


def final_norm(x, g):
    t = TILES_PER_GROUP
    return pl.pallas_call(
        _final_norm_kernel,
        grid=(N_TOK // TM,),
        in_specs=[pl.BlockSpec((TM, D), lambda i: (i, 0)),
                  pl.BlockSpec((1, D), lambda i: (0, 0))],
        out_specs=[pl.BlockSpec((TM, D), lambda i: (jnp.minimum(i, t - 1), 0)),
                   pl.BlockSpec((TM, D), lambda i: (jnp.maximum(i - t, 0), 0))],
        out_shape=[jax.ShapeDtypeStruct((GROUP_ROWS, D), F32),
                   jax.ShapeDtypeStruct((N_TOK - GROUP_ROWS, D), F32)],
        compiler_params=_cparams("arbitrary"),
        name="final_norm",
    )(x, g.reshape(1, D))


def _grid_pos_embed():
    rows = LAT_LEN // GRID_W
    quarter = D // 4
    omega = 1.0 / (10000.0 ** (jnp.arange(quarter, dtype=F32) / quarter))
    r = jnp.arange(rows, dtype=F32)[:, None] * omega
    cl = jnp.arange(GRID_W, dtype=F32)[:, None] * omega
    emb_r = jnp.concatenate([jnp.sin(r), jnp.cos(r)], axis=-1)
    emb_c = jnp.concatenate([jnp.sin(cl), jnp.cos(cl)], axis=-1)
    emb = jnp.concatenate([jnp.broadcast_to(emb_r[:, None], (rows, GRID_W, D // 2)),
                           jnp.broadcast_to(emb_c[None], (rows, GRID_W, D // 2))], axis=-1)
    return emb.reshape(LAT_LEN, D)


def kernel(x_prompt, x_sample, state_s5_re, state_s5_im, state_hgrn, state_ssd, c, c_ctx, w_ada, b_ada, norm_mix, norm_ffn, norm_final, s5_lam_re, s5_lam_im, s5_log_dt, s5_b_re, s5_b_im, s5_c_re, s5_c_im, s5_d, s5_w_glu, s5_b_glu, hg_w_qig, hg_w_f, hg_b_f, hg_lb_logits, hg_norm, hg_w_o, ssd_w_in, ssd_conv_w, ssd_conv_b, ssd_dt_bias, ssd_a_log, ssd_d, ssd_norm, ssd_w_out, moe_router, moe_w_gate, moe_w_up, moe_w_down):
    cond8 = jnp.concatenate([c_ctx[None], c, jnp.zeros((5, D), F32)], axis=0)
    mod = ada_mod(cond8, w_ada, b_ada)
    mods = jnp.transpose(mod.reshape(DEPTH, 8, 6, D)[:, :3], (0, 2, 1, 3)).reshape(DEPTH, 6, 3, 1, D)
    x = embed_tokens(x_prompt.reshape(-1, D), x_sample.reshape(-1, D), _grid_pos_embed())
    s5_re, s5_im, hg_fin, ssd_fin = [], [], [], []
    for i in range(DEPTH):
        mix_mods = (mods[i, 0], mods[i, 1], mods[i, 2])
        kind, j = i % 3, i // 3
        if kind == 0:
            x, fr, fi = s5_layer(x, mix_mods, norm_mix[i], state_s5_re[:, j], state_s5_im[:, j],
                                 s5_lam_re[j], s5_lam_im[j], s5_log_dt[j], s5_b_re[j], s5_b_im[j],
                                 s5_c_re[j], s5_c_im[j], s5_d[j], s5_w_glu[j], s5_b_glu[j])
            s5_re.append(fr)
            s5_im.append(fi)
        elif kind == 1:
            x, fh = hgrn_layer(x, mix_mods, norm_mix[i], state_hgrn[:, j], i, hg_lb_logits,
                               hg_w_qig[j], hg_w_f[j], hg_b_f[j], hg_norm[j], hg_w_o[j])
            hg_fin.append(fh)
        else:
            x, fs = ssd_layer(x, mix_mods, norm_mix[i], state_ssd[:, j], ssd_w_in[j], ssd_conv_w[j],
                              ssd_conv_b[j], ssd_dt_bias[j], ssd_a_log[j], ssd_d[j], ssd_norm[j],
                              ssd_w_out[j])
            ssd_fin.append(fs)
        x = moe_layer(x, (mods[i, 3], mods[i, 4], mods[i, 5]), norm_ffn[i], i, moe_router[i],
                      moe_w_gate, moe_w_up, moe_w_down)
    y_ctx, y_lat = final_norm(x, norm_final)
    return (y_ctx.reshape(N_CTX_SEQ, CTX_LEN, D), y_lat.reshape(N_LAT_SEQ, LAT_LEN, D),
            jnp.stack(s5_re, axis=1), jnp.stack(s5_im, axis=1),
            jnp.stack(hg_fin, axis=1), jnp.stack(ssd_fin, axis=1))
```
